```python
import math
import jax, jax.numpy as jnp
from jax import lax
import numpy as np

D_MODEL = 2048
BATCH = 1
SEQ = 8192
DEPTH = 1

CHUNK = 64
MIX_WIDTH = D_MODEL
MIX_A = MIX_WIDTH // 2
MIX_B = MIX_WIDTH - MIX_A
POOL_WINDOWS = (2, 4, 8, 16)
N_POOL_GROUPS = len(POOL_WINDOWS)
POOL_CH = MIX_A // N_POOL_GROUPS
CONV_HEADS = 8
CONV_HEAD_DIM = MIX_B // CONV_HEADS
CONV_W = 3
IN_COLS = MIX_A + 3 * MIX_B
N_GROUPS = 4
E_PER_GROUP = 8
N_EXPERTS = N_GROUPS * E_PER_GROUP
TOP_K = 2
D_EXPERT = D_MODEL // 2
BLK = 128
EPS = 1e-6

kernel_name = "hybrid_pool_shortconv_hmoe"


def rmsnorm(x, g):
    xf = x.astype(jnp.float32)
    y = xf * lax.rsqrt(jnp.mean(xf * xf, axis=-1, keepdims=True) + EPS)
    return (y * g.astype(jnp.float32)).astype(x.dtype)


def multiscale_pool(u, w_pool, pool_scale):
    bt, s, _ = u.shape
    uf = u.astype(jnp.float32).reshape(bt, s, N_POOL_GROUPS, POOL_CH)
    cs = jnp.cumsum(uf, axis=1)
    steps = jnp.arange(1, s + 1)
    outs = []
    for gi, w in enumerate(POOL_WINDOWS):
        c = cs[:, :, gi]
        lagged = jnp.pad(c, ((0, 0), (w, 0), (0, 0)))[:, :s]
        cnt = jnp.minimum(steps, w).astype(jnp.float32)[None, :, None]
        outs.append((c - lagged) / cnt - uf[:, :, gi])
    pooled = jnp.stack(outs, axis=2).astype(u.dtype)
    mixed = jnp.einsum('bsgc,gcd->bsgd', pooled, w_pool)
    return mixed.reshape(bt, s, MIX_A) * pool_scale


def short_gated_conv(b_gate, c_gate, v, conv_w):
    s = v.shape[1]
    z = c_gate * v
    zp = jnp.pad(z, ((0, 0), (CONV_W - 1, 0), (0, 0)))
    y = zp[:, 0:s] * conv_w[:, 0]
    for k in range(1, CONV_W):
        y = y + zp[:, k:k + s] * conv_w[:, k]
    return b_gate * y


def hierarchical_moe(h, w_rg, b_rg, w_re, b_re, w_gate, w_up, w_down):
    bt, s, d = h.shape
    t = bt * s
    ht = h.reshape(t, d)
    g_prob = jax.nn.softmax((ht @ w_rg).astype(jnp.float32) + b_rg.astype(jnp.float32), axis=-1)
    grp = jnp.argmax(g_prob, axis=-1)
    g_w = jnp.take_along_axis(g_prob, grp[:, None], axis=-1)[:, 0]
    e_logits = ((ht @ w_re).astype(jnp.float32) + b_re.astype(jnp.float32)).reshape(t, N_GROUPS, E_PER_GROUP)
    e_sel = jnp.take_along_axis(e_logits, grp[:, None, None], axis=1)[:, 0]
    e_prob = jax.nn.softmax(e_sel, axis=-1)
    top_p, top_i = lax.top_k(e_prob, TOP_K)
    top_p = top_p / jnp.sum(top_p, axis=-1, keepdims=True)
    weights = g_w[:, None] * top_p
    expert_id = grp[:, None] * E_PER_GROUP + top_i

    a = t * TOP_K
    e_flat = expert_id.reshape(a).astype(jnp.int32)
    w_flat = weights.reshape(a)
    tok_flat = jnp.repeat(jnp.arange(t, dtype=jnp.int32), TOP_K)
    order = jnp.argsort(e_flat, stable=True)
    e_s, tok_s, w_s = e_flat[order], tok_flat[order], w_flat[order]
    counts = jnp.bincount(e_flat, length=N_EXPERTS)
    starts = jnp.cumsum(counts) - counts
    padded = ((counts + BLK - 1) // BLK) * BLK
    pstarts = jnp.cumsum(padded) - padded
    pends = pstarts + padded
    dest = pstarts[e_s] + (jnp.arange(a, dtype=jnp.int32) - starts[e_s])
    n_blocks = (a + N_EXPERTS * (BLK - 1) + BLK - 1) // BLK
    n_pad = n_blocks * BLK
    xbuf = jnp.zeros((n_pad, d), h.dtype).at[dest].set(ht[tok_s])
    blk_start = jnp.arange(n_blocks, dtype=jnp.int32) * BLK
    blk_expert = jnp.clip(jnp.sum(pends[None, :] <= blk_start[:, None], axis=1), 0, N_EXPERTS - 1)

    def expert_block(args):
        xb, e = args
        hid = jax.nn.silu(xb @ w_gate[e]) * (xb @ w_up[e])
        return hid @ w_down[e]

    ybuf = lax.map(expert_block, (xbuf.reshape(n_blocks, BLK, d), blk_expert)).reshape(n_pad, d)
    y_s = ybuf[dest] * w_s[:, None].astype(h.dtype)
    out = jnp.zeros((t, d), h.dtype).at[tok_s].add(y_s)
    return out.reshape(bt, s, d)


def setup_inputs(seed: int = 0) -> dict:
    key = jax.random.key(seed)
    ks = jax.random.split(key, 20)
    f32 = jnp.float32
    nrm = lambda k, shape, scale: jax.random.normal(k, shape, f32) * scale
    return {
        "x": nrm(ks[0], (BATCH, SEQ, D_MODEL), 1.0),
        "norm_mix_g": 1.0 + nrm(ks[1], (DEPTH, D_MODEL), 0.05),
        "w_in": nrm(ks[2], (DEPTH, D_MODEL, IN_COLS), D_MODEL ** -0.5),
        "w_pool": nrm(ks[3], (DEPTH, N_POOL_GROUPS, POOL_CH, POOL_CH), POOL_CH ** -0.5),
        "pool_scale": 1.0 + nrm(ks[4], (DEPTH, MIX_A), 0.1),
        "conv_w": nrm(ks[5], (DEPTH, MIX_B, CONV_W), CONV_W ** -0.5),
        "w_out": nrm(ks[6], (DEPTH, MIX_WIDTH, D_MODEL), MIX_WIDTH ** -0.5),
        "norm_ffn_g": 1.0 + nrm(ks[7], (DEPTH, D_MODEL), 0.05),
        "w_router_group": nrm(ks[8], (DEPTH, D_MODEL, N_GROUPS), D_MODEL ** -0.5),
        "b_router_group": nrm(ks[9], (DEPTH, N_GROUPS), 0.01),
        "w_router_expert": nrm(ks[10], (DEPTH, D_MODEL, N_EXPERTS), D_MODEL ** -0.5),
        "b_router_expert": nrm(ks[11], (DEPTH, N_EXPERTS), 0.01),
        "w_gate": nrm(ks[12], (DEPTH, N_EXPERTS, D_MODEL, D_EXPERT), D_MODEL ** -0.5),
        "w_up": nrm(ks[13], (DEPTH, N_EXPERTS, D_MODEL, D_EXPERT), D_MODEL ** -0.5),
        "w_down": nrm(ks[14], (DEPTH, N_EXPERTS, D_EXPERT, D_MODEL), D_EXPERT ** -0.5),
        "norm_final_g": 1.0 + nrm(ks[15], (D_MODEL,), 0.05),
    }


def reference(x, norm_mix_g, w_in, w_pool, pool_scale, conv_w, w_out, norm_ffn_g,
              w_router_group, b_router_group, w_router_expert, b_router_expert,
              w_gate, w_up, w_down, norm_final_g):
    for l in range(DEPTH):
        h = rmsnorm(x, norm_mix_g[l])
        proj = jnp.einsum('bsd,dc->bsc', h, w_in[l])
        u_pool = proj[..., :MIX_A]
        b_gate = proj[..., MIX_A:MIX_A + MIX_B]
        c_gate = proj[..., MIX_A + MIX_B:MIX_A + 2 * MIX_B]
        v = proj[..., MIX_A + 2 * MIX_B:]
        y_a = multiscale_pool(u_pool, w_pool[l], pool_scale[l])
        y_b = short_gated_conv(b_gate, c_gate, v, conv_w[l])
        mixed = jnp.concatenate([y_a, y_b], axis=-1)
        x = x + jnp.einsum('bsc,cd->bsd', mixed, w_out[l])
        h2 = rmsnorm(x, norm_ffn_g[l])
        x = x + hierarchical_moe(h2, w_router_group[l], b_router_group[l],
                                 w_router_expert[l], b_router_expert[l],
                                 w_gate[l], w_up[l], w_down[l])
    return rmsnorm(x, norm_final_g)
```

```python
import functools

import jax
import jax.numpy as jnp
from jax import lax
from jax.experimental import pallas as pl
from jax.experimental.pallas import tpu as pltpu

D_MODEL = 2048
MIX_A = 1024
MIX_B = 1024
POOL_WINDOWS = (2, 4, 8, 16)
POOL_CH = MIX_A // len(POOL_WINDOWS)
CONV_W = 3
N_GROUPS = 4
E_PER_GROUP = 8
N_EXPERTS = N_GROUPS * E_PER_GROUP
TOP_K = 2
D_EXPERT = D_MODEL // 2
EPS = 1e-6

LANES = 128
HIST = 16
TM = 256
ROW_BLK = 256
VMEM_LIMIT = 56 * 1024 * 1024

_NEG = -1e30


def _rms(x, g):
    return x * lax.rsqrt(jnp.mean(x * x, axis=-1, keepdims=True) + EPS) * g


def _bdot(a, b):
    return jnp.dot(a, b, preferred_element_type=jnp.float32)


def _mixer_kernel(x_ref, g1_ref, w_in_ref, w_pool_ref, pscale_ref, convw_ref, w_out_ref,
                  g2_ref, w_r_ref, b_r_ref,
                  x1_ref, h2_ref, route_ref,
                  ext_u, ext_z):
    i = pl.program_id(0)

    @pl.when(i == 0)
    def _():
        ext_u[0:HIST, :] = jnp.zeros((HIST, MIX_A), jnp.float32)
        ext_z[0:HIST, :] = jnp.zeros((HIST, MIX_B), jnp.float32)

    x = x_ref[...]
    h = _rms(x, g1_ref[...]).astype(jnp.bfloat16)

    ext_u[HIST:HIST + TM, :] = _bdot(h, w_in_ref[:, 0:MIX_A])
    row = lax.broadcasted_iota(jnp.int32, (TM, 1), 0) + i * TM + 1
    y_a = []
    for gi, w in enumerate(POOL_WINDOWS):
        c0 = gi * POOL_CH
        u = ext_u[HIST:HIST + TM, c0:c0 + POOL_CH]
        acc = u
        for s in range(1, w):
            acc = acc + ext_u[HIST - s:HIST - s + TM, c0:c0 + POOL_CH]
        cnt = jnp.minimum(row, w).astype(jnp.float32)
        pooled = (acc / cnt - u).astype(jnp.bfloat16)
        y_a.append(_bdot(pooled, w_pool_ref[gi]))
    y_a = jnp.concatenate(y_a, axis=-1) * pscale_ref[...]
    ext_u[0:HIST, :] = ext_u[TM:TM + HIST, :]

    b_gate = _bdot(h, w_in_ref[:, MIX_A:MIX_A + MIX_B])
    c_gate = _bdot(h, w_in_ref[:, MIX_A + MIX_B:MIX_A + 2 * MIX_B])
    v = _bdot(h, w_in_ref[:, MIX_A + 2 * MIX_B:MIX_A + 3 * MIX_B])
    z = c_gate * v
    ext_z[HIST:HIST + TM, :] = z
    y = ext_z[HIST - 2:HIST - 2 + TM, :] * convw_ref[0:1, :]
    y = y + ext_z[HIST - 1:HIST - 1 + TM, :] * convw_ref[1:2, :]
    y = y + z * convw_ref[2:3, :]
    y_b = b_gate * y
    ext_z[0:HIST, :] = ext_z[TM:TM + HIST, :]

    mixed = jnp.concatenate([y_a, y_b], axis=-1).astype(jnp.bfloat16)
    x1 = x + _bdot(mixed, w_out_ref[...])
    x1_ref[...] = x1

    h2 = _rms(x1, g2_ref[...]).astype(jnp.bfloat16)
    h2_ref[...] = h2

    logits = _bdot(h2, w_r_ref[...]) + b_r_ref[...]
    lane = lax.broadcasted_iota(jnp.int32, (TM, LANES), 1)
    lane_f = lane.astype(jnp.float32)

    def first_argmax(vals, vmax):
        return jnp.min(jnp.where(vals == vmax, lane_f, float(LANES)), axis=-1, keepdims=True)

    gl = jnp.where(lane < N_GROUPS, logits, _NEG)
    gmax = jnp.max(gl, axis=-1, keepdims=True)
    g_w = 1.0 / jnp.sum(jnp.exp(gl - gmax), axis=-1, keepdims=True)
    grp = first_argmax(gl, gmax).astype(jnp.int32)
    lo = N_GROUPS + E_PER_GROUP * grp
    el = jnp.where((lane >= lo) & (lane < lo + E_PER_GROUP), logits, _NEG)
    emax = jnp.max(el, axis=-1, keepdims=True)
    idx1 = first_argmax(el, emax)
    esum = jnp.sum(jnp.exp(el - emax), axis=-1, keepdims=True)
    el2 = jnp.where(lane_f == idx1, _NEG, el)
    e2max = jnp.max(el2, axis=-1, keepdims=True)
    idx2 = first_argmax(el2, e2max)
    p1 = 1.0 / esum
    p2 = jnp.exp(e2max - emax) / esum
    tot = p1 + p2
    w1 = g_w * (p1 / tot)
    w2 = g_w * (p2 / tot)
    route = jnp.where(lane == 0, w1, 0.0)
    route = jnp.where(lane == 1, w2, route)
    route = jnp.where(lane == 2, idx1 - N_GROUPS, route)
    route = jnp.where(lane == 3, idx2 - N_GROUPS, route)
    route_ref[...] = route


def _mixer(x, g1, w_in, w_pool, pscale, convw, w_out, g2, w_r, b_r):
    t = x.shape[0]
    const = lambda shape: pl.BlockSpec(shape, lambda i: (0,) * len(shape),
                                       pipeline_mode=pl.Buffered(1))
    return pl.pallas_call(
        _mixer_kernel,
        grid=(t // TM,),
        in_specs=[
            pl.BlockSpec((TM, D_MODEL), lambda i: (i, 0)),
            const((1, D_MODEL)),
            const(w_in.shape),
            const(w_pool.shape),
            const((1, MIX_A)),
            const((CONV_W, MIX_B)),
            const(w_out.shape),
            const((1, D_MODEL)),
            const(w_r.shape),
            const((1, LANES)),
        ],
        out_specs=[
            pl.BlockSpec((TM, D_MODEL), lambda i: (i, 0)),
            pl.BlockSpec((TM, D_MODEL), lambda i: (i, 0)),
            pl.BlockSpec((TM, LANES), lambda i: (i, 0)),
        ],
        out_shape=[
            jax.ShapeDtypeStruct((t, D_MODEL), jnp.float32),
            jax.ShapeDtypeStruct((t, D_MODEL), jnp.bfloat16),
            jax.ShapeDtypeStruct((t, LANES), jnp.float32),
        ],
        scratch_shapes=[
            pltpu.VMEM((TM + HIST, MIX_A), jnp.float32),
            pltpu.VMEM((TM + HIST, MIX_B), jnp.float32),
        ],
        compiler_params=pltpu.CompilerParams(
            dimension_semantics=("arbitrary",), vmem_limit_bytes=VMEM_LIMIT),
        name="mixer",
    )(x, g1, w_in, w_pool, pscale, convw, w_out, g2, w_r, b_r)


def _expert_kernel(blk_e_ref, n_used_ref, x_ref, wg_ref, wu_ref, wd_ref, y_ref):
    b = pl.program_id(0)

    @pl.when(b < n_used_ref[0])
    def _():
        xb = x_ref[...]
        gate = _bdot(xb, wg_ref[0])
        up = _bdot(xb, wu_ref[0])
        hid = (gate * jax.nn.sigmoid(gate) * up).astype(jnp.bfloat16)
        y_ref[...] = _bdot(hid, wd_ref[0])


def _experts(blk_e, n_used, xbuf, wg, wu, wd):
    n_pad = xbuf.shape[0]
    n_blocks = n_pad // ROW_BLK
    return pl.pallas_call(
        _expert_kernel,
        grid_spec=pltpu.PrefetchScalarGridSpec(
            num_scalar_prefetch=2,
            grid=(n_blocks,),
            in_specs=[
                pl.BlockSpec((ROW_BLK, D_MODEL), lambda b, be, nu: (b, 0)),
                pl.BlockSpec((1, D_MODEL, D_EXPERT), lambda b, be, nu: (be[b], 0, 0)),
                pl.BlockSpec((1, D_MODEL, D_EXPERT), lambda b, be, nu: (be[b], 0, 0)),
                pl.BlockSpec((1, D_EXPERT, D_MODEL), lambda b, be, nu: (be[b], 0, 0)),
            ],
            out_specs=pl.BlockSpec((ROW_BLK, D_MODEL), lambda b, be, nu: (b, 0)),
        ),
        out_shape=jax.ShapeDtypeStruct((n_pad, D_MODEL), jnp.float32),
        compiler_params=pltpu.CompilerParams(
            dimension_semantics=("arbitrary",), vmem_limit_bytes=VMEM_LIMIT),
        name="experts",
    )(blk_e, n_used, xbuf, wg, wu, wd)


def _combine_kernel(x1_ref, y0_ref, y1_ref, route_ref, g_ref, o_ref):
    r = route_ref[...]
    xo = x1_ref[...] + (y0_ref[...] * r[:, 0:1] + y1_ref[...] * r[:, 1:2])
    o_ref[...] = _rms(xo, g_ref[...])


def _combine(x1, y0, y1, route, g):
    t = x1.shape[0]
    tile = lambda w: pl.BlockSpec((TM, w), lambda i: (i, 0))
    return pl.pallas_call(
        _combine_kernel,
        grid=(t // TM,),
        in_specs=[tile(D_MODEL), tile(D_MODEL), tile(D_MODEL), tile(LANES),
                  pl.BlockSpec((1, D_MODEL), lambda i: (0, 0))],
        out_specs=tile(D_MODEL),
        out_shape=jax.ShapeDtypeStruct((t, D_MODEL), jnp.float32),
        compiler_params=pltpu.CompilerParams(
            dimension_semantics=("arbitrary",), vmem_limit_bytes=VMEM_LIMIT),
        name="combine",
    )(x1, y0, y1, route, g)


def _dispatch_plan(expert_id):
    t = expert_id.shape[0]
    a = t * TOP_K
    e_flat = expert_id.reshape(a)
    onehot = (e_flat[:, None] == jnp.arange(N_EXPERTS, dtype=jnp.int32)[None, :]).astype(jnp.int32)
    rank = jnp.cumsum(onehot, axis=0) - onehot
    pos = jnp.sum(rank * onehot, axis=1)
    counts = jnp.sum(onehot, axis=0)
    padded = ((counts + ROW_BLK - 1) // ROW_BLK) * ROW_BLK
    pends = jnp.cumsum(padded)
    pstarts = pends - padded
    dest = pstarts[e_flat] + pos
    n_blocks = (a + N_EXPERTS * (ROW_BLK - 1) + ROW_BLK - 1) // ROW_BLK
    blk_start = jnp.arange(n_blocks, dtype=jnp.int32) * ROW_BLK
    blk_e = jnp.clip(jnp.sum(pends[None, :] <= blk_start[:, None], axis=1), 0, N_EXPERTS - 1)
    n_used = (pends[-1] // ROW_BLK).astype(jnp.int32).reshape(1)
    return dest.astype(jnp.int32), blk_e.astype(jnp.int32), n_used, n_blocks


def kernel(x, norm_mix_g, w_in, w_pool, pool_scale, conv_w, w_out, norm_ffn_g, w_router_group, b_router_group, w_router_expert, b_router_expert, w_gate, w_up, w_down, norm_final_g):
    bt, s, d = x.shape
    t = bt * s
    bf = jnp.bfloat16
    xt = x.reshape(t, d)
    assert norm_mix_g.shape[0] == 1, "the final norm is fused into the last layer's combine"
    for l in range(norm_mix_g.shape[0]):
        w_r = jnp.concatenate([w_router_group[l], w_router_expert[l]], axis=1)
        w_r = jnp.pad(w_r, ((0, 0), (0, LANES - w_r.shape[1]))).astype(bf)
        b_r = jnp.concatenate([b_router_group[l], b_router_expert[l]])
        b_r = jnp.pad(b_r, (0, LANES - b_r.shape[0])).reshape(1, LANES)
        x1, h2, route = _mixer(
            xt, norm_mix_g[l].reshape(1, d), w_in[l].astype(bf), w_pool[l].astype(bf),
            pool_scale[l].reshape(1, MIX_A), conv_w[l].T, w_out[l].astype(bf),
            norm_ffn_g[l].reshape(1, d), w_r, b_r)
        expert_id = route[:, 2:4].astype(jnp.int32)
        dest, blk_e, n_used, n_blocks = _dispatch_plan(expert_id)
        tok = jnp.repeat(jnp.arange(t, dtype=jnp.int32), TOP_K)
        row_tok = jnp.zeros((n_blocks * ROW_BLK,), jnp.int32).at[dest].set(tok)
        xbuf = h2[row_tok]
        ybuf = _experts(blk_e, n_used, xbuf, w_gate[l].astype(bf), w_up[l].astype(bf),
                        w_down[l].astype(bf))
        dest2 = dest.reshape(t, TOP_K)
        xt = _combine(x1, ybuf[dest2[:, 0]], ybuf[dest2[:, 1]], route,
                      norm_final_g.reshape(1, d))
    return xt.reshape(bt, s, d)
```

```python
import functools

import jax
import jax.numpy as jnp
from jax import lax
from jax.experimental import pallas as pl
from jax.experimental.pallas import tpu as pltpu

D_MODEL = 2048
MIX_A = 1024
MIX_B = 1024
POOL_WINDOWS = (2, 4, 8, 16)
POOL_CH = MIX_A // len(POOL_WINDOWS)
CONV_W = 3
N_GROUPS = 4
E_PER_GROUP = 8
N_EXPERTS = N_GROUPS * E_PER_GROUP
TOP_K = 2
D_EXPERT = D_MODEL // 2
EPS = 1e-6

LANES = 128
HIST = 16
TM = 256
ROW_BLK = 256
VMEM_LIMIT = 56 * 1024 * 1024

_NEG = -1e30


def _rms(x, g):
    return x * lax.rsqrt(jnp.mean(x * x, axis=-1, keepdims=True) + EPS) * g


def _bdot(a, b):
    return jnp.dot(a, b, preferred_element_type=jnp.float32)


def _mixer_kernel(x_ref, g1_ref, w_in_ref, w_pool_ref, pscale_ref, convw_ref, w_out_ref,
                  g2_ref, w_r_ref, b_r_ref,
                  x1_ref, h2_ref, route_ref,
                  ext_u, ext_z):
    i = pl.program_id(0)

    @pl.when(i == 0)
    def _():
        ext_u[0:HIST, :] = jnp.zeros((HIST, MIX_A), jnp.float32)
        ext_z[0:HIST, :] = jnp.zeros((HIST, MIX_B), jnp.float32)

    x = x_ref[...]
    h = _rms(x, g1_ref[...]).astype(jnp.bfloat16)

    ext_u[HIST:HIST + TM, :] = _bdot(h, w_in_ref[:, 0:MIX_A])
    row = lax.broadcasted_iota(jnp.int32, (TM, 1), 0) + i * TM + 1
    y_a = []
    for gi, w in enumerate(POOL_WINDOWS):
        c0 = gi * POOL_CH
        u = ext_u[HIST:HIST + TM, c0:c0 + POOL_CH]
        acc = u
        for s in range(1, w):
            acc = acc + ext_u[HIST - s:HIST - s + TM, c0:c0 + POOL_CH]
        cnt = jnp.minimum(row, w).astype(jnp.float32)
        pooled = (acc / cnt - u).astype(jnp.bfloat16)
        y_a.append(_bdot(pooled, w_pool_ref[gi]))
    y_a = jnp.concatenate(y_a, axis=-1) * pscale_ref[...]
    ext_u[0:HIST, :] = ext_u[TM:TM + HIST, :]

    b_gate = _bdot(h, w_in_ref[:, MIX_A:MIX_A + MIX_B])
    c_gate = _bdot(h, w_in_ref[:, MIX_A + MIX_B:MIX_A + 2 * MIX_B])
    v = _bdot(h, w_in_ref[:, MIX_A + 2 * MIX_B:MIX_A + 3 * MIX_B])
    z = c_gate * v
    ext_z[HIST:HIST + TM, :] = z
    y = ext_z[HIST - 2:HIST - 2 + TM, :] * convw_ref[0:1, :]
    y = y + ext_z[HIST - 1:HIST - 1 + TM, :] * convw_ref[1:2, :]
    y = y + z * convw_ref[2:3, :]
    y_b = b_gate * y
    ext_z[0:HIST, :] = ext_z[TM:TM + HIST, :]

    mixed = jnp.concatenate([y_a, y_b], axis=-1).astype(jnp.bfloat16)
    x1 = x + _bdot(mixed, w_out_ref[...])
    x1_ref[...] = x1

    h2_f32 = _rms(x1, g2_ref[...])
    h2_ref[...] = h2_f32
    h2 = h2_f32.astype(jnp.bfloat16)

    logits = _bdot(h2, w_r_ref[...]) + b_r_ref[...]
    lane = lax.broadcasted_iota(jnp.int32, (TM, LANES), 1)
    lane_f = lane.astype(jnp.float32)

    def first_argmax(vals, vmax):
        return jnp.min(jnp.where(vals == vmax, lane_f, float(LANES)), axis=-1, keepdims=True)

    gl = jnp.where(lane < N_GROUPS, logits, _NEG)
    gmax = jnp.max(gl, axis=-1, keepdims=True)
    g_w = 1.0 / jnp.sum(jnp.exp(gl - gmax), axis=-1, keepdims=True)
    grp = first_argmax(gl, gmax).astype(jnp.int32)
    lo = N_GROUPS + E_PER_GROUP * grp
    el = jnp.where((lane >= lo) & (lane < lo + E_PER_GROUP), logits, _NEG)
    emax = jnp.max(el, axis=-1, keepdims=True)
    idx1 = first_argmax(el, emax)
    esum = jnp.sum(jnp.exp(el - emax), axis=-1, keepdims=True)
    el2 = jnp.where(lane_f == idx1, _NEG, el)
    e2max = jnp.max(el2, axis=-1, keepdims=True)
    idx2 = first_argmax(el2, e2max)
    p1 = 1.0 / esum
    p2 = jnp.exp(e2max - emax) / esum
    tot = p1 + p2
    w1 = g_w * (p1 / tot)
    w2 = g_w * (p2 / tot)
    route = jnp.where(lane == 0, w1, 0.0)
    route = jnp.where(lane == 1, w2, route)
    route = jnp.where(lane == 2, idx1 - N_GROUPS, route)
    route = jnp.where(lane == 3, idx2 - N_GROUPS, route)
    route_ref[...] = route


def _mixer(x, g1, w_in, w_pool, pscale, convw, w_out, g2, w_r, b_r):
    t = x.shape[0]
    const = lambda shape: pl.BlockSpec(shape, lambda i: (0,) * len(shape),
                                       pipeline_mode=pl.Buffered(1))
    return pl.pallas_call(
        _mixer_kernel,
        grid=(t // TM,),
        in_specs=[
            pl.BlockSpec((TM, D_MODEL), lambda i: (i, 0)),
            const((1, D_MODEL)),
            const(w_in.shape),
            const(w_pool.shape),
            const((1, MIX_A)),
            const((CONV_W, MIX_B)),
            const(w_out.shape),
            const((1, D_MODEL)),
            const(w_r.shape),
            const((1, LANES)),
        ],
        out_specs=[
            pl.BlockSpec((TM, D_MODEL), lambda i: (i, 0)),
            pl.BlockSpec((TM, D_MODEL), lambda i: (i, 0)),
            pl.BlockSpec((TM, LANES), lambda i: (i, 0)),
        ],
        out_shape=[
            jax.ShapeDtypeStruct((t, D_MODEL), jnp.float32),
            jax.ShapeDtypeStruct((t, D_MODEL), jnp.float32),
            jax.ShapeDtypeStruct((t, LANES), jnp.float32),
        ],
        scratch_shapes=[
            pltpu.VMEM((TM + HIST, MIX_A), jnp.float32),
            pltpu.VMEM((TM + HIST, MIX_B), jnp.float32),
        ],
        compiler_params=pltpu.CompilerParams(
            dimension_semantics=("arbitrary",), vmem_limit_bytes=VMEM_LIMIT),
        name="mixer",
    )(x, g1, w_in, w_pool, pscale, convw, w_out, g2, w_r, b_r)


def _start_row_gather(src_hbm, idx_ref, base, dst_ref, sem, n_rows):
    for r in range(n_rows):
        pltpu.make_async_copy(src_hbm.at[pl.ds(idx_ref[base + r], 1)],
                              dst_ref.at[pl.ds(r, 1)], sem).start()


def _wait_row_gather(src_hbm, dst_ref, sem, n_rows):
    pltpu.make_async_copy(src_hbm.at[pl.ds(0, n_rows)], dst_ref, sem).wait()


def _dispatch_kernel(row_tok_ref, n_used_ref, h2_hbm, x_ref, sem):
    b = pl.program_id(0)

    @pl.when(b < n_used_ref[0])
    def _():
        _start_row_gather(h2_hbm, row_tok_ref, b * ROW_BLK, x_ref, sem, ROW_BLK)
        _wait_row_gather(h2_hbm, x_ref, sem, ROW_BLK)

    @pl.when(b >= n_used_ref[0])
    def _():
        x_ref[...] = jnp.zeros_like(x_ref)


def _dispatch(row_tok, n_used, h2):
    n_pad = row_tok.shape[0]
    return pl.pallas_call(
        _dispatch_kernel,
        grid_spec=pltpu.PrefetchScalarGridSpec(
            num_scalar_prefetch=2,
            grid=(n_pad // ROW_BLK,),
            in_specs=[pl.BlockSpec(memory_space=pl.ANY)],
            out_specs=pl.BlockSpec((ROW_BLK, D_MODEL), lambda b, rt, nu: (b, 0)),
            scratch_shapes=[pltpu.SemaphoreType.DMA],
        ),
        out_shape=jax.ShapeDtypeStruct((n_pad, D_MODEL), h2.dtype),
        compiler_params=pltpu.CompilerParams(
            dimension_semantics=("arbitrary",), vmem_limit_bytes=VMEM_LIMIT),
        name="dispatch",
    )(row_tok, n_used, h2)


def _expert_changed(blk_e_ref, b):
    return (b == 0) | (blk_e_ref[b] != blk_e_ref[jnp.maximum(b - 1, 0)])


def _gate_up_kernel(blk_e_ref, n_used_ref, x_ref, wg_ref, wu_ref, hid_ref, wg_bf, wu_bf):
    b = pl.program_id(0)

    @pl.when(b < n_used_ref[0])
    def _():
        @pl.when(_expert_changed(blk_e_ref, b))
        def _():
            wg_bf[...] = wg_ref[0].astype(jnp.bfloat16)
            wu_bf[...] = wu_ref[0].astype(jnp.bfloat16)

        xb = x_ref[...].astype(jnp.bfloat16)
        gate = _bdot(xb, wg_bf[...])
        up = _bdot(xb, wu_bf[...])
        hid_ref[...] = (gate * jax.nn.sigmoid(gate) * up).astype(jnp.bfloat16)

    @pl.when(b >= n_used_ref[0])
    def _():
        hid_ref[...] = jnp.zeros_like(hid_ref)


def _gate_up(blk_e, n_used, xbuf, wg, wu):
    n_pad = xbuf.shape[0]
    w_spec = pl.BlockSpec((1, D_MODEL, D_EXPERT), lambda b, be, nu: (be[b], 0, 0))
    return pl.pallas_call(
        _gate_up_kernel,
        grid_spec=pltpu.PrefetchScalarGridSpec(
            num_scalar_prefetch=2,
            grid=(n_pad // ROW_BLK,),
            in_specs=[pl.BlockSpec((ROW_BLK, D_MODEL), lambda b, be, nu: (b, 0)), w_spec, w_spec],
            out_specs=pl.BlockSpec((ROW_BLK, D_EXPERT), lambda b, be, nu: (b, 0)),
            scratch_shapes=[pltpu.VMEM((D_MODEL, D_EXPERT), jnp.bfloat16),
                            pltpu.VMEM((D_MODEL, D_EXPERT), jnp.bfloat16)],
        ),
        out_shape=jax.ShapeDtypeStruct((n_pad, D_EXPERT), jnp.bfloat16),
        compiler_params=pltpu.CompilerParams(
            dimension_semantics=("arbitrary",), vmem_limit_bytes=VMEM_LIMIT),
        name="gate_up",
    )(blk_e, n_used, xbuf, wg, wu)


def _down_kernel(blk_e_ref, n_used_ref, hid_ref, wd_ref, y_ref, wd_bf):
    b = pl.program_id(0)

    @pl.when(b < n_used_ref[0])
    def _():
        @pl.when(_expert_changed(blk_e_ref, b))
        def _():
            wd_bf[...] = wd_ref[0].astype(jnp.bfloat16)

        y_ref[...] = _bdot(hid_ref[...], wd_bf[...])

    @pl.when(b >= n_used_ref[0])
    def _():
        y_ref[...] = jnp.zeros_like(y_ref)


def _down(blk_e, n_used, hid, wd):
    n_pad = hid.shape[0]
    return pl.pallas_call(
        _down_kernel,
        grid_spec=pltpu.PrefetchScalarGridSpec(
            num_scalar_prefetch=2,
            grid=(n_pad // ROW_BLK,),
            in_specs=[pl.BlockSpec((ROW_BLK, D_EXPERT), lambda b, be, nu: (b, 0)),
                      pl.BlockSpec((1, D_EXPERT, D_MODEL), lambda b, be, nu: (be[b], 0, 0))],
            out_specs=pl.BlockSpec((ROW_BLK, D_MODEL), lambda b, be, nu: (b, 0)),
            scratch_shapes=[pltpu.VMEM((D_EXPERT, D_MODEL), jnp.bfloat16)],
        ),
        out_shape=jax.ShapeDtypeStruct((n_pad, D_MODEL), jnp.float32),
        compiler_params=pltpu.CompilerParams(
            dimension_semantics=("arbitrary",), vmem_limit_bytes=VMEM_LIMIT),
        name="down",
    )(blk_e, n_used, hid, wd)


def _combine_kernel(d0_ref, d1_ref, x1_ref, route_ref, g_ref, y_hbm, o_ref, y0_buf, y1_buf, sem):
    i = pl.program_id(0)
    _start_row_gather(y_hbm, d0_ref, i * TM, y0_buf, sem.at[0], TM)
    _start_row_gather(y_hbm, d1_ref, i * TM, y1_buf, sem.at[1], TM)
    r = route_ref[...]
    x1 = x1_ref[...]
    _wait_row_gather(y_hbm, y0_buf, sem.at[0], TM)
    _wait_row_gather(y_hbm, y1_buf, sem.at[1], TM)
    xo = x1 + (y0_buf[...] * r[:, 0:1] + y1_buf[...] * r[:, 1:2])
    o_ref[...] = _rms(xo, g_ref[...])


def _combine(d0, d1, x1, route, g, ybuf):
    t = x1.shape[0]
    tile = lambda w: pl.BlockSpec((TM, w), lambda i, a, b: (i, 0))
    return pl.pallas_call(
        _combine_kernel,
        grid_spec=pltpu.PrefetchScalarGridSpec(
            num_scalar_prefetch=2,
            grid=(t // TM,),
            in_specs=[tile(D_MODEL), tile(LANES),
                      pl.BlockSpec((1, D_MODEL), lambda i, a, b: (0, 0)),
                      pl.BlockSpec(memory_space=pl.ANY)],
            out_specs=tile(D_MODEL),
            scratch_shapes=[pltpu.VMEM((TM, D_MODEL), jnp.float32),
                            pltpu.VMEM((TM, D_MODEL), jnp.float32),
                            pltpu.SemaphoreType.DMA((2,))],
        ),
        out_shape=jax.ShapeDtypeStruct((t, D_MODEL), jnp.float32),
        compiler_params=pltpu.CompilerParams(
            dimension_semantics=("arbitrary",), vmem_limit_bytes=VMEM_LIMIT),
        name="combine",
    )(d0, d1, x1, route, g, ybuf)


def _dispatch_plan(expert_id):
    t = expert_id.shape[0]
    a = t * TOP_K
    e_flat = expert_id.reshape(a)
    onehot = (e_flat[:, None] == jnp.arange(N_EXPERTS, dtype=jnp.int32)[None, :]).astype(jnp.int32)
    rank = jnp.cumsum(onehot, axis=0) - onehot
    pos = jnp.sum(rank * onehot, axis=1)
    counts = jnp.sum(onehot, axis=0)
    padded = ((counts + ROW_BLK - 1) // ROW_BLK) * ROW_BLK
    pends = jnp.cumsum(padded)
    pstarts = pends - padded
    dest = pstarts[e_flat] + pos
    n_blocks = (a + N_EXPERTS * (ROW_BLK - 1) + ROW_BLK - 1) // ROW_BLK
    blk_start = jnp.arange(n_blocks, dtype=jnp.int32) * ROW_BLK
    blk_e = jnp.clip(jnp.sum(pends[None, :] <= blk_start[:, None], axis=1), 0, N_EXPERTS - 1)
    n_used = (pends[-1] // ROW_BLK).astype(jnp.int32).reshape(1)
    return dest.astype(jnp.int32), blk_e.astype(jnp.int32), n_used, n_blocks


def kernel(x, norm_mix_g, w_in, w_pool, pool_scale, conv_w, w_out, norm_ffn_g, w_router_group, b_router_group, w_router_expert, b_router_expert, w_gate, w_up, w_down, norm_final_g):
    bt, s, d = x.shape
    t = bt * s
    bf = jnp.bfloat16
    xt = x.reshape(t, d)
    assert norm_mix_g.shape[0] == 1, "the final norm is fused into the last layer's combine"
    for l in range(norm_mix_g.shape[0]):
        w_r = jnp.concatenate([w_router_group[l], w_router_expert[l]], axis=1)
        w_r = jnp.pad(w_r, ((0, 0), (0, LANES - w_r.shape[1]))).astype(bf)
        b_r = jnp.concatenate([b_router_group[l], b_router_expert[l]])
        b_r = jnp.pad(b_r, (0, LANES - b_r.shape[0])).reshape(1, LANES)
        x1, h2, route = _mixer(
            xt, norm_mix_g[l].reshape(1, d), w_in[l].astype(bf), w_pool[l].astype(bf),
            pool_scale[l].reshape(1, MIX_A), conv_w[l].T, w_out[l].astype(bf),
            norm_ffn_g[l].reshape(1, d), w_r, b_r)
        expert_id = route[:, 2:4].astype(jnp.int32)
        dest, blk_e, n_used, n_blocks = _dispatch_plan(expert_id)
        tok = jnp.repeat(jnp.arange(t, dtype=jnp.int32), TOP_K)
        row_tok = jnp.zeros((n_blocks * ROW_BLK,), jnp.int32).at[dest].set(tok)
        xbuf = _dispatch(row_tok, n_used, h2)
        hid = _gate_up(blk_e, n_used, xbuf, w_gate[l], w_up[l])
        ybuf = _down(blk_e, n_used, hid, w_down[l])
        dest2 = dest.reshape(t, TOP_K)
        xt = _combine(dest2[:, 0], dest2[:, 1], x1, route, norm_final_g.reshape(1, d), ybuf)
    return xt.reshape(bt, s, d)
```

```python
import jax
import jax.numpy as jnp
from jax import lax
from jax.experimental import pallas as pl
from jax.experimental.pallas import tpu as pltpu

D_MODEL = 2048
MIX_A = 1024
MIX_B = 1024
POOL_WINDOWS = (2, 4, 8, 16)
POOL_CH = MIX_A // len(POOL_WINDOWS)
CONV_W = 3
N_GROUPS = 4
E_PER_GROUP = 8
N_EXPERTS = N_GROUPS * E_PER_GROUP
TOP_K = 2
D_EXPERT = D_MODEL // 2
EPS = 1e-6

LANES = 128
ROW_SUB = D_MODEL // LANES
HIST = 16
TM = 256
ROW_BLK = 256
VMEM_LIMIT = 56 * 1024 * 1024

_NEG = -1e30


def _rms(x, g):
    return x * lax.rsqrt(jnp.mean(x * x, axis=-1, keepdims=True) + EPS) * g


def _bdot(a, b):
    return jnp.dot(a, b, preferred_element_type=jnp.float32)


def _store_rows(ref, val):
    n = val.shape[0]
    for c in range(ROW_SUB):
        ref[pl.ds(c, n, stride=ROW_SUB), :] = val[:, c * LANES:(c + 1) * LANES]


def _load_rows(ref, n):
    return jnp.concatenate([ref[pl.ds(c, n, stride=ROW_SUB), :] for c in range(ROW_SUB)], axis=-1)


def _mixer_kernel(x_ref, g1_ref, w_in_ref, w_pool_ref, pscale_ref, convw_ref, w_out_ref,
                  g2_ref, w_r_ref, b_r_ref,
                  x1_ref, h2_ref, route_ref,
                  ext_u, ext_z):
    i = pl.program_id(0)

    @pl.when(i == 0)
    def _():
        ext_u[0:HIST, :] = jnp.zeros((HIST, MIX_A), jnp.float32)
        ext_z[0:HIST, :] = jnp.zeros((HIST, MIX_B), jnp.float32)

    x = x_ref[...]
    h = _rms(x, g1_ref[...]).astype(jnp.bfloat16)

    ext_u[HIST:HIST + TM, :] = _bdot(h, w_in_ref[:, 0:MIX_A])
    row = lax.broadcasted_iota(jnp.int32, (TM, 1), 0) + i * TM + 1
    y_a = []
    for gi, w in enumerate(POOL_WINDOWS):
        c0 = gi * POOL_CH
        u = ext_u[HIST:HIST + TM, c0:c0 + POOL_CH]
        acc = u
        for s in range(1, w):
            acc = acc + ext_u[HIST - s:HIST - s + TM, c0:c0 + POOL_CH]
        cnt = jnp.minimum(row, w).astype(jnp.float32)
        pooled = (acc / cnt - u).astype(jnp.bfloat16)
        y_a.append(_bdot(pooled, w_pool_ref[gi]))
    y_a = jnp.concatenate(y_a, axis=-1) * pscale_ref[...]
    ext_u[0:HIST, :] = ext_u[TM:TM + HIST, :]

    b_gate = _bdot(h, w_in_ref[:, MIX_A:MIX_A + MIX_B])
    c_gate = _bdot(h, w_in_ref[:, MIX_A + MIX_B:MIX_A + 2 * MIX_B])
    v = _bdot(h, w_in_ref[:, MIX_A + 2 * MIX_B:MIX_A + 3 * MIX_B])
    z = c_gate * v
    ext_z[HIST:HIST + TM, :] = z
    y = ext_z[HIST - 2:HIST - 2 + TM, :] * convw_ref[0:1, :]
    y = y + ext_z[HIST - 1:HIST - 1 + TM, :] * convw_ref[1:2, :]
    y = y + z * convw_ref[2:3, :]
    y_b = b_gate * y
    ext_z[0:HIST, :] = ext_z[TM:TM + HIST, :]

    mixed = jnp.concatenate([y_a, y_b], axis=-1).astype(jnp.bfloat16)
    x1 = x + _bdot(mixed, w_out_ref[...])
    x1_ref[...] = x1

    h2_f32 = _rms(x1, g2_ref[...])
    _store_rows(h2_ref, h2_f32)
    h2 = h2_f32.astype(jnp.bfloat16)

    logits = _bdot(h2, w_r_ref[...]) + b_r_ref[...]
    lane = lax.broadcasted_iota(jnp.int32, (TM, LANES), 1)
    lane_f = lane.astype(jnp.float32)

    def first_argmax(vals, vmax):
        return jnp.min(jnp.where(vals == vmax, lane_f, float(LANES)), axis=-1, keepdims=True)

    gl = jnp.where(lane < N_GROUPS, logits, _NEG)
    gmax = jnp.max(gl, axis=-1, keepdims=True)
    g_w = 1.0 / jnp.sum(jnp.exp(gl - gmax), axis=-1, keepdims=True)
    grp = first_argmax(gl, gmax).astype(jnp.int32)
    lo = N_GROUPS + E_PER_GROUP * grp
    el = jnp.where((lane >= lo) & (lane < lo + E_PER_GROUP), logits, _NEG)
    emax = jnp.max(el, axis=-1, keepdims=True)
    idx1 = first_argmax(el, emax)
    esum = jnp.sum(jnp.exp(el - emax), axis=-1, keepdims=True)
    el2 = jnp.where(lane_f == idx1, _NEG, el)
    e2max = jnp.max(el2, axis=-1, keepdims=True)
    idx2 = first_argmax(el2, e2max)
    p1 = 1.0 / esum
    p2 = jnp.exp(e2max - emax) / esum
    tot = p1 + p2
    w1 = g_w * (p1 / tot)
    w2 = g_w * (p2 / tot)
    route = jnp.where(lane == 0, w1, 0.0)
    route = jnp.where(lane == 1, w2, route)
    route = jnp.where(lane == 2, idx1 - N_GROUPS, route)
    route = jnp.where(lane == 3, idx2 - N_GROUPS, route)
    route_ref[...] = route


def _mixer(x, g1, w_in, w_pool, pscale, convw, w_out, g2, w_r, b_r):
    t = x.shape[0]
    const = lambda shape: pl.BlockSpec(shape, lambda i: (0,) * len(shape),
                                       pipeline_mode=pl.Buffered(1))
    return pl.pallas_call(
        _mixer_kernel,
        grid=(t // TM,),
        in_specs=[
            pl.BlockSpec((TM, D_MODEL), lambda i: (i, 0)),
            const((1, D_MODEL)),
            const(w_in.shape),
            const(w_pool.shape),
            const((1, MIX_A)),
            const((CONV_W, MIX_B)),
            const(w_out.shape),
            const((1, D_MODEL)),
            const(w_r.shape),
            const((1, LANES)),
        ],
        out_specs=[
            pl.BlockSpec((TM, D_MODEL), lambda i: (i, 0)),
            pl.BlockSpec((TM * ROW_SUB, LANES), lambda i: (i, 0)),
            pl.BlockSpec((TM, LANES), lambda i: (i, 0)),
        ],
        out_shape=[
            jax.ShapeDtypeStruct((t, D_MODEL), jnp.float32),
            jax.ShapeDtypeStruct((t * ROW_SUB, LANES), jnp.float32),
            jax.ShapeDtypeStruct((t, LANES), jnp.float32),
        ],
        scratch_shapes=[
            pltpu.VMEM((TM + HIST, MIX_A), jnp.float32),
            pltpu.VMEM((TM + HIST, MIX_B), jnp.float32),
        ],
        compiler_params=pltpu.CompilerParams(
            dimension_semantics=("arbitrary",), vmem_limit_bytes=VMEM_LIMIT),
        name="mixer",
    )(x, g1, w_in, w_pool, pscale, convw, w_out, g2, w_r, b_r)


def _row_copy(src_hbm, row_start, dst_ref, r, sem):
    return pltpu.make_async_copy(src_hbm.at[pl.ds(pl.multiple_of(row_start, ROW_SUB), ROW_SUB)],
                                 dst_ref.at[pl.ds(r * ROW_SUB, ROW_SUB)], sem)


def _start_row_gather(src_hbm, idx_ref, base, dst_ref, sem, n_rows):
    for r in range(n_rows):
        _row_copy(src_hbm, idx_ref[base + r], dst_ref, r, sem).start()


def _wait_row_gather(src_hbm, dst_ref, sem, n_rows):
    pltpu.make_async_copy(src_hbm.at[pl.ds(0, n_rows * ROW_SUB)], dst_ref, sem).wait()


def _weight_copy(w_hbm, e, buf, slot, k, sem):
    return pltpu.make_async_copy(w_hbm.at[e], buf.at[slot, k], sem.at[slot])


def _gate_up_kernel(blk_e_ref, first_ref, wslot_ref, nxt_e_ref, n_used_ref, row_ref,
                    h2_hbm, wg_hbm, wu_hbm, hid_ref,
                    xg, wbuf, wg_bf, wu_bf, gsem, wsem):
    b = pl.program_id(0)
    n_used = n_used_ref[0]
    e = blk_e_ref[b]
    slot = wslot_ref[b]

    @pl.when(b == 0)
    def _():
        _weight_copy(wg_hbm, e, wbuf, slot, 0, wsem).start()
        _weight_copy(wu_hbm, e, wbuf, slot, 1, wsem).start()
        _start_row_gather(h2_hbm, row_ref, 0, xg.at[0], gsem.at[0], ROW_BLK)

    @pl.when(b < n_used)
    def _():
        @pl.when(first_ref[b] == 1)
        def _():
            _weight_copy(wg_hbm, e, wbuf, slot, 0, wsem).wait()
            _weight_copy(wu_hbm, e, wbuf, slot, 1, wsem).wait()
            nxt = nxt_e_ref[b]

            @pl.when(nxt != e)
            def _():
                _weight_copy(wg_hbm, nxt, wbuf, 1 - slot, 0, wsem).start()
                _weight_copy(wu_hbm, nxt, wbuf, 1 - slot, 1, wsem).start()

            wg_bf[...] = wbuf[slot, 0].astype(jnp.bfloat16)
            wu_bf[...] = wbuf[slot, 1].astype(jnp.bfloat16)

        cur = b % 2

        @pl.when(b + 1 < n_used)
        def _():
            _start_row_gather(h2_hbm, row_ref, (b + 1) * ROW_BLK, xg.at[1 - cur], gsem.at[1 - cur],
                              ROW_BLK)

        _wait_row_gather(h2_hbm, xg.at[cur], gsem.at[cur], ROW_BLK)
        xb = _load_rows(xg.at[cur], ROW_BLK).astype(jnp.bfloat16)
        gate = _bdot(xb, wg_bf[...])
        up = _bdot(xb, wu_bf[...])
        hid_ref[...] = (gate * jax.nn.sigmoid(gate) * up).astype(jnp.bfloat16)

    @pl.when(b >= n_used)
    def _():
        hid_ref[...] = jnp.zeros_like(hid_ref)


def _gate_up(plan, h2, wg, wu):
    n_blocks = plan["blk_e"].shape[0]
    any_spec = pl.BlockSpec(memory_space=pl.ANY)
    return pl.pallas_call(
        _gate_up_kernel,
        grid_spec=pltpu.PrefetchScalarGridSpec(
            num_scalar_prefetch=6,
            grid=(n_blocks,),
            in_specs=[any_spec, any_spec, any_spec],
            out_specs=pl.BlockSpec((ROW_BLK, D_EXPERT), lambda b, *_: (b, 0)),
            scratch_shapes=[
                pltpu.VMEM((2, ROW_BLK * ROW_SUB, LANES), jnp.float32),
                pltpu.VMEM((2, 2, D_MODEL, D_EXPERT), jnp.float32),
                pltpu.VMEM((D_MODEL, D_EXPERT), jnp.bfloat16),
                pltpu.VMEM((D_MODEL, D_EXPERT), jnp.bfloat16),
                pltpu.SemaphoreType.DMA((2,)),
                pltpu.SemaphoreType.DMA((2,)),
            ],
        ),
        out_shape=jax.ShapeDtypeStruct((n_blocks * ROW_BLK, D_EXPERT), jnp.bfloat16),
        compiler_params=pltpu.CompilerParams(
            dimension_semantics=("arbitrary",), vmem_limit_bytes=VMEM_LIMIT),
        name="gate_up",
    )(plan["blk_e"], plan["first"], plan["wslot"], plan["nxt_e"], plan["n_used"], plan["row_src"],
      h2, wg, wu)


def _down_kernel(blk_e_ref, first_ref, wslot_ref, nxt_e_ref, n_used_ref,
                 hid_ref, wd_hbm, y_ref, wbuf, wd_bf, wsem):
    b = pl.program_id(0)
    n_used = n_used_ref[0]
    e = blk_e_ref[b]
    slot = wslot_ref[b]

    @pl.when(b == 0)
    def _():
        _weight_copy(wd_hbm, e, wbuf, slot, 0, wsem).start()

    @pl.when(b < n_used)
    def _():
        @pl.when(first_ref[b] == 1)
        def _():
            _weight_copy(wd_hbm, e, wbuf, slot, 0, wsem).wait()
            nxt = nxt_e_ref[b]

            @pl.when(nxt != e)
            def _():
                _weight_copy(wd_hbm, nxt, wbuf, 1 - slot, 0, wsem).start()

            wd_bf[...] = wbuf[slot, 0].astype(jnp.bfloat16)

        _store_rows(y_ref, _bdot(hid_ref[...], wd_bf[...]))

    @pl.when(b >= n_used)
    def _():
        y_ref[...] = jnp.zeros_like(y_ref)


def _down(plan, hid, wd):
    n_blocks = plan["blk_e"].shape[0]
    return pl.pallas_call(
        _down_kernel,
        grid_spec=pltpu.PrefetchScalarGridSpec(
            num_scalar_prefetch=5,
            grid=(n_blocks,),
            in_specs=[pl.BlockSpec((ROW_BLK, D_EXPERT), lambda b, *_: (b, 0)),
                      pl.BlockSpec(memory_space=pl.ANY)],
            out_specs=pl.BlockSpec((ROW_BLK * ROW_SUB, LANES), lambda b, *_: (b, 0)),
            scratch_shapes=[
                pltpu.VMEM((2, 1, D_EXPERT, D_MODEL), jnp.float32),
                pltpu.VMEM((D_EXPERT, D_MODEL), jnp.bfloat16),
                pltpu.SemaphoreType.DMA((2,)),
            ],
        ),
        out_shape=jax.ShapeDtypeStruct((n_blocks * ROW_BLK * ROW_SUB, LANES), jnp.float32),
        compiler_params=pltpu.CompilerParams(
            dimension_semantics=("arbitrary",), vmem_limit_bytes=VMEM_LIMIT),
        name="down",
    )(plan["blk_e"], plan["first"], plan["wslot"], plan["nxt_e"], plan["n_used"], hid, wd)


def _combine_kernel(d0_ref, d1_ref, x1_ref, route_ref, g_ref, y_hbm, o_ref, y0_buf, y1_buf, sem):
    i = pl.program_id(0)
    n = pl.num_programs(0)
    cur = i % 2

    def start(step, slot):
        _start_row_gather(y_hbm, d0_ref, step * TM, y0_buf.at[slot], sem.at[0, slot], TM)
        _start_row_gather(y_hbm, d1_ref, step * TM, y1_buf.at[slot], sem.at[1, slot], TM)

    @pl.when(i == 0)
    def _():
        start(0, 0)

    @pl.when(i + 1 < n)
    def _():
        start(i + 1, 1 - cur)

    r = route_ref[...]
    _wait_row_gather(y_hbm, y0_buf.at[cur], sem.at[0, cur], TM)
    _wait_row_gather(y_hbm, y1_buf.at[cur], sem.at[1, cur], TM)
    y0 = _load_rows(y0_buf.at[cur], TM)
    y1 = _load_rows(y1_buf.at[cur], TM)
    xo = x1_ref[...] + (y0 * r[:, 0:1] + y1 * r[:, 1:2])
    o_ref[...] = _rms(xo, g_ref[...])


def _combine(d0, d1, x1, route, g, ybuf):
    t = x1.shape[0]
    tile = lambda w: pl.BlockSpec((TM, w), lambda i, *_: (i, 0))
    return pl.pallas_call(
        _combine_kernel,
        grid_spec=pltpu.PrefetchScalarGridSpec(
            num_scalar_prefetch=2,
            grid=(t // TM,),
            in_specs=[tile(D_MODEL), tile(LANES),
                      pl.BlockSpec((1, D_MODEL), lambda i, *_: (0, 0)),
                      pl.BlockSpec(memory_space=pl.ANY)],
            out_specs=tile(D_MODEL),
            scratch_shapes=[pltpu.VMEM((2, TM * ROW_SUB, LANES), jnp.float32),
                            pltpu.VMEM((2, TM * ROW_SUB, LANES), jnp.float32),
                            pltpu.SemaphoreType.DMA((2, 2))],
        ),
        out_shape=jax.ShapeDtypeStruct((t, D_MODEL), jnp.float32),
        compiler_params=pltpu.CompilerParams(
            dimension_semantics=("arbitrary",), vmem_limit_bytes=VMEM_LIMIT),
        name="combine",
    )(d0, d1, x1, route, g, ybuf)


def _dispatch_plan(expert_id):
    t = expert_id.shape[0]
    a = t * TOP_K
    i32 = jnp.int32
    e_flat = expert_id.reshape(a)
    experts = jnp.arange(N_EXPERTS, dtype=i32)
    onehot = (e_flat[:, None] == experts[None, :]).astype(i32)
    rank = jnp.cumsum(onehot, axis=0) - onehot
    pos = jnp.sum(rank * onehot, axis=1)
    counts = jnp.sum(onehot, axis=0)
    padded = ((counts + ROW_BLK - 1) // ROW_BLK) * ROW_BLK
    pends = jnp.cumsum(padded)
    pstarts = pends - padded
    dest = (pstarts[e_flat] + pos).astype(i32)
    n_blocks = (a + N_EXPERTS * (ROW_BLK - 1) + ROW_BLK - 1) // ROW_BLK
    blk_start = jnp.arange(n_blocks, dtype=i32) * ROW_BLK
    blk_e = jnp.clip(jnp.sum(pends[None, :] <= blk_start[:, None], axis=1), 0, N_EXPERTS - 1).astype(i32)
    first = jnp.concatenate([jnp.ones((1,), i32), (blk_e[1:] != blk_e[:-1]).astype(i32)])
    wslot = (jnp.cumsum(first) - 1) % 2
    later_used = (experts[None, :] > experts[:, None]) & (counts[None, :] > 0)
    nxt = jnp.min(jnp.where(later_used, experts[None, :], N_EXPERTS), axis=1)
    nxt = jnp.where(nxt == N_EXPERTS, experts, nxt)
    tok = jnp.repeat(jnp.arange(t, dtype=i32), TOP_K)
    row_tok = jnp.zeros((n_blocks * ROW_BLK,), i32).at[dest].set(tok)
    return {
        "dest": dest,
        "blk_e": blk_e,
        "first": first,
        "wslot": wslot.astype(i32),
        "nxt_e": nxt[blk_e].astype(i32),
        "n_used": (pends[-1] // ROW_BLK).astype(i32).reshape(1),
        "row_src": row_tok * ROW_SUB,
    }


def kernel(x, norm_mix_g, w_in, w_pool, pool_scale, conv_w, w_out, norm_ffn_g, w_router_group, b_router_group, w_router_expert, b_router_expert, w_gate, w_up, w_down, norm_final_g):
    bt, s, d = x.shape
    t = bt * s
    bf = jnp.bfloat16
    xt = x.reshape(t, d)
    assert norm_mix_g.shape[0] == 1, "the final norm is fused into the last layer's combine"
    for l in range(norm_mix_g.shape[0]):
        w_r = jnp.concatenate([w_router_group[l], w_router_expert[l]], axis=1)
        w_r = jnp.pad(w_r, ((0, 0), (0, LANES - w_r.shape[1]))).astype(bf)
        b_r = jnp.concatenate([b_router_group[l], b_router_expert[l]])
        b_r = jnp.pad(b_r, (0, LANES - b_r.shape[0])).reshape(1, LANES)
        x1, h2, route = _mixer(
            xt, norm_mix_g[l].reshape(1, d), w_in[l].astype(bf), w_pool[l].astype(bf),
            pool_scale[l].reshape(1, MIX_A), conv_w[l].T, w_out[l].astype(bf),
            norm_ffn_g[l].reshape(1, d), w_r, b_r)
        plan = _dispatch_plan(route[:, 2:4].astype(jnp.int32))
        hid = _gate_up(plan, h2, w_gate[l], w_up[l])
        ybuf = _down(plan, hid, w_down[l])
        dest2 = plan["dest"].reshape(t, TOP_K) * ROW_SUB
        xt = _combine(dest2[:, 0], dest2[:, 1], x1, route, norm_final_g.reshape(1, d), ybuf)
    return xt.reshape(bt, s, d)
```

```python
import jax
import jax.numpy as jnp
from jax import lax
from jax.experimental import pallas as pl
from jax.experimental.pallas import tpu as pltpu

D_MODEL = 2048
MIX_A = 1024
MIX_B = 1024
POOL_WINDOWS = (2, 4, 8, 16)
POOL_CH = MIX_A // len(POOL_WINDOWS)
CONV_W = 3
N_GROUPS = 4
E_PER_GROUP = 8
N_EXPERTS = N_GROUPS * E_PER_GROUP
TOP_K = 2
D_EXPERT = D_MODEL // 2
EPS = 1e-6

LANES = 128
ROW_SUB = D_MODEL // LANES
HIST = 16
TM = 256
ROW_BLK = 256
VMEM_LIMIT = 56 * 1024 * 1024
WEIGHT_DMA_PRIORITY = 1

_NEG = -1e30


def _rms(x, g):
    return x * lax.rsqrt(jnp.mean(x * x, axis=-1, keepdims=True) + EPS) * g


def _bdot(a, b):
    return jnp.dot(a, b, preferred_element_type=jnp.float32)


def _store_rows(ref, val):
    n = val.shape[0]
    for c in range(ROW_SUB):
        ref[pl.ds(c, n, stride=ROW_SUB), :] = val[:, c * LANES:(c + 1) * LANES]


def _load_rows(ref, n):
    return jnp.concatenate([ref[pl.ds(c, n, stride=ROW_SUB), :] for c in range(ROW_SUB)], axis=-1)


def _mixer_kernel(x_ref, g1_ref, w_in_ref, w_pool_ref, pscale_ref, convw_ref, w_out_ref,
                  g2_ref, w_r_ref, b_r_ref,
                  x1_ref, h2_ref, route_ref,
                  ext_u, ext_z):
    i = pl.program_id(0)

    @pl.when(i == 0)
    def _():
        ext_u[0:HIST, :] = jnp.zeros((HIST, MIX_A), jnp.float32)
        ext_z[0:HIST, :] = jnp.zeros((HIST, MIX_B), jnp.float32)

    x = x_ref[...]
    h = _rms(x, g1_ref[...]).astype(jnp.bfloat16)

    ext_u[HIST:HIST + TM, :] = _bdot(h, w_in_ref[:, 0:MIX_A])
    row = lax.broadcasted_iota(jnp.int32, (TM, 1), 0) + i * TM + 1
    y_a = []
    for gi, w in enumerate(POOL_WINDOWS):
        c0 = gi * POOL_CH
        u = ext_u[HIST:HIST + TM, c0:c0 + POOL_CH]
        acc = u
        for s in range(1, w):
            acc = acc + ext_u[HIST - s:HIST - s + TM, c0:c0 + POOL_CH]
        cnt = jnp.minimum(row, w).astype(jnp.float32)
        pooled = (acc / cnt - u).astype(jnp.bfloat16)
        y_a.append(_bdot(pooled, w_pool_ref[gi]))
    y_a = jnp.concatenate(y_a, axis=-1) * pscale_ref[...]
    ext_u[0:HIST, :] = ext_u[TM:TM + HIST, :]

    b_gate = _bdot(h, w_in_ref[:, MIX_A:MIX_A + MIX_B])
    c_gate = _bdot(h, w_in_ref[:, MIX_A + MIX_B:MIX_A + 2 * MIX_B])
    v = _bdot(h, w_in_ref[:, MIX_A + 2 * MIX_B:MIX_A + 3 * MIX_B])
    z = c_gate * v
    ext_z[HIST:HIST + TM, :] = z
    y = ext_z[HIST - 2:HIST - 2 + TM, :] * convw_ref[0:1, :]
    y = y + ext_z[HIST - 1:HIST - 1 + TM, :] * convw_ref[1:2, :]
    y = y + z * convw_ref[2:3, :]
    y_b = b_gate * y
    ext_z[0:HIST, :] = ext_z[TM:TM + HIST, :]

    mixed = jnp.concatenate([y_a, y_b], axis=-1).astype(jnp.bfloat16)
    x1 = x + _bdot(mixed, w_out_ref[...])
    x1_ref[...] = x1

    h2_f32 = _rms(x1, g2_ref[...])
    _store_rows(h2_ref, h2_f32)
    h2 = h2_f32.astype(jnp.bfloat16)

    logits = _bdot(h2, w_r_ref[...]) + b_r_ref[...]
    lane = lax.broadcasted_iota(jnp.int32, (TM, LANES), 1)
    lane_f = lane.astype(jnp.float32)

    def first_argmax(vals, vmax):
        return jnp.min(jnp.where(vals == vmax, lane_f, float(LANES)), axis=-1, keepdims=True)

    gl = jnp.where(lane < N_GROUPS, logits, _NEG)
    gmax = jnp.max(gl, axis=-1, keepdims=True)
    g_w = 1.0 / jnp.sum(jnp.exp(gl - gmax), axis=-1, keepdims=True)
    grp = first_argmax(gl, gmax).astype(jnp.int32)
    lo = N_GROUPS + E_PER_GROUP * grp
    el = jnp.where((lane >= lo) & (lane < lo + E_PER_GROUP), logits, _NEG)
    emax = jnp.max(el, axis=-1, keepdims=True)
    idx1 = first_argmax(el, emax)
    esum = jnp.sum(jnp.exp(el - emax), axis=-1, keepdims=True)
    el2 = jnp.where(lane_f == idx1, _NEG, el)
    e2max = jnp.max(el2, axis=-1, keepdims=True)
    idx2 = first_argmax(el2, e2max)
    p1 = 1.0 / esum
    p2 = jnp.exp(e2max - emax) / esum
    tot = p1 + p2
    w1 = g_w * (p1 / tot)
    w2 = g_w * (p2 / tot)
    route = jnp.where(lane == 0, w1, 0.0)
    route = jnp.where(lane == 1, w2, route)
    route = jnp.where(lane == 2, idx1 - N_GROUPS, route)
    route = jnp.where(lane == 3, idx2 - N_GROUPS, route)
    route_ref[...] = route


def _mixer(x, g1, w_in, w_pool, pscale, convw, w_out, g2, w_r, b_r):
    t = x.shape[0]
    const = lambda shape: pl.BlockSpec(shape, lambda i: (0,) * len(shape),
                                       pipeline_mode=pl.Buffered(1))
    return pl.pallas_call(
        _mixer_kernel,
        grid=(t // TM,),
        in_specs=[
            pl.BlockSpec((TM, D_MODEL), lambda i: (i, 0)),
            const((1, D_MODEL)),
            const(w_in.shape),
            const(w_pool.shape),
            const((1, MIX_A)),
            const((CONV_W, MIX_B)),
            const(w_out.shape),
            const((1, D_MODEL)),
            const(w_r.shape),
            const((1, LANES)),
        ],
        out_specs=[
            pl.BlockSpec((TM, D_MODEL), lambda i: (i, 0)),
            pl.BlockSpec((TM * ROW_SUB, LANES), lambda i: (i, 0)),
            pl.BlockSpec((TM, LANES), lambda i: (i, 0)),
        ],
        out_shape=[
            jax.ShapeDtypeStruct((t, D_MODEL), jnp.float32),
            jax.ShapeDtypeStruct((t * ROW_SUB, LANES), jnp.float32),
            jax.ShapeDtypeStruct((t, LANES), jnp.float32),
        ],
        scratch_shapes=[
            pltpu.VMEM((TM + HIST, MIX_A), jnp.float32),
            pltpu.VMEM((TM + HIST, MIX_B), jnp.float32),
        ],
        compiler_params=pltpu.CompilerParams(
            dimension_semantics=("arbitrary",), vmem_limit_bytes=VMEM_LIMIT),
        name="mixer",
    )(x, g1, w_in, w_pool, pscale, convw, w_out, g2, w_r, b_r)


def _row_copy(src_hbm, row_start, dst_ref, r, sem):
    return pltpu.make_async_copy(src_hbm.at[pl.ds(pl.multiple_of(row_start, ROW_SUB), ROW_SUB)],
                                 dst_ref.at[pl.ds(r * ROW_SUB, ROW_SUB)], sem)


def _start_row_gather(src_hbm, idx_ref, base, dst_ref, sem, n_rows):
    for r in range(n_rows):
        _row_copy(src_hbm, idx_ref[base + r], dst_ref, r, sem).start()


def _wait_row_gather(src_hbm, dst_ref, sem, n_rows):
    pltpu.make_async_copy(src_hbm.at[pl.ds(0, n_rows * ROW_SUB)], dst_ref, sem).wait()


def _weight_copy(w_hbm, e, buf, slot, k, sem):
    return pltpu.make_async_copy(w_hbm.at[e], buf.at[slot, k], sem.at[slot])


def _gate_up_kernel(blk_e_ref, first_ref, wslot_ref, nxt_e_ref, n_used_ref, row_ref,
                    h2_hbm, wg_hbm, wu_hbm, hid_ref,
                    xg, wbuf, wg_bf, wu_bf, gsem, wsem):
    b = pl.program_id(0)
    n_used = n_used_ref[0]
    e = blk_e_ref[b]
    slot = wslot_ref[b]

    @pl.when(b == 0)
    def _():
        _weight_copy(wg_hbm, e, wbuf, slot, 0, wsem).start(priority=WEIGHT_DMA_PRIORITY)
        _weight_copy(wu_hbm, e, wbuf, slot, 1, wsem).start(priority=WEIGHT_DMA_PRIORITY)
        _start_row_gather(h2_hbm, row_ref, 0, xg.at[0], gsem.at[0], ROW_BLK)

    @pl.when(b < n_used)
    def _():
        @pl.when(first_ref[b] == 1)
        def _():
            _weight_copy(wg_hbm, e, wbuf, slot, 0, wsem).wait()
            _weight_copy(wu_hbm, e, wbuf, slot, 1, wsem).wait()
            nxt = nxt_e_ref[b]

            @pl.when(nxt != e)
            def _():
                _weight_copy(wg_hbm, nxt, wbuf, 1 - slot, 0, wsem).start(priority=WEIGHT_DMA_PRIORITY)
                _weight_copy(wu_hbm, nxt, wbuf, 1 - slot, 1, wsem).start(priority=WEIGHT_DMA_PRIORITY)

            wg_bf[...] = wbuf[slot, 0].astype(jnp.bfloat16)
            wu_bf[...] = wbuf[slot, 1].astype(jnp.bfloat16)

        cur = b % 2

        @pl.when(b + 1 < n_used)
        def _():
            _start_row_gather(h2_hbm, row_ref, (b + 1) * ROW_BLK, xg.at[1 - cur], gsem.at[1 - cur],
                              ROW_BLK)

        _wait_row_gather(h2_hbm, xg.at[cur], gsem.at[cur], ROW_BLK)
        xb = _load_rows(xg.at[cur], ROW_BLK).astype(jnp.bfloat16)
        gate = _bdot(xb, wg_bf[...])
        up = _bdot(xb, wu_bf[...])
        hid_ref[...] = (gate * jax.nn.sigmoid(gate) * up).astype(jnp.bfloat16)

    @pl.when(b >= n_used)
    def _():
        hid_ref[...] = jnp.zeros_like(hid_ref)


def _gate_up(plan, h2, wg, wu):
    n_blocks = plan["blk_e"].shape[0]
    any_spec = pl.BlockSpec(memory_space=pl.ANY)
    return pl.pallas_call(
        _gate_up_kernel,
        grid_spec=pltpu.PrefetchScalarGridSpec(
            num_scalar_prefetch=6,
            grid=(n_blocks,),
            in_specs=[any_spec, any_spec, any_spec],
            out_specs=pl.BlockSpec((ROW_BLK, D_EXPERT), lambda b, *_: (b, 0)),
            scratch_shapes=[
                pltpu.VMEM((2, ROW_BLK * ROW_SUB, LANES), jnp.float32),
                pltpu.VMEM((2, 2, D_MODEL, D_EXPERT), jnp.float32),
                pltpu.VMEM((D_MODEL, D_EXPERT), jnp.bfloat16),
                pltpu.VMEM((D_MODEL, D_EXPERT), jnp.bfloat16),
                pltpu.SemaphoreType.DMA((2,)),
                pltpu.SemaphoreType.DMA((2,)),
            ],
        ),
        out_shape=jax.ShapeDtypeStruct((n_blocks * ROW_BLK, D_EXPERT), jnp.bfloat16),
        compiler_params=pltpu.CompilerParams(
            dimension_semantics=("arbitrary",), vmem_limit_bytes=VMEM_LIMIT),
        name="gate_up",
    )(plan["blk_e"], plan["first"], plan["wslot"], plan["nxt_e"], plan["n_used"], plan["row_src"],
      h2, wg, wu)


def _down_kernel(blk_e_ref, first_ref, wslot_ref, nxt_e_ref, n_used_ref,
                 hid_ref, wd_hbm, y_ref, wbuf, wd_bf, wsem):
    b = pl.program_id(0)
    n_used = n_used_ref[0]
    e = blk_e_ref[b]
    slot = wslot_ref[b]

    @pl.when(b == 0)
    def _():
        _weight_copy(wd_hbm, e, wbuf, slot, 0, wsem).start(priority=WEIGHT_DMA_PRIORITY)

    @pl.when(b < n_used)
    def _():
        @pl.when(first_ref[b] == 1)
        def _():
            _weight_copy(wd_hbm, e, wbuf, slot, 0, wsem).wait()
            nxt = nxt_e_ref[b]

            @pl.when(nxt != e)
            def _():
                _weight_copy(wd_hbm, nxt, wbuf, 1 - slot, 0, wsem).start(priority=WEIGHT_DMA_PRIORITY)

            wd_bf[...] = wbuf[slot, 0].astype(jnp.bfloat16)

        _store_rows(y_ref, _bdot(hid_ref[...], wd_bf[...]))

    @pl.when(b >= n_used)
    def _():
        y_ref[...] = jnp.zeros_like(y_ref)


def _down(plan, hid, wd):
    n_blocks = plan["blk_e"].shape[0]
    return pl.pallas_call(
        _down_kernel,
        grid_spec=pltpu.PrefetchScalarGridSpec(
            num_scalar_prefetch=5,
            grid=(n_blocks,),
            in_specs=[pl.BlockSpec((ROW_BLK, D_EXPERT), lambda b, *_: (b, 0)),
                      pl.BlockSpec(memory_space=pl.ANY)],
            out_specs=pl.BlockSpec((ROW_BLK * ROW_SUB, LANES), lambda b, *_: (b, 0)),
            scratch_shapes=[
                pltpu.VMEM((2, 1, D_EXPERT, D_MODEL), jnp.float32),
                pltpu.VMEM((D_EXPERT, D_MODEL), jnp.bfloat16),
                pltpu.SemaphoreType.DMA((2,)),
            ],
        ),
        out_shape=jax.ShapeDtypeStruct((n_blocks * ROW_BLK * ROW_SUB, LANES), jnp.float32),
        compiler_params=pltpu.CompilerParams(
            dimension_semantics=("arbitrary",), vmem_limit_bytes=VMEM_LIMIT),
        name="down",
    )(plan["blk_e"], plan["first"], plan["wslot"], plan["nxt_e"], plan["n_used"], hid, wd)


def _combine_kernel(d0_ref, d1_ref, x1_ref, route_ref, g_ref, y_hbm, o_ref, y0_buf, y1_buf, sem):
    i = pl.program_id(0)
    n = pl.num_programs(0)
    cur = i % 2

    def start(step, slot):
        _start_row_gather(y_hbm, d0_ref, step * TM, y0_buf.at[slot], sem.at[0, slot], TM)
        _start_row_gather(y_hbm, d1_ref, step * TM, y1_buf.at[slot], sem.at[1, slot], TM)

    @pl.when(i == 0)
    def _():
        start(0, 0)

    @pl.when(i + 1 < n)
    def _():
        start(i + 1, 1 - cur)

    r = route_ref[...]
    _wait_row_gather(y_hbm, y0_buf.at[cur], sem.at[0, cur], TM)
    _wait_row_gather(y_hbm, y1_buf.at[cur], sem.at[1, cur], TM)
    y0 = _load_rows(y0_buf.at[cur], TM)
    y1 = _load_rows(y1_buf.at[cur], TM)
    xo = x1_ref[...] + (y0 * r[:, 0:1] + y1 * r[:, 1:2])
    o_ref[...] = _rms(xo, g_ref[...])


def _combine(d0, d1, x1, route, g, ybuf):
    t = x1.shape[0]
    tile = lambda w: pl.BlockSpec((TM, w), lambda i, *_: (i, 0))
    return pl.pallas_call(
        _combine_kernel,
        grid_spec=pltpu.PrefetchScalarGridSpec(
            num_scalar_prefetch=2,
            grid=(t // TM,),
            in_specs=[tile(D_MODEL), tile(LANES),
                      pl.BlockSpec((1, D_MODEL), lambda i, *_: (0, 0)),
                      pl.BlockSpec(memory_space=pl.ANY)],
            out_specs=tile(D_MODEL),
            scratch_shapes=[pltpu.VMEM((2, TM * ROW_SUB, LANES), jnp.float32),
                            pltpu.VMEM((2, TM * ROW_SUB, LANES), jnp.float32),
                            pltpu.SemaphoreType.DMA((2, 2))],
        ),
        out_shape=jax.ShapeDtypeStruct((t, D_MODEL), jnp.float32),
        compiler_params=pltpu.CompilerParams(
            dimension_semantics=("arbitrary",), vmem_limit_bytes=VMEM_LIMIT),
        name="combine",
    )(d0, d1, x1, route, g, ybuf)


def _dispatch_plan(expert_id):
    t = expert_id.shape[0]
    a = t * TOP_K
    i32 = jnp.int32
    e_flat = expert_id.reshape(a)
    experts = jnp.arange(N_EXPERTS, dtype=i32)
    onehot = (e_flat[:, None] == experts[None, :]).astype(i32)
    rank = jnp.cumsum(onehot, axis=0) - onehot
    pos = jnp.sum(rank * onehot, axis=1)
    counts = jnp.sum(onehot, axis=0)
    padded = ((counts + ROW_BLK - 1) // ROW_BLK) * ROW_BLK
    pends = jnp.cumsum(padded)
    pstarts = pends - padded
    dest = (pstarts[e_flat] + pos).astype(i32)
    n_blocks = (a + N_EXPERTS * (ROW_BLK - 1) + ROW_BLK - 1) // ROW_BLK
    blk_start = jnp.arange(n_blocks, dtype=i32) * ROW_BLK
    blk_e = jnp.clip(jnp.sum(pends[None, :] <= blk_start[:, None], axis=1), 0, N_EXPERTS - 1).astype(i32)
    first = jnp.concatenate([jnp.ones((1,), i32), (blk_e[1:] != blk_e[:-1]).astype(i32)])
    wslot = (jnp.cumsum(first) - 1) % 2
    later_used = (experts[None, :] > experts[:, None]) & (counts[None, :] > 0)
    nxt = jnp.min(jnp.where(later_used, experts[None, :], N_EXPERTS), axis=1)
    nxt = jnp.where(nxt == N_EXPERTS, experts, nxt)
    tok = jnp.repeat(jnp.arange(t, dtype=i32), TOP_K)
    row_tok = jnp.zeros((n_blocks * ROW_BLK,), i32).at[dest].set(tok)
    return {
        "dest": dest,
        "blk_e": blk_e,
        "first": first,
        "wslot": wslot.astype(i32),
        "nxt_e": nxt[blk_e].astype(i32),
        "n_used": (pends[-1] // ROW_BLK).astype(i32).reshape(1),
        "row_src": row_tok * ROW_SUB,
    }


def kernel(x, norm_mix_g, w_in, w_pool, pool_scale, conv_w, w_out, norm_ffn_g, w_router_group, b_router_group, w_router_expert, b_router_expert, w_gate, w_up, w_down, norm_final_g):
    bt, s, d = x.shape
    t = bt * s
    bf = jnp.bfloat16
    xt = x.reshape(t, d)
    assert norm_mix_g.shape[0] == 1, "the final norm is fused into the last layer's combine"
    for l in range(norm_mix_g.shape[0]):
        w_r = jnp.concatenate([w_router_group[l], w_router_expert[l]], axis=1)
        w_r = jnp.pad(w_r, ((0, 0), (0, LANES - w_r.shape[1]))).astype(bf)
        b_r = jnp.concatenate([b_router_group[l], b_router_expert[l]])
        b_r = jnp.pad(b_r, (0, LANES - b_r.shape[0])).reshape(1, LANES)
        x1, h2, route = _mixer(
            xt, norm_mix_g[l].reshape(1, d), w_in[l].astype(bf), w_pool[l].astype(bf),
            pool_scale[l].reshape(1, MIX_A), conv_w[l].T, w_out[l].astype(bf),
            norm_ffn_g[l].reshape(1, d), w_r, b_r)
        plan = _dispatch_plan(route[:, 2:4].astype(jnp.int32))
        hid = _gate_up(plan, h2, w_gate[l], w_up[l])
        ybuf = _down(plan, hid, w_down[l])
        dest2 = plan["dest"].reshape(t, TOP_K) * ROW_SUB
        xt = _combine(dest2[:, 0], dest2[:, 1], x1, route, norm_final_g.reshape(1, d), ybuf)
    return xt.reshape(bt, s, d)
```

```python
import jax
import jax.numpy as jnp
from jax import lax
from jax.experimental import pallas as pl
from jax.experimental.pallas import tpu as pltpu

D_MODEL = 2048
MIX_A = 1024
MIX_B = 1024
POOL_WINDOWS = (2, 4, 8, 16)
POOL_CH = MIX_A // len(POOL_WINDOWS)
CONV_W = 3
N_GROUPS = 4
E_PER_GROUP = 8
N_EXPERTS = N_GROUPS * E_PER_GROUP
TOP_K = 2
D_EXPERT = D_MODEL // 2
EPS = 1e-6

LANES = 128
ROW_SUB = D_MODEL // LANES
HIST = 16
TM = 256
ROW_BLK = 256
VMEM_LIMIT = 56 * 1024 * 1024
WEIGHT_DMA_PRIORITY = 1
WEIGHT_DMA_CHUNKS = 8

_NEG = -1e30


def _rms(x, g):
    return x * lax.rsqrt(jnp.mean(x * x, axis=-1, keepdims=True) + EPS) * g


def _bdot(a, b):
    return jnp.dot(a, b, preferred_element_type=jnp.float32)


def _store_rows(ref, val):
    n = val.shape[0]
    for c in range(ROW_SUB):
        ref[pl.ds(c, n, stride=ROW_SUB), :] = val[:, c * LANES:(c + 1) * LANES]


def _load_rows(ref, n):
    return jnp.concatenate([ref[pl.ds(c, n, stride=ROW_SUB), :] for c in range(ROW_SUB)], axis=-1)


def _mixer_kernel(x_ref, g1_ref, w_in_ref, w_pool_ref, pscale_ref, convw_ref, w_out_ref,
                  g2_ref, w_r_ref, b_r_ref,
                  x1_ref, h2_ref, route_ref,
                  ext_u, ext_z):
    i = pl.program_id(0)

    @pl.when(i == 0)
    def _():
        ext_u[0:HIST, :] = jnp.zeros((HIST, MIX_A), jnp.float32)
        ext_z[0:HIST, :] = jnp.zeros((HIST, MIX_B), jnp.float32)

    x = x_ref[...]
    h = _rms(x, g1_ref[...]).astype(jnp.bfloat16)

    ext_u[HIST:HIST + TM, :] = _bdot(h, w_in_ref[:, 0:MIX_A])
    row = lax.broadcasted_iota(jnp.int32, (TM, 1), 0) + i * TM + 1
    y_a = []
    for gi, w in enumerate(POOL_WINDOWS):
        c0 = gi * POOL_CH
        u = ext_u[HIST:HIST + TM, c0:c0 + POOL_CH]
        acc = u
        for s in range(1, w):
            acc = acc + ext_u[HIST - s:HIST - s + TM, c0:c0 + POOL_CH]
        cnt = jnp.minimum(row, w).astype(jnp.float32)
        pooled = (acc / cnt - u).astype(jnp.bfloat16)
        y_a.append(_bdot(pooled, w_pool_ref[gi]))
    y_a = jnp.concatenate(y_a, axis=-1) * pscale_ref[...]
    ext_u[0:HIST, :] = ext_u[TM:TM + HIST, :]

    b_gate = _bdot(h, w_in_ref[:, MIX_A:MIX_A + MIX_B])
    c_gate = _bdot(h, w_in_ref[:, MIX_A + MIX_B:MIX_A + 2 * MIX_B])
    v = _bdot(h, w_in_ref[:, MIX_A + 2 * MIX_B:MIX_A + 3 * MIX_B])
    z = c_gate * v
    ext_z[HIST:HIST + TM, :] = z
    y = ext_z[HIST - 2:HIST - 2 + TM, :] * convw_ref[0:1, :]
    y = y + ext_z[HIST - 1:HIST - 1 + TM, :] * convw_ref[1:2, :]
    y = y + z * convw_ref[2:3, :]
    y_b = b_gate * y
    ext_z[0:HIST, :] = ext_z[TM:TM + HIST, :]

    mixed = jnp.concatenate([y_a, y_b], axis=-1).astype(jnp.bfloat16)
    x1 = x + _bdot(mixed, w_out_ref[...])
    x1_ref[...] = x1

    h2_f32 = _rms(x1, g2_ref[...])
    _store_rows(h2_ref, h2_f32)
    h2 = h2_f32.astype(jnp.bfloat16)

    logits = _bdot(h2, w_r_ref[...]) + b_r_ref[...]
    lane = lax.broadcasted_iota(jnp.int32, (TM, LANES), 1)
    lane_f = lane.astype(jnp.float32)

    def first_argmax(vals, vmax):
        return jnp.min(jnp.where(vals == vmax, lane_f, float(LANES)), axis=-1, keepdims=True)

    gl = jnp.where(lane < N_GROUPS, logits, _NEG)
    gmax = jnp.max(gl, axis=-1, keepdims=True)
    g_w = 1.0 / jnp.sum(jnp.exp(gl - gmax), axis=-1, keepdims=True)
    grp = first_argmax(gl, gmax).astype(jnp.int32)
    lo = N_GROUPS + E_PER_GROUP * grp
    el = jnp.where((lane >= lo) & (lane < lo + E_PER_GROUP), logits, _NEG)
    emax = jnp.max(el, axis=-1, keepdims=True)
    idx1 = first_argmax(el, emax)
    esum = jnp.sum(jnp.exp(el - emax), axis=-1, keepdims=True)
    el2 = jnp.where(lane_f == idx1, _NEG, el)
    e2max = jnp.max(el2, axis=-1, keepdims=True)
    idx2 = first_argmax(el2, e2max)
    p1 = 1.0 / esum
    p2 = jnp.exp(e2max - emax) / esum
    tot = p1 + p2
    w1 = g_w * (p1 / tot)
    w2 = g_w * (p2 / tot)
    route = jnp.where(lane == 0, w1, 0.0)
    route = jnp.where(lane == 1, w2, route)
    route = jnp.where(lane == 2, idx1 - N_GROUPS, route)
    route = jnp.where(lane == 3, idx2 - N_GROUPS, route)
    route_ref[...] = route


def _mixer(x, g1, w_in, w_pool, pscale, convw, w_out, g2, w_r, b_r):
    t = x.shape[0]
    const = lambda shape: pl.BlockSpec(shape, lambda i: (0,) * len(shape),
                                       pipeline_mode=pl.Buffered(1))
    return pl.pallas_call(
        _mixer_kernel,
        grid=(t // TM,),
        in_specs=[
            pl.BlockSpec((TM, D_MODEL), lambda i: (i, 0)),
            const((1, D_MODEL)),
            const(w_in.shape),
            const(w_pool.shape),
            const((1, MIX_A)),
            const((CONV_W, MIX_B)),
            const(w_out.shape),
            const((1, D_MODEL)),
            const(w_r.shape),
            const((1, LANES)),
        ],
        out_specs=[
            pl.BlockSpec((TM, D_MODEL), lambda i: (i, 0)),
            pl.BlockSpec((TM * ROW_SUB, LANES), lambda i: (i, 0)),
            pl.BlockSpec((TM, LANES), lambda i: (i, 0)),
        ],
        out_shape=[
            jax.ShapeDtypeStruct((t, D_MODEL), jnp.float32),
            jax.ShapeDtypeStruct((t * ROW_SUB, LANES), jnp.float32),
            jax.ShapeDtypeStruct((t, LANES), jnp.float32),
        ],
        scratch_shapes=[
            pltpu.VMEM((TM + HIST, MIX_A), jnp.float32),
            pltpu.VMEM((TM + HIST, MIX_B), jnp.float32),
        ],
        compiler_params=pltpu.CompilerParams(
            dimension_semantics=("arbitrary",), vmem_limit_bytes=VMEM_LIMIT),
        name="mixer",
    )(x, g1, w_in, w_pool, pscale, convw, w_out, g2, w_r, b_r)


def _row_copy(src_hbm, row_start, dst_ref, r, sem):
    return pltpu.make_async_copy(src_hbm.at[pl.ds(pl.multiple_of(row_start, ROW_SUB), ROW_SUB)],
                                 dst_ref.at[pl.ds(r * ROW_SUB, ROW_SUB)], sem)


def _start_row_gather(src_hbm, idx_ref, base, dst_ref, sem, n_rows):
    for r in range(n_rows):
        _row_copy(src_hbm, idx_ref[base + r], dst_ref, r, sem).start()


def _wait_row_gather(src_hbm, dst_ref, sem, n_rows):
    pltpu.make_async_copy(src_hbm.at[pl.ds(0, n_rows * ROW_SUB)], dst_ref, sem).wait()


def _weight_copies(w_hbm, e, buf, slot, k, sem):
    rows = w_hbm.shape[1] // WEIGHT_DMA_CHUNKS
    return [pltpu.make_async_copy(w_hbm.at[e, pl.ds(c * rows, rows)],
                                  buf.at[slot, k, pl.ds(c * rows, rows)], sem.at[slot])
            for c in range(WEIGHT_DMA_CHUNKS)]


def _start_weight(*args):
    for cp in _weight_copies(*args):
        cp.start(priority=WEIGHT_DMA_PRIORITY)


def _wait_weight(*args):
    for cp in _weight_copies(*args):
        cp.wait()


def _gate_up_kernel(blk_e_ref, first_ref, wslot_ref, nxt_e_ref, n_used_ref, row_ref,
                    h2_hbm, wg_hbm, wu_hbm, hid_ref,
                    xg, wbuf, wg_bf, wu_bf, gsem, wsem):
    b = pl.program_id(0)
    n_used = n_used_ref[0]
    e = blk_e_ref[b]
    slot = wslot_ref[b]

    @pl.when(b == 0)
    def _():
        _start_weight(wg_hbm, e, wbuf, slot, 0, wsem)
        _start_weight(wu_hbm, e, wbuf, slot, 1, wsem)
        _start_row_gather(h2_hbm, row_ref, 0, xg.at[0], gsem.at[0], ROW_BLK)

    @pl.when(b < n_used)
    def _():
        @pl.when(first_ref[b] == 1)
        def _():
            _wait_weight(wg_hbm, e, wbuf, slot, 0, wsem)
            _wait_weight(wu_hbm, e, wbuf, slot, 1, wsem)
            nxt = nxt_e_ref[b]

            @pl.when(nxt != e)
            def _():
                _start_weight(wg_hbm, nxt, wbuf, 1 - slot, 0, wsem)
                _start_weight(wu_hbm, nxt, wbuf, 1 - slot, 1, wsem)

            wg_bf[...] = wbuf[slot, 0].astype(jnp.bfloat16)
            wu_bf[...] = wbuf[slot, 1].astype(jnp.bfloat16)

        cur = b % 2

        @pl.when(b + 1 < n_used)
        def _():
            _start_row_gather(h2_hbm, row_ref, (b + 1) * ROW_BLK, xg.at[1 - cur], gsem.at[1 - cur],
                              ROW_BLK)

        _wait_row_gather(h2_hbm, xg.at[cur], gsem.at[cur], ROW_BLK)
        xb = _load_rows(xg.at[cur], ROW_BLK).astype(jnp.bfloat16)
        gate = _bdot(xb, wg_bf[...])
        up = _bdot(xb, wu_bf[...])
        hid_ref[...] = (gate * jax.nn.sigmoid(gate) * up).astype(jnp.bfloat16)

    @pl.when(b >= n_used)
    def _():
        hid_ref[...] = jnp.zeros_like(hid_ref)


def _gate_up(plan, h2, wg, wu):
    n_blocks = plan["blk_e"].shape[0]
    any_spec = pl.BlockSpec(memory_space=pl.ANY)
    return pl.pallas_call(
        _gate_up_kernel,
        grid_spec=pltpu.PrefetchScalarGridSpec(
            num_scalar_prefetch=6,
            grid=(n_blocks,),
            in_specs=[any_spec, any_spec, any_spec],
            out_specs=pl.BlockSpec((ROW_BLK, D_EXPERT), lambda b, *_: (b, 0)),
            scratch_shapes=[
                pltpu.VMEM((2, ROW_BLK * ROW_SUB, LANES), jnp.float32),
                pltpu.VMEM((2, 2, D_MODEL, D_EXPERT), jnp.float32),
                pltpu.VMEM((D_MODEL, D_EXPERT), jnp.bfloat16),
                pltpu.VMEM((D_MODEL, D_EXPERT), jnp.bfloat16),
                pltpu.SemaphoreType.DMA((2,)),
                pltpu.SemaphoreType.DMA((2,)),
            ],
        ),
        out_shape=jax.ShapeDtypeStruct((n_blocks * ROW_BLK, D_EXPERT), jnp.bfloat16),
        compiler_params=pltpu.CompilerParams(
            dimension_semantics=("arbitrary",), vmem_limit_bytes=VMEM_LIMIT),
        name="gate_up",
    )(plan["blk_e"], plan["first"], plan["wslot"], plan["nxt_e"], plan["n_used"], plan["row_src"],
      h2, wg, wu)


def _down_kernel(blk_e_ref, first_ref, wslot_ref, nxt_e_ref, n_used_ref,
                 hid_ref, wd_hbm, y_ref, wbuf, wd_bf, wsem):
    b = pl.program_id(0)
    n_used = n_used_ref[0]
    e = blk_e_ref[b]
    slot = wslot_ref[b]

    @pl.when(b == 0)
    def _():
        _start_weight(wd_hbm, e, wbuf, slot, 0, wsem)

    @pl.when(b < n_used)
    def _():
        @pl.when(first_ref[b] == 1)
        def _():
            _wait_weight(wd_hbm, e, wbuf, slot, 0, wsem)
            nxt = nxt_e_ref[b]

            @pl.when(nxt != e)
            def _():
                _start_weight(wd_hbm, nxt, wbuf, 1 - slot, 0, wsem)

            wd_bf[...] = wbuf[slot, 0].astype(jnp.bfloat16)

        _store_rows(y_ref, _bdot(hid_ref[...], wd_bf[...]))

    @pl.when(b >= n_used)
    def _():
        y_ref[...] = jnp.zeros_like(y_ref)


def _down(plan, hid, wd):
    n_blocks = plan["blk_e"].shape[0]
    return pl.pallas_call(
        _down_kernel,
        grid_spec=pltpu.PrefetchScalarGridSpec(
            num_scalar_prefetch=5,
            grid=(n_blocks,),
            in_specs=[pl.BlockSpec((ROW_BLK, D_EXPERT), lambda b, *_: (b, 0)),
                      pl.BlockSpec(memory_space=pl.ANY)],
            out_specs=pl.BlockSpec((ROW_BLK * ROW_SUB, LANES), lambda b, *_: (b, 0)),
            scratch_shapes=[
                pltpu.VMEM((2, 1, D_EXPERT, D_MODEL), jnp.float32),
                pltpu.VMEM((D_EXPERT, D_MODEL), jnp.bfloat16),
                pltpu.SemaphoreType.DMA((2,)),
            ],
        ),
        out_shape=jax.ShapeDtypeStruct((n_blocks * ROW_BLK * ROW_SUB, LANES), jnp.float32),
        compiler_params=pltpu.CompilerParams(
            dimension_semantics=("arbitrary",), vmem_limit_bytes=VMEM_LIMIT),
        name="down",
    )(plan["blk_e"], plan["first"], plan["wslot"], plan["nxt_e"], plan["n_used"], hid, wd)


def _combine_kernel(d0_ref, d1_ref, x1_ref, route_ref, g_ref, y_hbm, o_ref, y0_buf, y1_buf, sem):
    i = pl.program_id(0)
    n = pl.num_programs(0)
    cur = i % 2

    def start(step, slot):
        _start_row_gather(y_hbm, d0_ref, step * TM, y0_buf.at[slot], sem.at[0, slot], TM)
        _start_row_gather(y_hbm, d1_ref, step * TM, y1_buf.at[slot], sem.at[1, slot], TM)

    @pl.when(i == 0)
    def _():
        start(0, 0)

    @pl.when(i + 1 < n)
    def _():
        start(i + 1, 1 - cur)

    r = route_ref[...]
    _wait_row_gather(y_hbm, y0_buf.at[cur], sem.at[0, cur], TM)
    _wait_row_gather(y_hbm, y1_buf.at[cur], sem.at[1, cur], TM)
    y0 = _load_rows(y0_buf.at[cur], TM)
    y1 = _load_rows(y1_buf.at[cur], TM)
    xo = x1_ref[...] + (y0 * r[:, 0:1] + y1 * r[:, 1:2])
    o_ref[...] = _rms(xo, g_ref[...])


def _combine(d0, d1, x1, route, g, ybuf):
    t = x1.shape[0]
    tile = lambda w: pl.BlockSpec((TM, w), lambda i, *_: (i, 0))
    return pl.pallas_call(
        _combine_kernel,
        grid_spec=pltpu.PrefetchScalarGridSpec(
            num_scalar_prefetch=2,
            grid=(t // TM,),
            in_specs=[tile(D_MODEL), tile(LANES),
                      pl.BlockSpec((1, D_MODEL), lambda i, *_: (0, 0)),
                      pl.BlockSpec(memory_space=pl.ANY)],
            out_specs=tile(D_MODEL),
            scratch_shapes=[pltpu.VMEM((2, TM * ROW_SUB, LANES), jnp.float32),
                            pltpu.VMEM((2, TM * ROW_SUB, LANES), jnp.float32),
                            pltpu.SemaphoreType.DMA((2, 2))],
        ),
        out_shape=jax.ShapeDtypeStruct((t, D_MODEL), jnp.float32),
        compiler_params=pltpu.CompilerParams(
            dimension_semantics=("arbitrary",), vmem_limit_bytes=VMEM_LIMIT),
        name="combine",
    )(d0, d1, x1, route, g, ybuf)


def _dispatch_plan(expert_id):
    t = expert_id.shape[0]
    a = t * TOP_K
    i32 = jnp.int32
    e_flat = expert_id.reshape(a)
    experts = jnp.arange(N_EXPERTS, dtype=i32)
    onehot = (e_flat[:, None] == experts[None, :]).astype(i32)
    rank = jnp.cumsum(onehot, axis=0) - onehot
    pos = jnp.sum(rank * onehot, axis=1)
    counts = jnp.sum(onehot, axis=0)
    padded = ((counts + ROW_BLK - 1) // ROW_BLK) * ROW_BLK
    pends = jnp.cumsum(padded)
    pstarts = pends - padded
    dest = (pstarts[e_flat] + pos).astype(i32)
    n_blocks = (a + N_EXPERTS * (ROW_BLK - 1) + ROW_BLK - 1) // ROW_BLK
    blk_start = jnp.arange(n_blocks, dtype=i32) * ROW_BLK
    blk_e = jnp.clip(jnp.sum(pends[None, :] <= blk_start[:, None], axis=1), 0, N_EXPERTS - 1).astype(i32)
    first = jnp.concatenate([jnp.ones((1,), i32), (blk_e[1:] != blk_e[:-1]).astype(i32)])
    wslot = (jnp.cumsum(first) - 1) % 2
    later_used = (experts[None, :] > experts[:, None]) & (counts[None, :] > 0)
    nxt = jnp.min(jnp.where(later_used, experts[None, :], N_EXPERTS), axis=1)
    nxt = jnp.where(nxt == N_EXPERTS, experts, nxt)
    tok = jnp.repeat(jnp.arange(t, dtype=i32), TOP_K)
    row_tok = jnp.zeros((n_blocks * ROW_BLK,), i32).at[dest].set(tok)
    return {
        "dest": dest,
        "blk_e": blk_e,
        "first": first,
        "wslot": wslot.astype(i32),
        "nxt_e": nxt[blk_e].astype(i32),
        "n_used": (pends[-1] // ROW_BLK).astype(i32).reshape(1),
        "row_src": row_tok * ROW_SUB,
    }


def kernel(x, norm_mix_g, w_in, w_pool, pool_scale, conv_w, w_out, norm_ffn_g, w_router_group, b_router_group, w_router_expert, b_router_expert, w_gate, w_up, w_down, norm_final_g):
    bt, s, d = x.shape
    t = bt * s
    bf = jnp.bfloat16
    xt = x.reshape(t, d)
    assert norm_mix_g.shape[0] == 1, "the final norm is fused into the last layer's combine"
    for l in range(norm_mix_g.shape[0]):
        w_r = jnp.concatenate([w_router_group[l], w_router_expert[l]], axis=1)
        w_r = jnp.pad(w_r, ((0, 0), (0, LANES - w_r.shape[1]))).astype(bf)
        b_r = jnp.concatenate([b_router_group[l], b_router_expert[l]])
        b_r = jnp.pad(b_r, (0, LANES - b_r.shape[0])).reshape(1, LANES)
        x1, h2, route = _mixer(
            xt, norm_mix_g[l].reshape(1, d), w_in[l].astype(bf), w_pool[l].astype(bf),
            pool_scale[l].reshape(1, MIX_A), conv_w[l].T, w_out[l].astype(bf),
            norm_ffn_g[l].reshape(1, d), w_r, b_r)
        plan = _dispatch_plan(route[:, 2:4].astype(jnp.int32))
        hid = _gate_up(plan, h2, w_gate[l], w_up[l])
        ybuf = _down(plan, hid, w_down[l])
        dest2 = plan["dest"].reshape(t, TOP_K) * ROW_SUB
        xt = _combine(dest2[:, 0], dest2[:, 1], x1, route, norm_final_g.reshape(1, d), ybuf)
    return xt.reshape(bt, s, d)
```

```python
import jax
import jax.numpy as jnp
from jax import lax
from jax.experimental import pallas as pl
from jax.experimental.pallas import tpu as pltpu

D_MODEL = 2048
MIX_A = 1024
MIX_B = 1024
POOL_WINDOWS = (2, 4, 8, 16)
POOL_CH = MIX_A // len(POOL_WINDOWS)
CONV_W = 3
N_GROUPS = 4
E_PER_GROUP = 8
N_EXPERTS = N_GROUPS * E_PER_GROUP
TOP_K = 2
D_EXPERT = D_MODEL // 2
EPS = 1e-6

LANES = 128
ROW_SUB = D_MODEL // LANES
HIST = 16
TM = 256
ROW_BLK = 256
VMEM_LIMIT = 56 * 1024 * 1024
WEIGHT_DMA_PRIORITY = 1
WEIGHT_DMA_CHUNKS = 8
DMA_QUEUES = 2
GATHER_PHASES = 4

_NEG = -1e30


def _rms(x, g):
    return x * lax.rsqrt(jnp.mean(x * x, axis=-1, keepdims=True) + EPS) * g


def _bdot(a, b):
    return jnp.dot(a, b, preferred_element_type=jnp.float32)


def _store_rows(ref, val):
    n = val.shape[0]
    for c in range(ROW_SUB):
        ref[pl.ds(c, n, stride=ROW_SUB), :] = val[:, c * LANES:(c + 1) * LANES]


def _load_rows(ref, n):
    return jnp.concatenate([ref[pl.ds(c, n, stride=ROW_SUB), :] for c in range(ROW_SUB)], axis=-1)


def _mixer_kernel(x_ref, g1_ref, w_in_ref, w_pool_ref, pscale_ref, convw_ref, w_out_ref,
                  g2_ref, w_r_ref, b_r_ref,
                  x1_ref, h2_ref, route_ref,
                  ext_u, ext_z):
    i = pl.program_id(0)

    @pl.when(i == 0)
    def _():
        ext_u[0:HIST, :] = jnp.zeros((HIST, MIX_A), jnp.float32)
        ext_z[0:HIST, :] = jnp.zeros((HIST, MIX_B), jnp.float32)

    x = x_ref[...]
    h = _rms(x, g1_ref[...]).astype(jnp.bfloat16)

    ext_u[HIST:HIST + TM, :] = _bdot(h, w_in_ref[:, 0:MIX_A])
    row = lax.broadcasted_iota(jnp.int32, (TM, 1), 0) + i * TM + 1
    y_a = []
    for gi, w in enumerate(POOL_WINDOWS):
        c0 = gi * POOL_CH
        u = ext_u[HIST:HIST + TM, c0:c0 + POOL_CH]
        acc = u
        for s in range(1, w):
            acc = acc + ext_u[HIST - s:HIST - s + TM, c0:c0 + POOL_CH]
        cnt = jnp.minimum(row, w).astype(jnp.float32)
        pooled = (acc / cnt - u).astype(jnp.bfloat16)
        y_a.append(_bdot(pooled, w_pool_ref[gi]))
    y_a = jnp.concatenate(y_a, axis=-1) * pscale_ref[...]
    ext_u[0:HIST, :] = ext_u[TM:TM + HIST, :]

    b_gate = _bdot(h, w_in_ref[:, MIX_A:MIX_A + MIX_B])
    c_gate = _bdot(h, w_in_ref[:, MIX_A + MIX_B:MIX_A + 2 * MIX_B])
    v = _bdot(h, w_in_ref[:, MIX_A + 2 * MIX_B:MIX_A + 3 * MIX_B])
    z = c_gate * v
    ext_z[HIST:HIST + TM, :] = z
    y = ext_z[HIST - 2:HIST - 2 + TM, :] * convw_ref[0:1, :]
    y = y + ext_z[HIST - 1:HIST - 1 + TM, :] * convw_ref[1:2, :]
    y = y + z * convw_ref[2:3, :]
    y_b = b_gate * y
    ext_z[0:HIST, :] = ext_z[TM:TM + HIST, :]

    mixed = jnp.concatenate([y_a, y_b], axis=-1).astype(jnp.bfloat16)
    x1 = x + _bdot(mixed, w_out_ref[...])
    x1_ref[...] = x1

    h2_f32 = _rms(x1, g2_ref[...])
    _store_rows(h2_ref, h2_f32)
    h2 = h2_f32.astype(jnp.bfloat16)

    logits = _bdot(h2, w_r_ref[...]) + b_r_ref[...]
    lane = lax.broadcasted_iota(jnp.int32, (TM, LANES), 1)
    lane_f = lane.astype(jnp.float32)

    def first_argmax(vals, vmax):
        return jnp.min(jnp.where(vals == vmax, lane_f, float(LANES)), axis=-1, keepdims=True)

    gl = jnp.where(lane < N_GROUPS, logits, _NEG)
    gmax = jnp.max(gl, axis=-1, keepdims=True)
    g_w = 1.0 / jnp.sum(jnp.exp(gl - gmax), axis=-1, keepdims=True)
    grp = first_argmax(gl, gmax).astype(jnp.int32)
    lo = N_GROUPS + E_PER_GROUP * grp
    el = jnp.where((lane >= lo) & (lane < lo + E_PER_GROUP), logits, _NEG)
    emax = jnp.max(el, axis=-1, keepdims=True)
    idx1 = first_argmax(el, emax)
    esum = jnp.sum(jnp.exp(el - emax), axis=-1, keepdims=True)
    el2 = jnp.where(lane_f == idx1, _NEG, el)
    e2max = jnp.max(el2, axis=-1, keepdims=True)
    idx2 = first_argmax(el2, e2max)
    p1 = 1.0 / esum
    p2 = jnp.exp(e2max - emax) / esum
    tot = p1 + p2
    w1 = g_w * (p1 / tot)
    w2 = g_w * (p2 / tot)
    route = jnp.where(lane == 0, w1, 0.0)
    route = jnp.where(lane == 1, w2, route)
    route = jnp.where(lane == 2, idx1 - N_GROUPS, route)
    route = jnp.where(lane == 3, idx2 - N_GROUPS, route)
    route_ref[...] = route


def _mixer(x, g1, w_in, w_pool, pscale, convw, w_out, g2, w_r, b_r):
    t = x.shape[0]
    const = lambda shape: pl.BlockSpec(shape, lambda i: (0,) * len(shape),
                                       pipeline_mode=pl.Buffered(1))
    return pl.pallas_call(
        _mixer_kernel,
        grid=(t // TM,),
        in_specs=[
            pl.BlockSpec((TM, D_MODEL), lambda i: (i, 0)),
            const((1, D_MODEL)),
            const(w_in.shape),
            const(w_pool.shape),
            const((1, MIX_A)),
            const((CONV_W, MIX_B)),
            const(w_out.shape),
            const((1, D_MODEL)),
            const(w_r.shape),
            const((1, LANES)),
        ],
        out_specs=[
            pl.BlockSpec((TM, D_MODEL), lambda i: (i, 0)),
            pl.BlockSpec((TM * ROW_SUB, LANES), lambda i: (i, 0)),
            pl.BlockSpec((TM, LANES), lambda i: (i, 0)),
        ],
        out_shape=[
            jax.ShapeDtypeStruct((t, D_MODEL), jnp.float32),
            jax.ShapeDtypeStruct((t * ROW_SUB, LANES), jnp.float32),
            jax.ShapeDtypeStruct((t, LANES), jnp.float32),
        ],
        scratch_shapes=[
            pltpu.VMEM((TM + HIST, MIX_A), jnp.float32),
            pltpu.VMEM((TM + HIST, MIX_B), jnp.float32),
        ],
        compiler_params=pltpu.CompilerParams(
            dimension_semantics=("arbitrary",), vmem_limit_bytes=VMEM_LIMIT),
        name="mixer",
    )(x, g1, w_in, w_pool, pscale, convw, w_out, g2, w_r, b_r)


def _row_copy(src_hbm, row_start, dst_ref, r, sem):
    return pltpu.make_async_copy(src_hbm.at[pl.ds(pl.multiple_of(row_start, ROW_SUB), ROW_SUB)],
                                 dst_ref.at[pl.ds(r * ROW_SUB, ROW_SUB)], sem)


def _start_row_gather(src_hbm, idx_ref, base, dst_ref, sem, rows, n_queues=1):
    for r in rows:
        _row_copy(src_hbm, idx_ref[base + r], dst_ref, r, sem).start(priority=r % n_queues)


def _wait_row_gather(src_hbm, dst_ref, sem, n_rows):
    pltpu.make_async_copy(src_hbm.at[pl.ds(0, n_rows * ROW_SUB)], dst_ref, sem).wait()


def _weight_copies(w_hbm, e, buf, slot, k, sem):
    rows = w_hbm.shape[1] // WEIGHT_DMA_CHUNKS
    return [pltpu.make_async_copy(w_hbm.at[e, pl.ds(c * rows, rows)],
                                  buf.at[slot, k, pl.ds(c * rows, rows)], sem.at[slot])
            for c in range(WEIGHT_DMA_CHUNKS)]


def _start_weight(*args):
    for cp in _weight_copies(*args):
        cp.start(priority=WEIGHT_DMA_PRIORITY)


def _wait_weight(*args):
    for cp in _weight_copies(*args):
        cp.wait()


def _gate_up_kernel(blk_e_ref, first_ref, wslot_ref, nxt_e_ref, n_used_ref, row_ref,
                    h2_hbm, wg_hbm, wu_hbm, hid_ref,
                    xg, xb_s, wbuf, wg_bf, wu_bf, gsem, wsem):
    b = pl.program_id(0)
    n_used = n_used_ref[0]
    e = blk_e_ref[b]
    slot = wslot_ref[b]

    @pl.when(b == 0)
    def _():
        _start_weight(wg_hbm, e, wbuf, slot, 0, wsem)
        _start_weight(wu_hbm, e, wbuf, slot, 1, wsem)
        _start_row_gather(h2_hbm, row_ref, 0, xg.at[0], gsem.at[0], range(ROW_BLK))

    @pl.when(b < n_used)
    def _():
        @pl.when(first_ref[b] == 1)
        def _():
            _wait_weight(wg_hbm, e, wbuf, slot, 0, wsem)
            _wait_weight(wu_hbm, e, wbuf, slot, 1, wsem)
            nxt = nxt_e_ref[b]

            @pl.when(nxt != e)
            def _():
                _start_weight(wg_hbm, nxt, wbuf, 1 - slot, 0, wsem)
                _start_weight(wu_hbm, nxt, wbuf, 1 - slot, 1, wsem)

            wg_bf[...] = wbuf[slot, 0].astype(jnp.bfloat16)
            wu_bf[...] = wbuf[slot, 1].astype(jnp.bfloat16)

        cur = b % 2
        _wait_row_gather(h2_hbm, xg.at[cur], gsem.at[cur], ROW_BLK)
        xb_s[...] = _load_rows(xg.at[cur], ROW_BLK).astype(jnp.bfloat16)
        rows_per_phase = ROW_BLK // GATHER_PHASES
        cols = D_EXPERT // GATHER_PHASES
        for c in range(GATHER_PHASES):
            @pl.when(b + 1 < n_used)
            def _():
                _start_row_gather(h2_hbm, row_ref, (b + 1) * ROW_BLK, xg.at[1 - cur], gsem.at[1 - cur],
                                  range(c * rows_per_phase, (c + 1) * rows_per_phase))

            cs = slice(c * cols, (c + 1) * cols)
            xb = xb_s[...]
            gate = _bdot(xb, wg_bf[:, cs])
            up = _bdot(xb, wu_bf[:, cs])
            hid_ref[:, cs] = (gate * jax.nn.sigmoid(gate) * up).astype(jnp.bfloat16)

    @pl.when(b >= n_used)
    def _():
        hid_ref[...] = jnp.zeros_like(hid_ref)


def _gate_up(plan, h2, wg, wu):
    n_blocks = plan["blk_e"].shape[0]
    any_spec = pl.BlockSpec(memory_space=pl.ANY)
    return pl.pallas_call(
        _gate_up_kernel,
        grid_spec=pltpu.PrefetchScalarGridSpec(
            num_scalar_prefetch=6,
            grid=(n_blocks,),
            in_specs=[any_spec, any_spec, any_spec],
            out_specs=pl.BlockSpec((ROW_BLK, D_EXPERT), lambda b, *_: (b, 0)),
            scratch_shapes=[
                pltpu.VMEM((2, ROW_BLK * ROW_SUB, LANES), jnp.float32),
                pltpu.VMEM((ROW_BLK, D_MODEL), jnp.bfloat16),
                pltpu.VMEM((2, 2, D_MODEL, D_EXPERT), jnp.float32),
                pltpu.VMEM((D_MODEL, D_EXPERT), jnp.bfloat16),
                pltpu.VMEM((D_MODEL, D_EXPERT), jnp.bfloat16),
                pltpu.SemaphoreType.DMA((2,)),
                pltpu.SemaphoreType.DMA((2,)),
            ],
        ),
        out_shape=jax.ShapeDtypeStruct((n_blocks * ROW_BLK, D_EXPERT), jnp.bfloat16),
        compiler_params=pltpu.CompilerParams(
            dimension_semantics=("arbitrary",), vmem_limit_bytes=VMEM_LIMIT),
        name="gate_up",
    )(plan["blk_e"], plan["first"], plan["wslot"], plan["nxt_e"], plan["n_used"], plan["row_src"],
      h2, wg, wu)


def _down_kernel(blk_e_ref, first_ref, wslot_ref, nxt_e_ref, n_used_ref,
                 hid_ref, wd_hbm, y_ref, wbuf, wd_bf, wsem):
    b = pl.program_id(0)
    n_used = n_used_ref[0]
    e = blk_e_ref[b]
    slot = wslot_ref[b]

    @pl.when(b == 0)
    def _():
        _start_weight(wd_hbm, e, wbuf, slot, 0, wsem)

    @pl.when(b < n_used)
    def _():
        @pl.when(first_ref[b] == 1)
        def _():
            _wait_weight(wd_hbm, e, wbuf, slot, 0, wsem)
            nxt = nxt_e_ref[b]

            @pl.when(nxt != e)
            def _():
                _start_weight(wd_hbm, nxt, wbuf, 1 - slot, 0, wsem)

            wd_bf[...] = wbuf[slot, 0].astype(jnp.bfloat16)

        _store_rows(y_ref, _bdot(hid_ref[...], wd_bf[...]))

    @pl.when(b >= n_used)
    def _():
        y_ref[...] = jnp.zeros_like(y_ref)


def _down(plan, hid, wd):
    n_blocks = plan["blk_e"].shape[0]
    return pl.pallas_call(
        _down_kernel,
        grid_spec=pltpu.PrefetchScalarGridSpec(
            num_scalar_prefetch=5,
            grid=(n_blocks,),
            in_specs=[pl.BlockSpec((ROW_BLK, D_EXPERT), lambda b, *_: (b, 0)),
                      pl.BlockSpec(memory_space=pl.ANY)],
            out_specs=pl.BlockSpec((ROW_BLK * ROW_SUB, LANES), lambda b, *_: (b, 0)),
            scratch_shapes=[
                pltpu.VMEM((2, 1, D_EXPERT, D_MODEL), jnp.float32),
                pltpu.VMEM((D_EXPERT, D_MODEL), jnp.bfloat16),
                pltpu.SemaphoreType.DMA((2,)),
            ],
        ),
        out_shape=jax.ShapeDtypeStruct((n_blocks * ROW_BLK * ROW_SUB, LANES), jnp.float32),
        compiler_params=pltpu.CompilerParams(
            dimension_semantics=("arbitrary",), vmem_limit_bytes=VMEM_LIMIT),
        name="down",
    )(plan["blk_e"], plan["first"], plan["wslot"], plan["nxt_e"], plan["n_used"], hid, wd)


def _combine_kernel(d0_ref, d1_ref, x1_ref, route_ref, g_ref, y_hbm, o_ref, y0_buf, y1_buf, sem):
    i = pl.program_id(0)
    n = pl.num_programs(0)
    cur = i % 2

    def start(step, slot):
        _start_row_gather(y_hbm, d0_ref, step * TM, y0_buf.at[slot], sem.at[0, slot], range(TM),
                          n_queues=DMA_QUEUES)
        _start_row_gather(y_hbm, d1_ref, step * TM, y1_buf.at[slot], sem.at[1, slot], range(TM),
                          n_queues=DMA_QUEUES)

    @pl.when(i == 0)
    def _():
        start(0, 0)

    @pl.when(i + 1 < n)
    def _():
        start(i + 1, 1 - cur)

    r = route_ref[...]
    _wait_row_gather(y_hbm, y0_buf.at[cur], sem.at[0, cur], TM)
    _wait_row_gather(y_hbm, y1_buf.at[cur], sem.at[1, cur], TM)
    y0 = _load_rows(y0_buf.at[cur], TM)
    y1 = _load_rows(y1_buf.at[cur], TM)
    xo = x1_ref[...] + (y0 * r[:, 0:1] + y1 * r[:, 1:2])
    o_ref[...] = _rms(xo, g_ref[...])


def _combine(d0, d1, x1, route, g, ybuf):
    t = x1.shape[0]
    tile = lambda w: pl.BlockSpec((TM, w), lambda i, *_: (i, 0))
    return pl.pallas_call(
        _combine_kernel,
        grid_spec=pltpu.PrefetchScalarGridSpec(
            num_scalar_prefetch=2,
            grid=(t // TM,),
            in_specs=[tile(D_MODEL), tile(LANES),
                      pl.BlockSpec((1, D_MODEL), lambda i, *_: (0, 0)),
                      pl.BlockSpec(memory_space=pl.ANY)],
            out_specs=tile(D_MODEL),
            scratch_shapes=[pltpu.VMEM((2, TM * ROW_SUB, LANES), jnp.float32),
                            pltpu.VMEM((2, TM * ROW_SUB, LANES), jnp.float32),
                            pltpu.SemaphoreType.DMA((2, 2))],
        ),
        out_shape=jax.ShapeDtypeStruct((t, D_MODEL), jnp.float32),
        compiler_params=pltpu.CompilerParams(
            dimension_semantics=("arbitrary",), vmem_limit_bytes=VMEM_LIMIT),
        name="combine",
    )(d0, d1, x1, route, g, ybuf)


def _dispatch_plan(expert_id):
    t = expert_id.shape[0]
    a = t * TOP_K
    i32 = jnp.int32
    e_flat = expert_id.reshape(a)
    experts = jnp.arange(N_EXPERTS, dtype=i32)
    onehot = (e_flat[:, None] == experts[None, :]).astype(i32)
    rank = jnp.cumsum(onehot, axis=0) - onehot
    pos = jnp.sum(rank * onehot, axis=1)
    counts = jnp.sum(onehot, axis=0)
    padded = ((counts + ROW_BLK - 1) // ROW_BLK) * ROW_BLK
    pends = jnp.cumsum(padded)
    pstarts = pends - padded
    dest = (pstarts[e_flat] + pos).astype(i32)
    n_blocks = (a + N_EXPERTS * (ROW_BLK - 1) + ROW_BLK - 1) // ROW_BLK
    blk_start = jnp.arange(n_blocks, dtype=i32) * ROW_BLK
    blk_e = jnp.clip(jnp.sum(pends[None, :] <= blk_start[:, None], axis=1), 0, N_EXPERTS - 1).astype(i32)
    first = jnp.concatenate([jnp.ones((1,), i32), (blk_e[1:] != blk_e[:-1]).astype(i32)])
    wslot = (jnp.cumsum(first) - 1) % 2
    later_used = (experts[None, :] > experts[:, None]) & (counts[None, :] > 0)
    nxt = jnp.min(jnp.where(later_used, experts[None, :], N_EXPERTS), axis=1)
    nxt = jnp.where(nxt == N_EXPERTS, experts, nxt)
    tok = jnp.repeat(jnp.arange(t, dtype=i32), TOP_K)
    row_tok = jnp.zeros((n_blocks * ROW_BLK,), i32).at[dest].set(tok)
    return {
        "dest": dest,
        "blk_e": blk_e,
        "first": first,
        "wslot": wslot.astype(i32),
        "nxt_e": nxt[blk_e].astype(i32),
        "n_used": (pends[-1] // ROW_BLK).astype(i32).reshape(1),
        "row_src": row_tok * ROW_SUB,
    }


def kernel(x, norm_mix_g, w_in, w_pool, pool_scale, conv_w, w_out, norm_ffn_g, w_router_group, b_router_group, w_router_expert, b_router_expert, w_gate, w_up, w_down, norm_final_g):
    bt, s, d = x.shape
    t = bt * s
    bf = jnp.bfloat16
    xt = x.reshape(t, d)
    assert norm_mix_g.shape[0] == 1, "the final norm is fused into the last layer's combine"
    for l in range(norm_mix_g.shape[0]):
        w_r = jnp.concatenate([w_router_group[l], w_router_expert[l]], axis=1)
        w_r = jnp.pad(w_r, ((0, 0), (0, LANES - w_r.shape[1]))).astype(bf)
        b_r = jnp.concatenate([b_router_group[l], b_router_expert[l]])
        b_r = jnp.pad(b_r, (0, LANES - b_r.shape[0])).reshape(1, LANES)
        x1, h2, route = _mixer(
            xt, norm_mix_g[l].reshape(1, d), w_in[l].astype(bf), w_pool[l].astype(bf),
            pool_scale[l].reshape(1, MIX_A), conv_w[l].T, w_out[l].astype(bf),
            norm_ffn_g[l].reshape(1, d), w_r, b_r)
        plan = _dispatch_plan(route[:, 2:4].astype(jnp.int32))
        hid = _gate_up(plan, h2, w_gate[l], w_up[l])
        ybuf = _down(plan, hid, w_down[l])
        dest2 = plan["dest"].reshape(t, TOP_K) * ROW_SUB
        xt = _combine(dest2[:, 0], dest2[:, 1], x1, route, norm_final_g.reshape(1, d), ybuf)
    return xt.reshape(bt, s, d)
```

```python
import jax
import jax.numpy as jnp
from jax import lax
from jax.experimental import pallas as pl
from jax.experimental.pallas import tpu as pltpu

D_MODEL = 2048
MIX_A = 1024
MIX_B = 1024
POOL_WINDOWS = (2, 4, 8, 16)
POOL_CH = MIX_A // len(POOL_WINDOWS)
CONV_W = 3
N_GROUPS = 4
E_PER_GROUP = 8
N_EXPERTS = N_GROUPS * E_PER_GROUP
TOP_K = 2
D_EXPERT = D_MODEL // 2
EPS = 1e-6

LANES = 128
ROW_SUB = D_MODEL // LANES
HIST = 16
TM = 256
ROW_BLK = 256
VMEM_LIMIT = 56 * 1024 * 1024
WEIGHT_DMA_PRIORITY = 1
WEIGHT_DMA_CHUNKS = 8
DMA_QUEUES = 2
GATHER_SHIFT = 2
GATHER_UNROLL = 1 << GATHER_SHIFT

_NEG = -1e30


def _rms(x, g):
    return x * lax.rsqrt(jnp.mean(x * x, axis=-1, keepdims=True) + EPS) * g


def _bdot(a, b):
    return jnp.dot(a, b, preferred_element_type=jnp.float32)


def _store_rows(ref, val):
    n = val.shape[0]
    for c in range(ROW_SUB):
        ref[pl.ds(c, n, stride=ROW_SUB), :] = val[:, c * LANES:(c + 1) * LANES]


def _load_rows(ref, n):
    return jnp.concatenate([ref[pl.ds(c, n, stride=ROW_SUB), :] for c in range(ROW_SUB)], axis=-1)


def _mixer_kernel(x_ref, g1_ref, w_in_ref, w_pool_ref, pscale_ref, convw_ref, w_out_ref,
                  g2_ref, w_r_ref, b_r_ref,
                  x1_ref, h2_ref, route_ref,
                  ext_u, ext_z):
    i = pl.program_id(0)

    @pl.when(i == 0)
    def _():
        ext_u[0:HIST, :] = jnp.zeros((HIST, MIX_A), jnp.float32)
        ext_z[0:HIST, :] = jnp.zeros((HIST, MIX_B), jnp.float32)

    x = x_ref[...]
    h = _rms(x, g1_ref[...]).astype(jnp.bfloat16)

    ext_u[HIST:HIST + TM, :] = _bdot(h, w_in_ref[:, 0:MIX_A])
    c_gate = _bdot(h, w_in_ref[:, MIX_A + MIX_B:MIX_A + 2 * MIX_B])
    v = _bdot(h, w_in_ref[:, MIX_A + 2 * MIX_B:MIX_A + 3 * MIX_B])
    z = c_gate * v
    ext_z[HIST:HIST + TM, :] = z
    b_gate = _bdot(h, w_in_ref[:, MIX_A:MIX_A + MIX_B])

    row = lax.broadcasted_iota(jnp.int32, (TM, 1), 0) + i * TM + 1
    y_a = []
    for gi, w in enumerate(POOL_WINDOWS):
        c0 = gi * POOL_CH
        u = ext_u[HIST:HIST + TM, c0:c0 + POOL_CH]
        acc = u
        for s in range(1, w):
            acc = acc + ext_u[HIST - s:HIST - s + TM, c0:c0 + POOL_CH]
        cnt = jnp.minimum(row, w).astype(jnp.float32)
        pooled = (acc / cnt - u).astype(jnp.bfloat16)
        y_a.append(_bdot(pooled, w_pool_ref[gi]))
    y_a = jnp.concatenate(y_a, axis=-1) * pscale_ref[...]
    ext_u[0:HIST, :] = ext_u[TM:TM + HIST, :]

    y = ext_z[HIST - 2:HIST - 2 + TM, :] * convw_ref[0:1, :]
    y = y + ext_z[HIST - 1:HIST - 1 + TM, :] * convw_ref[1:2, :]
    y = y + z * convw_ref[2:3, :]
    y_b = b_gate * y
    ext_z[0:HIST, :] = ext_z[TM:TM + HIST, :]

    mixed = jnp.concatenate([y_a, y_b], axis=-1).astype(jnp.bfloat16)
    x1 = x + _bdot(mixed, w_out_ref[...])
    x1_ref[...] = x1

    h2_f32 = _rms(x1, g2_ref[...])
    _store_rows(h2_ref, h2_f32)
    h2 = h2_f32.astype(jnp.bfloat16)

    logits = _bdot(h2, w_r_ref[...]) + b_r_ref[...]
    lane = lax.broadcasted_iota(jnp.int32, (TM, LANES), 1)
    lane_f = lane.astype(jnp.float32)

    def first_argmax(vals, vmax):
        return jnp.min(jnp.where(vals == vmax, lane_f, float(LANES)), axis=-1, keepdims=True)

    gl = jnp.where(lane < N_GROUPS, logits, _NEG)
    gmax = jnp.max(gl, axis=-1, keepdims=True)
    g_w = 1.0 / jnp.sum(jnp.exp(gl - gmax), axis=-1, keepdims=True)
    grp = first_argmax(gl, gmax).astype(jnp.int32)
    lo = N_GROUPS + E_PER_GROUP * grp
    el = jnp.where((lane >= lo) & (lane < lo + E_PER_GROUP), logits, _NEG)
    emax = jnp.max(el, axis=-1, keepdims=True)
    idx1 = first_argmax(el, emax)
    esum = jnp.sum(jnp.exp(el - emax), axis=-1, keepdims=True)
    el2 = jnp.where(lane_f == idx1, _NEG, el)
    e2max = jnp.max(el2, axis=-1, keepdims=True)
    idx2 = first_argmax(el2, e2max)
    p1 = 1.0 / esum
    p2 = jnp.exp(e2max - emax) / esum
    tot = p1 + p2
    w1 = g_w * (p1 / tot)
    w2 = g_w * (p2 / tot)
    route = jnp.where(lane == 0, w1, 0.0)
    route = jnp.where(lane == 1, w2, route)
    route = jnp.where(lane == 2, idx1 - N_GROUPS, route)
    route = jnp.where(lane == 3, idx2 - N_GROUPS, route)
    route_ref[...] = route


def _mixer(x, g1, w_in, w_pool, pscale, convw, w_out, g2, w_r, b_r):
    t = x.shape[0]
    const = lambda shape: pl.BlockSpec(shape, lambda i: (0,) * len(shape),
                                       pipeline_mode=pl.Buffered(1))
    return pl.pallas_call(
        _mixer_kernel,
        grid=(t // TM,),
        in_specs=[
            pl.BlockSpec((TM, D_MODEL), lambda i: (i, 0)),
            const((1, D_MODEL)),
            const(w_in.shape),
            const(w_pool.shape),
            const((1, MIX_A)),
            const((CONV_W, MIX_B)),
            const(w_out.shape),
            const((1, D_MODEL)),
            const(w_r.shape),
            const((1, LANES)),
        ],
        out_specs=[
            pl.BlockSpec((TM, D_MODEL), lambda i: (i, 0)),
            pl.BlockSpec((TM * ROW_SUB, LANES), lambda i: (i, 0)),
            pl.BlockSpec((TM, LANES), lambda i: (i, 0)),
        ],
        out_shape=[
            jax.ShapeDtypeStruct((t, D_MODEL), jnp.float32),
            jax.ShapeDtypeStruct((t * ROW_SUB, LANES), jnp.float32),
            jax.ShapeDtypeStruct((t, LANES), jnp.float32),
        ],
        scratch_shapes=[
            pltpu.VMEM((TM + HIST, MIX_A), jnp.float32),
            pltpu.VMEM((TM + HIST, MIX_B), jnp.float32),
        ],
        compiler_params=pltpu.CompilerParams(
            dimension_semantics=("arbitrary",), vmem_limit_bytes=VMEM_LIMIT),
        name="mixer",
    )(x, g1, w_in, w_pool, pscale, convw, w_out, g2, w_r, b_r)


def _row_copy(src_hbm, row_start, dst_ref, r, sem):
    dst_start = pl.multiple_of(r * ROW_SUB, ROW_SUB)
    return pltpu.make_async_copy(src_hbm.at[pl.ds(pl.multiple_of(row_start, ROW_SUB), ROW_SUB)],
                                 dst_ref.at[pl.ds(dst_start, ROW_SUB)], sem)


def _gather_rows_issued(n_rows):
    return ((n_rows + GATHER_UNROLL - 1) >> GATHER_SHIFT) << GATHER_SHIFT


def _start_row_gather(src_hbm, idx_ref, base, n_rows, dst_ref, sem):
    def body(i, carry):
        for u in range(GATHER_UNROLL):
            r = i * GATHER_UNROLL + u
            _row_copy(src_hbm, idx_ref[base + r], dst_ref, r, sem).start(priority=u % DMA_QUEUES)
        return carry

    lax.fori_loop(0, _gather_rows_issued(n_rows) >> GATHER_SHIFT, body, 0)


def _wait_row_gather(src_hbm, dst_ref, sem, n_rows):
    n = _gather_rows_issued(n_rows) * ROW_SUB
    pltpu.make_async_copy(src_hbm.at[pl.ds(0, n)], dst_ref.at[pl.ds(0, n)], sem).wait()


def _weight_copies(w_hbm, e, buf, slot, k, sem):
    rows = w_hbm.shape[1] // WEIGHT_DMA_CHUNKS
    return [pltpu.make_async_copy(w_hbm.at[e, pl.ds(c * rows, rows)],
                                  buf.at[slot, k, pl.ds(c * rows, rows)], sem.at[slot])
            for c in range(WEIGHT_DMA_CHUNKS)]


def _start_weight(*args):
    for cp in _weight_copies(*args):
        cp.start(priority=WEIGHT_DMA_PRIORITY)


def _wait_weight(*args):
    for cp in _weight_copies(*args):
        cp.wait()


def _gate_up_kernel(blk_e_ref, first_ref, wslot_ref, nxt_e_ref, n_used_ref, n_valid_ref, row_ref,
                    h2_hbm, wg_hbm, wu_hbm, hid_ref,
                    xg, wbuf, wg_bf, wu_bf, gsem, wsem):
    b = pl.program_id(0)
    n_used = n_used_ref[0]
    e = blk_e_ref[b]
    slot = wslot_ref[b]

    @pl.when(b == 0)
    def _():
        xg[...] = jnp.zeros_like(xg)
        _start_row_gather(h2_hbm, row_ref, 0, n_valid_ref[0], xg.at[0], gsem.at[0])
        _start_weight(wg_hbm, e, wbuf, slot, 0, wsem)
        _start_weight(wu_hbm, e, wbuf, slot, 1, wsem)

    @pl.when(b < n_used)
    def _():
        cur = b % 2

        @pl.when(b + 1 < n_used)
        def _():
            _start_row_gather(h2_hbm, row_ref, (b + 1) * ROW_BLK, n_valid_ref[b + 1],
                              xg.at[1 - cur], gsem.at[1 - cur])

        @pl.when(first_ref[b] == 1)
        def _():
            _wait_weight(wg_hbm, e, wbuf, slot, 0, wsem)
            _wait_weight(wu_hbm, e, wbuf, slot, 1, wsem)
            nxt = nxt_e_ref[b]

            @pl.when(nxt != e)
            def _():
                _start_weight(wg_hbm, nxt, wbuf, 1 - slot, 0, wsem)
                _start_weight(wu_hbm, nxt, wbuf, 1 - slot, 1, wsem)

            wg_bf[...] = wbuf[slot, 0].astype(jnp.bfloat16)
            wu_bf[...] = wbuf[slot, 1].astype(jnp.bfloat16)

        _wait_row_gather(h2_hbm, xg.at[cur], gsem.at[cur], n_valid_ref[b])
        xb = _load_rows(xg.at[cur], ROW_BLK).astype(jnp.bfloat16)
        gate = _bdot(xb, wg_bf[...])
        up = _bdot(xb, wu_bf[...])
        hid_ref[...] = (gate * jax.nn.sigmoid(gate) * up).astype(jnp.bfloat16)

    @pl.when(b >= n_used)
    def _():
        hid_ref[...] = jnp.zeros_like(hid_ref)


def _gate_up(plan, h2, wg, wu):
    n_blocks = plan["blk_e"].shape[0]
    any_spec = pl.BlockSpec(memory_space=pl.ANY)
    return pl.pallas_call(
        _gate_up_kernel,
        grid_spec=pltpu.PrefetchScalarGridSpec(
            num_scalar_prefetch=7,
            grid=(n_blocks,),
            in_specs=[any_spec, any_spec, any_spec],
            out_specs=pl.BlockSpec((ROW_BLK, D_EXPERT), lambda b, *_: (b, 0)),
            scratch_shapes=[
                pltpu.VMEM((2, ROW_BLK * ROW_SUB, LANES), jnp.float32),
                pltpu.VMEM((2, 2, D_MODEL, D_EXPERT), jnp.float32),
                pltpu.VMEM((D_MODEL, D_EXPERT), jnp.bfloat16),
                pltpu.VMEM((D_MODEL, D_EXPERT), jnp.bfloat16),
                pltpu.SemaphoreType.DMA((2,)),
                pltpu.SemaphoreType.DMA((2,)),
            ],
        ),
        out_shape=jax.ShapeDtypeStruct((n_blocks * ROW_BLK, D_EXPERT), jnp.bfloat16),
        compiler_params=pltpu.CompilerParams(
            dimension_semantics=("arbitrary",), vmem_limit_bytes=VMEM_LIMIT),
        name="gate_up",
    )(plan["blk_e"], plan["first"], plan["wslot"], plan["nxt_e"], plan["n_used"], plan["n_valid"],
      plan["row_src"], h2, wg, wu)


def _down_kernel(blk_e_ref, first_ref, wslot_ref, nxt_e_ref, n_used_ref,
                 hid_ref, wd_hbm, y_ref, wbuf, wd_bf, wsem):
    b = pl.program_id(0)
    n_used = n_used_ref[0]
    e = blk_e_ref[b]
    slot = wslot_ref[b]

    @pl.when(b == 0)
    def _():
        _start_weight(wd_hbm, e, wbuf, slot, 0, wsem)

    @pl.when(b < n_used)
    def _():
        @pl.when(first_ref[b] == 1)
        def _():
            _wait_weight(wd_hbm, e, wbuf, slot, 0, wsem)
            nxt = nxt_e_ref[b]

            @pl.when(nxt != e)
            def _():
                _start_weight(wd_hbm, nxt, wbuf, 1 - slot, 0, wsem)

            wd_bf[...] = wbuf[slot, 0].astype(jnp.bfloat16)

        _store_rows(y_ref, _bdot(hid_ref[...], wd_bf[...]))

    @pl.when(b >= n_used)
    def _():
        y_ref[...] = jnp.zeros_like(y_ref)


def _down(plan, hid, wd):
    n_blocks = plan["blk_e"].shape[0]
    return pl.pallas_call(
        _down_kernel,
        grid_spec=pltpu.PrefetchScalarGridSpec(
            num_scalar_prefetch=5,
            grid=(n_blocks,),
            in_specs=[pl.BlockSpec((ROW_BLK, D_EXPERT), lambda b, *_: (b, 0)),
                      pl.BlockSpec(memory_space=pl.ANY)],
            out_specs=pl.BlockSpec((ROW_BLK * ROW_SUB, LANES), lambda b, *_: (b, 0)),
            scratch_shapes=[
                pltpu.VMEM((2, 1, D_EXPERT, D_MODEL), jnp.float32),
                pltpu.VMEM((D_EXPERT, D_MODEL), jnp.bfloat16),
                pltpu.SemaphoreType.DMA((2,)),
            ],
        ),
        out_shape=jax.ShapeDtypeStruct((n_blocks * ROW_BLK * ROW_SUB, LANES), jnp.float32),
        compiler_params=pltpu.CompilerParams(
            dimension_semantics=("arbitrary",), vmem_limit_bytes=VMEM_LIMIT),
        name="down",
    )(plan["blk_e"], plan["first"], plan["wslot"], plan["nxt_e"], plan["n_used"], hid, wd)


def _combine_kernel(d0_ref, d1_ref, x1_ref, route_ref, g_ref, y_hbm, o_ref, y0_buf, y1_buf, sem):
    i = pl.program_id(0)
    n = pl.num_programs(0)
    cur = i % 2

    def start(step, slot):
        _start_row_gather(y_hbm, d0_ref, step * TM, TM, y0_buf.at[slot], sem.at[0, slot])
        _start_row_gather(y_hbm, d1_ref, step * TM, TM, y1_buf.at[slot], sem.at[1, slot])

    @pl.when(i == 0)
    def _():
        start(0, 0)

    @pl.when(i + 1 < n)
    def _():
        start(i + 1, 1 - cur)

    r = route_ref[...]
    _wait_row_gather(y_hbm, y0_buf.at[cur], sem.at[0, cur], TM)
    _wait_row_gather(y_hbm, y1_buf.at[cur], sem.at[1, cur], TM)
    y0 = _load_rows(y0_buf.at[cur], TM)
    y1 = _load_rows(y1_buf.at[cur], TM)
    xo = x1_ref[...] + (y0 * r[:, 0:1] + y1 * r[:, 1:2])
    o_ref[...] = _rms(xo, g_ref[...])


def _combine(d0, d1, x1, route, g, ybuf):
    t = x1.shape[0]
    tile = lambda w: pl.BlockSpec((TM, w), lambda i, *_: (i, 0))
    return pl.pallas_call(
        _combine_kernel,
        grid_spec=pltpu.PrefetchScalarGridSpec(
            num_scalar_prefetch=2,
            grid=(t // TM,),
            in_specs=[tile(D_MODEL), tile(LANES),
                      pl.BlockSpec((1, D_MODEL), lambda i, *_: (0, 0)),
                      pl.BlockSpec(memory_space=pl.ANY)],
            out_specs=tile(D_MODEL),
            scratch_shapes=[pltpu.VMEM((2, TM * ROW_SUB, LANES), jnp.float32),
                            pltpu.VMEM((2, TM * ROW_SUB, LANES), jnp.float32),
                            pltpu.SemaphoreType.DMA((2, 2))],
        ),
        out_shape=jax.ShapeDtypeStruct((t, D_MODEL), jnp.float32),
        compiler_params=pltpu.CompilerParams(
            dimension_semantics=("arbitrary",), vmem_limit_bytes=VMEM_LIMIT),
        name="combine",
    )(d0, d1, x1, route, g, ybuf)


def _dispatch_plan(expert_id):
    t = expert_id.shape[0]
    a = t * TOP_K
    i32 = jnp.int32
    e_flat = expert_id.reshape(a)
    experts = jnp.arange(N_EXPERTS, dtype=i32)
    onehot = (e_flat[:, None] == experts[None, :]).astype(i32)
    rank = jnp.cumsum(onehot, axis=0) - onehot
    pos = jnp.sum(rank * onehot, axis=1)
    counts = jnp.sum(onehot, axis=0)
    padded = ((counts + ROW_BLK - 1) // ROW_BLK) * ROW_BLK
    pends = jnp.cumsum(padded)
    pstarts = pends - padded
    dest = (pstarts[e_flat] + pos).astype(i32)
    n_blocks = (a + N_EXPERTS * (ROW_BLK - 1) + ROW_BLK - 1) // ROW_BLK
    blk_start = jnp.arange(n_blocks, dtype=i32) * ROW_BLK
    blk_e = jnp.clip(jnp.sum(pends[None, :] <= blk_start[:, None], axis=1), 0, N_EXPERTS - 1).astype(i32)
    first = jnp.concatenate([jnp.ones((1,), i32), (blk_e[1:] != blk_e[:-1]).astype(i32)])
    wslot = (jnp.cumsum(first) - 1) % 2
    later_used = (experts[None, :] > experts[:, None]) & (counts[None, :] > 0)
    nxt = jnp.min(jnp.where(later_used, experts[None, :], N_EXPERTS), axis=1)
    nxt = jnp.where(nxt == N_EXPERTS, experts, nxt)
    tok = jnp.repeat(jnp.arange(t, dtype=i32), TOP_K)
    row_tok = jnp.zeros((n_blocks * ROW_BLK,), i32).at[dest].set(tok)
    return {
        "dest": dest,
        "blk_e": blk_e,
        "first": first,
        "wslot": wslot.astype(i32),
        "nxt_e": nxt[blk_e].astype(i32),
        "n_used": (pends[-1] // ROW_BLK).astype(i32).reshape(1),
        "n_valid": jnp.clip(counts[blk_e] - (blk_start - pstarts[blk_e]), 0, ROW_BLK).astype(i32),
        "row_src": row_tok * ROW_SUB,
    }


def kernel(x, norm_mix_g, w_in, w_pool, pool_scale, conv_w, w_out, norm_ffn_g, w_router_group, b_router_group, w_router_expert, b_router_expert, w_gate, w_up, w_down, norm_final_g):
    bt, s, d = x.shape
    t = bt * s
    bf = jnp.bfloat16
    xt = x.reshape(t, d)
    assert norm_mix_g.shape[0] == 1, "the final norm is fused into the last layer's combine"
    for l in range(norm_mix_g.shape[0]):
        w_r = jnp.concatenate([w_router_group[l], w_router_expert[l]], axis=1)
        w_r = jnp.pad(w_r, ((0, 0), (0, LANES - w_r.shape[1]))).astype(bf)
        b_r = jnp.concatenate([b_router_group[l], b_router_expert[l]])
        b_r = jnp.pad(b_r, (0, LANES - b_r.shape[0])).reshape(1, LANES)
        x1, h2, route = _mixer(
            xt, norm_mix_g[l].reshape(1, d), w_in[l].astype(bf), w_pool[l].astype(bf),
            pool_scale[l].reshape(1, MIX_A), conv_w[l].T, w_out[l].astype(bf),
            norm_ffn_g[l].reshape(1, d), w_r, b_r)
        plan = _dispatch_plan(route[:, 2:4].astype(jnp.int32))
        hid = _gate_up(plan, h2, w_gate[l], w_up[l])
        ybuf = _down(plan, hid, w_down[l])
        dest2 = plan["dest"].reshape(t, TOP_K) * ROW_SUB
        xt = _combine(dest2[:, 0], dest2[:, 1], x1, route, norm_final_g.reshape(1, d), ybuf)
    return xt.reshape(bt, s, d)
```

```python
import jax
import jax.numpy as jnp
from jax import lax
from jax.experimental import pallas as pl
from jax.experimental.pallas import tpu as pltpu

D_MODEL = 2048
MIX_A = 1024
MIX_B = 1024
POOL_WINDOWS = (2, 4, 8, 16)
POOL_CH = MIX_A // len(POOL_WINDOWS)
CONV_W = 3
N_GROUPS = 4
E_PER_GROUP = 8
N_EXPERTS = N_GROUPS * E_PER_GROUP
TOP_K = 2
D_EXPERT = D_MODEL // 2
EPS = 1e-6

LANES = 128
ROW_SUB = D_MODEL // LANES
HIST = 16
TM = 256
ROW_BLK = 256
VMEM_LIMIT = 56 * 1024 * 1024
WEIGHT_DMA_PRIORITY = 1
WEIGHT_DMA_CHUNKS = 8
DMA_QUEUES = 2
GATHER_SHIFT = 2
GATHER_UNROLL = 1 << GATHER_SHIFT

_NEG = -1e30


def _rms(x, g):
    return x * lax.rsqrt(jnp.mean(x * x, axis=-1, keepdims=True) + EPS) * g


def _bdot(a, b):
    return jnp.dot(a, b, preferred_element_type=jnp.float32)


def _store_rows(ref, val):
    ref[...] = val.astype(jnp.bfloat16).reshape(val.shape[0] * ROW_SUB, LANES)


def _load_rows(ref, n):
    return ref[...].reshape(n, D_MODEL)


def _mixer_kernel(x_ref, x_next_ref, h0_ref, g1_ref, w_in_ref, w_pool_ref, pscale_ref, convw_ref, w_out_ref,
                  g2_ref, w_r_ref, b_r_ref,
                  x1_ref, h2_ref, route_ref,
                  ext_u, ext_z, h_buf):
    i = pl.program_id(0)

    @pl.when(i == 0)
    def _():
        ext_u[0:HIST, :] = jnp.zeros((HIST, MIX_A), jnp.float32)
        ext_z[0:HIST, :] = jnp.zeros((HIST, MIX_B), jnp.float32)
        h_buf[...] = h0_ref[...]

    x = x_ref[...]
    h = h_buf[...]

    ext_u[HIST:HIST + TM, :] = _bdot(h, w_in_ref[:, 0:MIX_A])
    c_gate = _bdot(h, w_in_ref[:, MIX_A + MIX_B:MIX_A + 2 * MIX_B])
    v = _bdot(h, w_in_ref[:, MIX_A + 2 * MIX_B:MIX_A + 3 * MIX_B])
    z = c_gate * v
    ext_z[HIST:HIST + TM, :] = z
    b_gate = _bdot(h, w_in_ref[:, MIX_A:MIX_A + MIX_B])

    y = ext_z[HIST - 2:HIST - 2 + TM, :] * convw_ref[0:1, :]
    y = y + ext_z[HIST - 1:HIST - 1 + TM, :] * convw_ref[1:2, :]
    y = y + z * convw_ref[2:3, :]
    y_b = b_gate * y
    ext_z[0:HIST, :] = ext_z[TM:TM + HIST, :]
    out_b = _bdot(y_b.astype(jnp.bfloat16), w_out_ref[MIX_A:MIX_A + MIX_B, :])

    row = lax.broadcasted_iota(jnp.int32, (TM, 1), 0) + i * TM + 1
    y_a = []
    for gi, w in enumerate(POOL_WINDOWS):
        c0 = gi * POOL_CH
        u = ext_u[HIST:HIST + TM, c0:c0 + POOL_CH]
        acc = u
        for s in range(1, w):
            acc = acc + ext_u[HIST - s:HIST - s + TM, c0:c0 + POOL_CH]
        cnt = jnp.minimum(row, w).astype(jnp.float32)
        pooled = (acc / cnt - u).astype(jnp.bfloat16)
        y_a.append(_bdot(pooled, w_pool_ref[gi]))
    y_a = jnp.concatenate(y_a, axis=-1) * pscale_ref[...]
    ext_u[0:HIST, :] = ext_u[TM:TM + HIST, :]
    out_a = _bdot(y_a.astype(jnp.bfloat16), w_out_ref[0:MIX_A, :])

    x1 = x + (out_a + out_b)
    x1_ref[...] = x1

    h2_f32 = _rms(x1, g2_ref[...])
    _store_rows(h2_ref, h2_f32)
    h2 = h2_f32.astype(jnp.bfloat16)

    logits = _bdot(h2, w_r_ref[...]) + b_r_ref[...]
    lane = lax.broadcasted_iota(jnp.int32, (TM, LANES), 1)
    lane_f = lane.astype(jnp.float32)

    def first_argmax(vals, vmax):
        return jnp.min(jnp.where(vals == vmax, lane_f, float(LANES)), axis=-1, keepdims=True)

    gl = jnp.where(lane < N_GROUPS, logits, _NEG)
    gmax = jnp.max(gl, axis=-1, keepdims=True)
    g_w = 1.0 / jnp.sum(jnp.exp(gl - gmax), axis=-1, keepdims=True)
    grp = first_argmax(gl, gmax).astype(jnp.int32)
    lo = N_GROUPS + E_PER_GROUP * grp
    el = jnp.where((lane >= lo) & (lane < lo + E_PER_GROUP), logits, _NEG)
    emax = jnp.max(el, axis=-1, keepdims=True)
    idx1 = first_argmax(el, emax)
    esum = jnp.sum(jnp.exp(el - emax), axis=-1, keepdims=True)
    el2 = jnp.where(lane_f == idx1, _NEG, el)
    e2max = jnp.max(el2, axis=-1, keepdims=True)
    idx2 = first_argmax(el2, e2max)
    p1 = 1.0 / esum
    p2 = jnp.exp(e2max - emax) / esum
    tot = p1 + p2
    w1 = g_w * (p1 / tot)
    w2 = g_w * (p2 / tot)
    route = jnp.where(lane == 0, w1, 0.0)
    route = jnp.where(lane == 1, w2, route)
    route = jnp.where(lane == 2, idx1 - N_GROUPS, route)
    route = jnp.where(lane == 3, idx2 - N_GROUPS, route)
    route_ref[...] = route

    h_buf[...] = _rms(x_next_ref[...], g1_ref[...]).astype(jnp.bfloat16)


def _first_tile_norm_kernel(x_ref, g_ref, o_ref):
    o_ref[...] = _rms(x_ref[...], g_ref[...]).astype(jnp.bfloat16)


def _first_tile_norm(x, g):
    return pl.pallas_call(
        _first_tile_norm_kernel,
        grid=(1,),
        in_specs=[pl.BlockSpec((TM, D_MODEL), lambda i: (0, 0)),
                  pl.BlockSpec((1, D_MODEL), lambda i: (0, 0))],
        out_specs=pl.BlockSpec((TM, D_MODEL), lambda i: (0, 0)),
        out_shape=jax.ShapeDtypeStruct((TM, D_MODEL), jnp.bfloat16),
        name="first_tile_norm",
    )(x, g)


def _mixer(x, g1, w_in, w_pool, pscale, convw, w_out, g2, w_r, b_r):
    t = x.shape[0]
    h0 = _first_tile_norm(x, g1)
    const = lambda shape: pl.BlockSpec(shape, lambda i: (0,) * len(shape),
                                       pipeline_mode=pl.Buffered(1))
    return pl.pallas_call(
        _mixer_kernel,
        grid=(t // TM,),
        in_specs=[
            pl.BlockSpec((TM, D_MODEL), lambda i: (i, 0)),
            pl.BlockSpec((TM, D_MODEL), lambda i: (jnp.minimum(i + 1, t // TM - 1), 0)),
            const((TM, D_MODEL)),
            const((1, D_MODEL)),
            const(w_in.shape),
            const(w_pool.shape),
            const((1, MIX_A)),
            const((CONV_W, MIX_B)),
            const(w_out.shape),
            const((1, D_MODEL)),
            const(w_r.shape),
            const((1, LANES)),
        ],
        out_specs=[
            pl.BlockSpec((TM, D_MODEL), lambda i: (i, 0)),
            pl.BlockSpec((TM * ROW_SUB, LANES), lambda i: (i, 0)),
            pl.BlockSpec((TM, LANES), lambda i: (i, 0)),
        ],
        out_shape=[
            jax.ShapeDtypeStruct((t, D_MODEL), jnp.float32),
            jax.ShapeDtypeStruct((t * ROW_SUB, LANES), jnp.bfloat16),
            jax.ShapeDtypeStruct((t, LANES), jnp.float32),
        ],
        scratch_shapes=[
            pltpu.VMEM((TM + HIST, MIX_A), jnp.float32),
            pltpu.VMEM((TM + HIST, MIX_B), jnp.float32),
            pltpu.VMEM((TM, D_MODEL), jnp.bfloat16),
        ],
        compiler_params=pltpu.CompilerParams(
            dimension_semantics=("arbitrary",), vmem_limit_bytes=VMEM_LIMIT),
        name="mixer",
    )(x, x, h0, g1, w_in, w_pool, pscale, convw, w_out, g2, w_r, b_r)


def _row_copy(src_hbm, row_start, dst_ref, r, sem):
    dst_start = r * ROW_SUB if isinstance(r, int) else pl.multiple_of(r * ROW_SUB, ROW_SUB)
    return pltpu.make_async_copy(src_hbm.at[pl.ds(pl.multiple_of(row_start, ROW_SUB), ROW_SUB)],
                                 dst_ref.at[pl.ds(dst_start, ROW_SUB)], sem)


def _gather_rows_issued(n_rows):
    if isinstance(n_rows, int):
        return n_rows
    return ((n_rows + GATHER_UNROLL - 1) >> GATHER_SHIFT) << GATHER_SHIFT


def _start_row_gather(src_hbm, idx_ref, base, n_rows, dst_ref, sem):
    if isinstance(n_rows, int):
        for r in range(n_rows):
            _row_copy(src_hbm, idx_ref[base + r], dst_ref, r, sem).start(priority=r % DMA_QUEUES)
        return

    def body(i, carry):
        for u in range(GATHER_UNROLL):
            r = i * GATHER_UNROLL + u
            _row_copy(src_hbm, idx_ref[base + r], dst_ref, r, sem).start(priority=u % DMA_QUEUES)
        return carry

    lax.fori_loop(0, _gather_rows_issued(n_rows) >> GATHER_SHIFT, body, 0)


def _wait_row_gather(src_hbm, dst_ref, sem, n_rows):
    n = _gather_rows_issued(n_rows) * ROW_SUB
    pltpu.make_async_copy(src_hbm.at[pl.ds(0, n)], dst_ref.at[pl.ds(0, n)], sem).wait()


def _weight_copies(w_hbm, e, buf, slot, k, sem):
    rows = w_hbm.shape[1] // WEIGHT_DMA_CHUNKS
    return [pltpu.make_async_copy(w_hbm.at[e, pl.ds(c * rows, rows)],
                                  buf.at[slot, k, pl.ds(c * rows, rows)], sem.at[slot])
            for c in range(WEIGHT_DMA_CHUNKS)]


def _start_weight(*args):
    for cp in _weight_copies(*args):
        cp.start(priority=WEIGHT_DMA_PRIORITY)


def _wait_weight(*args):
    for cp in _weight_copies(*args):
        cp.wait()


def _gate_up_kernel(blk_e_ref, first_ref, wslot_ref, nxt_e_ref, n_used_ref, n_valid_ref, row_ref,
                    h2_hbm, wg_hbm, wu_hbm, hid_ref,
                    xg, wbuf, wg_bf, wu_bf, gsem, wsem):
    b = pl.program_id(0)
    n_used = n_used_ref[0]
    e = blk_e_ref[b]
    slot = wslot_ref[b]

    @pl.when(b == 0)
    def _():
        xg[...] = jnp.zeros_like(xg)
        _start_row_gather(h2_hbm, row_ref, 0, n_valid_ref[0], xg.at[0], gsem.at[0])
        _start_weight(wg_hbm, e, wbuf, slot, 0, wsem)
        _start_weight(wu_hbm, e, wbuf, slot, 1, wsem)

    @pl.when(b < n_used)
    def _():
        cur = b % 2

        @pl.when(b + 1 < n_used)
        def _():
            _start_row_gather(h2_hbm, row_ref, (b + 1) * ROW_BLK, n_valid_ref[b + 1],
                              xg.at[1 - cur], gsem.at[1 - cur])

        @pl.when(first_ref[b] == 1)
        def _():
            _wait_weight(wg_hbm, e, wbuf, slot, 0, wsem)
            _wait_weight(wu_hbm, e, wbuf, slot, 1, wsem)
            nxt = nxt_e_ref[b]

            @pl.when(nxt != e)
            def _():
                _start_weight(wg_hbm, nxt, wbuf, 1 - slot, 0, wsem)
                _start_weight(wu_hbm, nxt, wbuf, 1 - slot, 1, wsem)

            wg_bf[...] = wbuf[slot, 0].astype(jnp.bfloat16)
            wu_bf[...] = wbuf[slot, 1].astype(jnp.bfloat16)

        _wait_row_gather(h2_hbm, xg.at[cur], gsem.at[cur], n_valid_ref[b])
        xb = _load_rows(xg.at[cur], ROW_BLK)
        gate = _bdot(xb, wg_bf[...])
        up = _bdot(xb, wu_bf[...])
        hid_ref[...] = (gate * jax.nn.sigmoid(gate) * up).astype(jnp.bfloat16)

    @pl.when(b >= n_used)
    def _():
        hid_ref[...] = jnp.zeros_like(hid_ref)


def _gate_up(plan, h2, wg, wu):
    n_blocks = plan["blk_e"].shape[0]
    any_spec = pl.BlockSpec(memory_space=pl.ANY)
    return pl.pallas_call(
        _gate_up_kernel,
        grid_spec=pltpu.PrefetchScalarGridSpec(
            num_scalar_prefetch=7,
            grid=(n_blocks,),
            in_specs=[any_spec, any_spec, any_spec],
            out_specs=pl.BlockSpec((ROW_BLK, D_EXPERT), lambda b, *_: (b, 0)),
            scratch_shapes=[
                pltpu.VMEM((2, ROW_BLK * ROW_SUB, LANES), jnp.bfloat16),
                pltpu.VMEM((2, 2, D_MODEL, D_EXPERT), jnp.float32),
                pltpu.VMEM((D_MODEL, D_EXPERT), jnp.bfloat16),
                pltpu.VMEM((D_MODEL, D_EXPERT), jnp.bfloat16),
                pltpu.SemaphoreType.DMA((2,)),
                pltpu.SemaphoreType.DMA((2,)),
            ],
        ),
        out_shape=jax.ShapeDtypeStruct((n_blocks * ROW_BLK, D_EXPERT), jnp.bfloat16),
        compiler_params=pltpu.CompilerParams(
            dimension_semantics=("arbitrary",), vmem_limit_bytes=VMEM_LIMIT),
        name="gate_up",
    )(plan["blk_e"], plan["first"], plan["wslot"], plan["nxt_e"], plan["n_used"], plan["n_valid"],
      plan["row_src"], h2, wg, wu)


def _down_kernel(blk_e_ref, first_ref, wslot_ref, nxt_e_ref, n_used_ref,
                 hid_ref, wd_hbm, y_ref, wbuf, wd_bf, wsem):
    b = pl.program_id(0)
    n_used = n_used_ref[0]
    e = blk_e_ref[b]
    slot = wslot_ref[b]

    @pl.when(b == 0)
    def _():
        _start_weight(wd_hbm, e, wbuf, slot, 0, wsem)

    @pl.when(b < n_used)
    def _():
        @pl.when(first_ref[b] == 1)
        def _():
            _wait_weight(wd_hbm, e, wbuf, slot, 0, wsem)
            nxt = nxt_e_ref[b]

            @pl.when(nxt != e)
            def _():
                _start_weight(wd_hbm, nxt, wbuf, 1 - slot, 0, wsem)

            wd_bf[...] = wbuf[slot, 0].astype(jnp.bfloat16)

        _store_rows(y_ref, _bdot(hid_ref[...], wd_bf[...]))

    @pl.when(b >= n_used)
    def _():
        y_ref[...] = jnp.zeros_like(y_ref)


def _down(plan, hid, wd):
    n_blocks = plan["blk_e"].shape[0]
    return pl.pallas_call(
        _down_kernel,
        grid_spec=pltpu.PrefetchScalarGridSpec(
            num_scalar_prefetch=5,
            grid=(n_blocks,),
            in_specs=[pl.BlockSpec((ROW_BLK, D_EXPERT), lambda b, *_: (b, 0)),
                      pl.BlockSpec(memory_space=pl.ANY)],
            out_specs=pl.BlockSpec((ROW_BLK * ROW_SUB, LANES), lambda b, *_: (b, 0)),
            scratch_shapes=[
                pltpu.VMEM((2, 1, D_EXPERT, D_MODEL), jnp.float32),
                pltpu.VMEM((D_EXPERT, D_MODEL), jnp.bfloat16),
                pltpu.SemaphoreType.DMA((2,)),
            ],
        ),
        out_shape=jax.ShapeDtypeStruct((n_blocks * ROW_BLK * ROW_SUB, LANES), jnp.bfloat16),
        compiler_params=pltpu.CompilerParams(
            dimension_semantics=("arbitrary",), vmem_limit_bytes=VMEM_LIMIT),
        name="down",
    )(plan["blk_e"], plan["first"], plan["wslot"], plan["nxt_e"], plan["n_used"], hid, wd)


def _combine_kernel(d0_ref, d1_ref, x1_ref, route_ref, g_ref, y_hbm, o_ref, y0_buf, y1_buf, sem):
    i = pl.program_id(0)
    n = pl.num_programs(0)
    cur = i % 2

    def start(step, slot):
        _start_row_gather(y_hbm, d0_ref, step * TM, TM, y0_buf.at[slot], sem.at[0, slot])
        _start_row_gather(y_hbm, d1_ref, step * TM, TM, y1_buf.at[slot], sem.at[1, slot])

    @pl.when(i == 0)
    def _():
        start(0, 0)

    @pl.when(i + 1 < n)
    def _():
        start(i + 1, 1 - cur)

    r = route_ref[...]
    _wait_row_gather(y_hbm, y0_buf.at[cur], sem.at[0, cur], TM)
    _wait_row_gather(y_hbm, y1_buf.at[cur], sem.at[1, cur], TM)
    y0 = _load_rows(y0_buf.at[cur], TM).astype(jnp.float32)
    y1 = _load_rows(y1_buf.at[cur], TM).astype(jnp.float32)
    xo = x1_ref[...] + (y0 * r[:, 0:1] + y1 * r[:, 1:2])
    o_ref[...] = _rms(xo, g_ref[...])


def _combine(d0, d1, x1, route, g, ybuf):
    t = x1.shape[0]
    tile = lambda w: pl.BlockSpec((TM, w), lambda i, *_: (i, 0))
    return pl.pallas_call(
        _combine_kernel,
        grid_spec=pltpu.PrefetchScalarGridSpec(
            num_scalar_prefetch=2,
            grid=(t // TM,),
            in_specs=[tile(D_MODEL), tile(LANES),
                      pl.BlockSpec((1, D_MODEL), lambda i, *_: (0, 0)),
                      pl.BlockSpec(memory_space=pl.ANY)],
            out_specs=tile(D_MODEL),
            scratch_shapes=[pltpu.VMEM((2, TM * ROW_SUB, LANES), jnp.bfloat16),
                            pltpu.VMEM((2, TM * ROW_SUB, LANES), jnp.bfloat16),
                            pltpu.SemaphoreType.DMA((2, 2))],
        ),
        out_shape=jax.ShapeDtypeStruct((t, D_MODEL), jnp.float32),
        compiler_params=pltpu.CompilerParams(
            dimension_semantics=("arbitrary",), vmem_limit_bytes=VMEM_LIMIT),
        name="combine",
    )(d0, d1, x1, route, g, ybuf)


def _dispatch_plan(expert_id):
    t = expert_id.shape[0]
    a = t * TOP_K
    i32 = jnp.int32
    e_flat = expert_id.reshape(a)
    experts = jnp.arange(N_EXPERTS, dtype=i32)
    onehot = (e_flat[:, None] == experts[None, :]).astype(i32)
    rank = jnp.cumsum(onehot, axis=0) - onehot
    pos = jnp.sum(rank * onehot, axis=1)
    counts = jnp.sum(onehot, axis=0)
    padded = ((counts + ROW_BLK - 1) // ROW_BLK) * ROW_BLK
    pends = jnp.cumsum(padded)
    pstarts = pends - padded
    dest = (pstarts[e_flat] + pos).astype(i32)
    n_blocks = (a + N_EXPERTS * (ROW_BLK - 1) + ROW_BLK - 1) // ROW_BLK
    blk_start = jnp.arange(n_blocks, dtype=i32) * ROW_BLK
    blk_e = jnp.clip(jnp.sum(pends[None, :] <= blk_start[:, None], axis=1), 0, N_EXPERTS - 1).astype(i32)
    first = jnp.concatenate([jnp.ones((1,), i32), (blk_e[1:] != blk_e[:-1]).astype(i32)])
    wslot = (jnp.cumsum(first) - 1) % 2
    later_used = (experts[None, :] > experts[:, None]) & (counts[None, :] > 0)
    nxt = jnp.min(jnp.where(later_used, experts[None, :], N_EXPERTS), axis=1)
    nxt = jnp.where(nxt == N_EXPERTS, experts, nxt)
    tok = jnp.repeat(jnp.arange(t, dtype=i32), TOP_K)
    row_tok = jnp.zeros((n_blocks * ROW_BLK,), i32).at[dest].set(tok)
    return {
        "dest": dest,
        "blk_e": blk_e,
        "first": first,
        "wslot": wslot.astype(i32),
        "nxt_e": nxt[blk_e].astype(i32),
        "n_used": (pends[-1] // ROW_BLK).astype(i32).reshape(1),
        "n_valid": jnp.clip(counts[blk_e] - (blk_start - pstarts[blk_e]), 0, ROW_BLK).astype(i32),
        "row_src": row_tok * ROW_SUB,
    }


def kernel(x, norm_mix_g, w_in, w_pool, pool_scale, conv_w, w_out, norm_ffn_g, w_router_group, b_router_group, w_router_expert, b_router_expert, w_gate, w_up, w_down, norm_final_g):
    bt, s, d = x.shape
    t = bt * s
    bf = jnp.bfloat16
    xt = x.reshape(t, d)
    assert norm_mix_g.shape[0] == 1, "the final norm is fused into the last layer's combine"
    for l in range(norm_mix_g.shape[0]):
        w_r = jnp.concatenate([w_router_group[l], w_router_expert[l]], axis=1)
        w_r = jnp.pad(w_r, ((0, 0), (0, LANES - w_r.shape[1]))).astype(bf)
        b_r = jnp.concatenate([b_router_group[l], b_router_expert[l]])
        b_r = jnp.pad(b_r, (0, LANES - b_r.shape[0])).reshape(1, LANES)
        x1, h2, route = _mixer(
            xt, norm_mix_g[l].reshape(1, d), w_in[l].astype(bf), w_pool[l].astype(bf),
            pool_scale[l].reshape(1, MIX_A), conv_w[l].T, w_out[l].astype(bf),
            norm_ffn_g[l].reshape(1, d), w_r, b_r)
        plan = _dispatch_plan(route[:, 2:4].astype(jnp.int32))
        hid = _gate_up(plan, h2, w_gate[l], w_up[l])
        ybuf = _down(plan, hid, w_down[l])
        dest2 = plan["dest"].reshape(t, TOP_K) * ROW_SUB
        xt = _combine(dest2[:, 0], dest2[:, 1], x1, route, norm_final_g.reshape(1, d), ybuf)
    return xt.reshape(bt, s, d)
```

```python
import jax
import jax.numpy as jnp
from jax import lax
from jax.experimental import pallas as pl
from jax.experimental.pallas import tpu as pltpu

D_MODEL = 2048
MIX_A = 1024
MIX_B = 1024
POOL_WINDOWS = (2, 4, 8, 16)
POOL_CH = MIX_A // len(POOL_WINDOWS)
CONV_W = 3
N_GROUPS = 4
E_PER_GROUP = 8
N_EXPERTS = N_GROUPS * E_PER_GROUP
TOP_K = 2
D_EXPERT = D_MODEL // 2
EPS = 1e-6

LANES = 128
ROW_SUB = D_MODEL // LANES
HIST = 16
TM = 256
ROW_BLK = 256
VMEM_LIMIT = 56 * 1024 * 1024
WEIGHT_DMA_PRIORITY = 1
WEIGHT_DMA_CHUNKS = 8
DMA_QUEUES = 2
GATHER_SHIFT = 2
GATHER_UNROLL = 1 << GATHER_SHIFT
SCALAR_UNROLL = 8
TOP_K_SHIFT = 1
assert 1 << TOP_K_SHIFT == TOP_K

_NEG = -1e30


def _rms(x, g):
    return x * lax.rsqrt(jnp.mean(x * x, axis=-1, keepdims=True) + EPS) * g


def _bdot(a, b):
    return jnp.dot(a, b, preferred_element_type=jnp.float32)


def _store_rows(ref, val):
    ref[...] = val.astype(jnp.bfloat16).reshape(val.shape[0] * ROW_SUB, LANES)


def _load_rows(ref, n):
    return ref[...].reshape(n, D_MODEL)


def _mixer_kernel(x_ref, x_next_ref, h0_ref, g1_ref, w_in_ref, w_pool_ref, pscale_ref, convw_ref, w_out_ref,
                  g2_ref, w_r_ref, b_r_ref,
                  x1_ref, h2_ref, route_ref,
                  ext_u, ext_z, h_buf):
    i = pl.program_id(0)

    @pl.when(i == 0)
    def _():
        ext_u[0:HIST, :] = jnp.zeros((HIST, MIX_A), jnp.float32)
        ext_z[0:HIST, :] = jnp.zeros((HIST, MIX_B), jnp.float32)
        h_buf[...] = h0_ref[...]

    x = x_ref[...]
    h = h_buf[...]

    ext_u[HIST:HIST + TM, :] = _bdot(h, w_in_ref[:, 0:MIX_A])
    c_gate = _bdot(h, w_in_ref[:, MIX_A + MIX_B:MIX_A + 2 * MIX_B])
    v = _bdot(h, w_in_ref[:, MIX_A + 2 * MIX_B:MIX_A + 3 * MIX_B])
    z = c_gate * v
    ext_z[HIST:HIST + TM, :] = z
    b_gate = _bdot(h, w_in_ref[:, MIX_A:MIX_A + MIX_B])

    y = ext_z[HIST - 2:HIST - 2 + TM, :] * convw_ref[0:1, :]
    y = y + ext_z[HIST - 1:HIST - 1 + TM, :] * convw_ref[1:2, :]
    y = y + z * convw_ref[2:3, :]
    y_b = b_gate * y
    ext_z[0:HIST, :] = ext_z[TM:TM + HIST, :]
    out_b = _bdot(y_b.astype(jnp.bfloat16), w_out_ref[MIX_A:MIX_A + MIX_B, :])

    row = lax.broadcasted_iota(jnp.int32, (TM, 1), 0) + i * TM + 1
    y_a = []
    for gi, w in enumerate(POOL_WINDOWS):
        c0 = gi * POOL_CH
        u = ext_u[HIST:HIST + TM, c0:c0 + POOL_CH]
        acc = u
        for s in range(1, w):
            acc = acc + ext_u[HIST - s:HIST - s + TM, c0:c0 + POOL_CH]
        cnt = jnp.minimum(row, w).astype(jnp.float32)
        pooled = (acc / cnt - u).astype(jnp.bfloat16)
        y_a.append(_bdot(pooled, w_pool_ref[gi]))
    y_a = jnp.concatenate(y_a, axis=-1) * pscale_ref[...]
    ext_u[0:HIST, :] = ext_u[TM:TM + HIST, :]
    out_a = _bdot(y_a.astype(jnp.bfloat16), w_out_ref[0:MIX_A, :])

    x1 = x + (out_a + out_b)
    x1_ref[...] = x1

    h2_f32 = _rms(x1, g2_ref[...])
    _store_rows(h2_ref, h2_f32)
    h2 = h2_f32.astype(jnp.bfloat16)

    logits = _bdot(h2, w_r_ref[...]) + b_r_ref[...]
    lane = lax.broadcasted_iota(jnp.int32, (TM, LANES), 1)
    lane_f = lane.astype(jnp.float32)

    def first_argmax(vals, vmax):
        return jnp.min(jnp.where(vals == vmax, lane_f, float(LANES)), axis=-1, keepdims=True)

    gl = jnp.where(lane < N_GROUPS, logits, _NEG)
    gmax = jnp.max(gl, axis=-1, keepdims=True)
    g_w = 1.0 / jnp.sum(jnp.exp(gl - gmax), axis=-1, keepdims=True)
    grp = first_argmax(gl, gmax).astype(jnp.int32)
    lo = N_GROUPS + E_PER_GROUP * grp
    el = jnp.where((lane >= lo) & (lane < lo + E_PER_GROUP), logits, _NEG)
    emax = jnp.max(el, axis=-1, keepdims=True)
    idx1 = first_argmax(el, emax)
    esum = jnp.sum(jnp.exp(el - emax), axis=-1, keepdims=True)
    el2 = jnp.where(lane_f == idx1, _NEG, el)
    e2max = jnp.max(el2, axis=-1, keepdims=True)
    idx2 = first_argmax(el2, e2max)
    p1 = 1.0 / esum
    p2 = jnp.exp(e2max - emax) / esum
    tot = p1 + p2
    w1 = g_w * (p1 / tot)
    w2 = g_w * (p2 / tot)
    route = jnp.where(lane == 0, w1, 0.0)
    route = jnp.where(lane == 1, w2, route)
    route = jnp.where(lane == 2, idx1 - N_GROUPS, route)
    route = jnp.where(lane == 3, idx2 - N_GROUPS, route)
    route_ref[...] = route

    h_buf[...] = _rms(x_next_ref[...], g1_ref[...]).astype(jnp.bfloat16)


def _first_tile_norm_kernel(x_ref, g_ref, o_ref):
    o_ref[...] = _rms(x_ref[...], g_ref[...]).astype(jnp.bfloat16)


def _first_tile_norm(x, g):
    return pl.pallas_call(
        _first_tile_norm_kernel,
        grid=(1,),
        in_specs=[pl.BlockSpec((TM, D_MODEL), lambda i: (0, 0)),
                  pl.BlockSpec((1, D_MODEL), lambda i: (0, 0))],
        out_specs=pl.BlockSpec((TM, D_MODEL), lambda i: (0, 0)),
        out_shape=jax.ShapeDtypeStruct((TM, D_MODEL), jnp.bfloat16),
        name="first_tile_norm",
    )(x, g)


def _mixer(x, g1, w_in, w_pool, pscale, convw, w_out, g2, w_r, b_r):
    t = x.shape[0]
    h0 = _first_tile_norm(x, g1)
    const = lambda shape: pl.BlockSpec(shape, lambda i: (0,) * len(shape),
                                       pipeline_mode=pl.Buffered(1))
    return pl.pallas_call(
        _mixer_kernel,
        grid=(t // TM,),
        in_specs=[
            pl.BlockSpec((TM, D_MODEL), lambda i: (i, 0)),
            pl.BlockSpec((TM, D_MODEL), lambda i: (jnp.minimum(i + 1, t // TM - 1), 0)),
            const((TM, D_MODEL)),
            const((1, D_MODEL)),
            const(w_in.shape),
            const(w_pool.shape),
            const((1, MIX_A)),
            const((CONV_W, MIX_B)),
            const(w_out.shape),
            const((1, D_MODEL)),
            const(w_r.shape),
            const((1, LANES)),
        ],
        out_specs=[
            pl.BlockSpec((TM, D_MODEL), lambda i: (i, 0)),
            pl.BlockSpec((TM * ROW_SUB, LANES), lambda i: (i, 0)),
            pl.BlockSpec((TM, LANES), lambda i: (i, 0)),
        ],
        out_shape=[
            jax.ShapeDtypeStruct((t, D_MODEL), jnp.float32),
            jax.ShapeDtypeStruct((t * ROW_SUB, LANES), jnp.bfloat16),
            jax.ShapeDtypeStruct((t, LANES), jnp.float32),
        ],
        scratch_shapes=[
            pltpu.VMEM((TM + HIST, MIX_A), jnp.float32),
            pltpu.VMEM((TM + HIST, MIX_B), jnp.float32),
            pltpu.VMEM((TM, D_MODEL), jnp.bfloat16),
        ],
        compiler_params=pltpu.CompilerParams(
            dimension_semantics=("arbitrary",), vmem_limit_bytes=VMEM_LIMIT),
        name="mixer",
    )(x, x, h0, g1, w_in, w_pool, pscale, convw, w_out, g2, w_r, b_r)


def _row_copy(src_hbm, row_start, dst_ref, r, sem):
    dst_start = r * ROW_SUB if isinstance(r, int) else pl.multiple_of(r * ROW_SUB, ROW_SUB)
    return pltpu.make_async_copy(src_hbm.at[pl.ds(pl.multiple_of(row_start, ROW_SUB), ROW_SUB)],
                                 dst_ref.at[pl.ds(dst_start, ROW_SUB)], sem)


def _gather_rows_issued(n_rows):
    if isinstance(n_rows, int):
        return n_rows
    return ((n_rows + GATHER_UNROLL - 1) >> GATHER_SHIFT) << GATHER_SHIFT


def _start_row_gather(src_hbm, idx_ref, base, n_rows, dst_ref, sem):
    if isinstance(n_rows, int):
        for r in range(n_rows):
            _row_copy(src_hbm, idx_ref[base + r], dst_ref, r, sem).start(priority=r % DMA_QUEUES)
        return

    def body(i, carry):
        for u in range(GATHER_UNROLL):
            r = i * GATHER_UNROLL + u
            src_row = idx_ref[base + jnp.minimum(r, n_rows - 1)]
            _row_copy(src_hbm, src_row, dst_ref, r, sem).start(priority=u % DMA_QUEUES)
        return carry

    lax.fori_loop(0, _gather_rows_issued(n_rows) >> GATHER_SHIFT, body, 0)


def _wait_row_gather(src_hbm, dst_ref, sem, n_rows):
    n = _gather_rows_issued(n_rows) * ROW_SUB
    pltpu.make_async_copy(src_hbm.at[pl.ds(0, n)], dst_ref.at[pl.ds(0, n)], sem).wait()


def _weight_copies(w_hbm, e, buf, slot, k, sem):
    rows = w_hbm.shape[1] // WEIGHT_DMA_CHUNKS
    return [pltpu.make_async_copy(w_hbm.at[e, pl.ds(c * rows, rows)],
                                  buf.at[slot, k, pl.ds(c * rows, rows)], sem.at[slot])
            for c in range(WEIGHT_DMA_CHUNKS)]


def _start_weight(*args):
    for cp in _weight_copies(*args):
        cp.start(priority=WEIGHT_DMA_PRIORITY)


def _wait_weight(*args):
    for cp in _weight_copies(*args):
        cp.wait()


def _build_row_sources(dest_ref, row_ref):
    n_assign = dest_ref.shape[0]

    def scatter(i, carry):
        for u in range(SCALAR_UNROLL):
            a = i * SCALAR_UNROLL + u
            row_ref[dest_ref[a]] = (a >> TOP_K_SHIFT) * ROW_SUB
        return carry

    lax.fori_loop(0, n_assign // SCALAR_UNROLL, scatter, 0)


def _gate_up_kernel(blk_e_ref, first_ref, wslot_ref, nxt_e_ref, n_used_ref, n_valid_ref, dest_ref,
                    h2_hbm, wg_hbm, wu_hbm, hid_ref,
                    xg, wbuf, wg_bf, wu_bf, row_ref, gsem, wsem):
    b = pl.program_id(0)
    n_used = n_used_ref[0]
    e = blk_e_ref[b]
    slot = wslot_ref[b]

    @pl.when(b == 0)
    def _():
        _start_weight(wg_hbm, e, wbuf, slot, 0, wsem)
        _start_weight(wu_hbm, e, wbuf, slot, 1, wsem)
        xg[...] = jnp.zeros_like(xg)
        _build_row_sources(dest_ref, row_ref)
        _start_row_gather(h2_hbm, row_ref, 0, n_valid_ref[0], xg.at[0], gsem.at[0])

    @pl.when(b < n_used)
    def _():
        cur = b % 2

        @pl.when(b + 1 < n_used)
        def _():
            _start_row_gather(h2_hbm, row_ref, (b + 1) * ROW_BLK, n_valid_ref[b + 1],
                              xg.at[1 - cur], gsem.at[1 - cur])

        @pl.when(first_ref[b] == 1)
        def _():
            _wait_weight(wg_hbm, e, wbuf, slot, 0, wsem)
            _wait_weight(wu_hbm, e, wbuf, slot, 1, wsem)
            nxt = nxt_e_ref[b]

            @pl.when(nxt != e)
            def _():
                _start_weight(wg_hbm, nxt, wbuf, 1 - slot, 0, wsem)
                _start_weight(wu_hbm, nxt, wbuf, 1 - slot, 1, wsem)

            wg_bf[...] = wbuf[slot, 0].astype(jnp.bfloat16)
            wu_bf[...] = wbuf[slot, 1].astype(jnp.bfloat16)

        _wait_row_gather(h2_hbm, xg.at[cur], gsem.at[cur], n_valid_ref[b])
        xb = _load_rows(xg.at[cur], ROW_BLK)
        gate = _bdot(xb, wg_bf[...])
        up = _bdot(xb, wu_bf[...])
        hid_ref[...] = (gate * jax.nn.sigmoid(gate) * up).astype(jnp.bfloat16)

    @pl.when(b >= n_used)
    def _():
        hid_ref[...] = jnp.zeros_like(hid_ref)


def _gate_up(plan, h2, wg, wu):
    n_blocks = plan["blk_e"].shape[0]
    any_spec = pl.BlockSpec(memory_space=pl.ANY)
    return pl.pallas_call(
        _gate_up_kernel,
        grid_spec=pltpu.PrefetchScalarGridSpec(
            num_scalar_prefetch=7,
            grid=(n_blocks,),
            in_specs=[any_spec, any_spec, any_spec],
            out_specs=pl.BlockSpec((ROW_BLK, D_EXPERT), lambda b, *_: (b, 0)),
            scratch_shapes=[
                pltpu.VMEM((2, ROW_BLK * ROW_SUB, LANES), jnp.bfloat16),
                pltpu.VMEM((2, 2, D_MODEL, D_EXPERT), jnp.float32),
                pltpu.VMEM((D_MODEL, D_EXPERT), jnp.bfloat16),
                pltpu.VMEM((D_MODEL, D_EXPERT), jnp.bfloat16),
                pltpu.SMEM((n_blocks * ROW_BLK,), jnp.int32),
                pltpu.SemaphoreType.DMA((2,)),
                pltpu.SemaphoreType.DMA((2,)),
            ],
        ),
        out_shape=jax.ShapeDtypeStruct((n_blocks * ROW_BLK, D_EXPERT), jnp.bfloat16),
        compiler_params=pltpu.CompilerParams(
            dimension_semantics=("arbitrary",), vmem_limit_bytes=VMEM_LIMIT),
        name="gate_up",
    )(plan["blk_e"], plan["first"], plan["wslot"], plan["nxt_e"], plan["n_used"], plan["n_valid"],
      plan["dest"], h2, wg, wu)


def _down_kernel(blk_e_ref, first_ref, wslot_ref, nxt_e_ref, n_used_ref,
                 hid_ref, wd_hbm, y_ref, wbuf, wd_bf, wsem):
    b = pl.program_id(0)
    n_used = n_used_ref[0]
    e = blk_e_ref[b]
    slot = wslot_ref[b]

    @pl.when(b == 0)
    def _():
        _start_weight(wd_hbm, e, wbuf, slot, 0, wsem)

    @pl.when(b < n_used)
    def _():
        @pl.when(first_ref[b] == 1)
        def _():
            _wait_weight(wd_hbm, e, wbuf, slot, 0, wsem)
            nxt = nxt_e_ref[b]

            @pl.when(nxt != e)
            def _():
                _start_weight(wd_hbm, nxt, wbuf, 1 - slot, 0, wsem)

            wd_bf[...] = wbuf[slot, 0].astype(jnp.bfloat16)

        _store_rows(y_ref, _bdot(hid_ref[...], wd_bf[...]))

    @pl.when(b >= n_used)
    def _():
        y_ref[...] = jnp.zeros_like(y_ref)


def _down(plan, hid, wd):
    n_blocks = plan["blk_e"].shape[0]
    return pl.pallas_call(
        _down_kernel,
        grid_spec=pltpu.PrefetchScalarGridSpec(
            num_scalar_prefetch=5,
            grid=(n_blocks,),
            in_specs=[pl.BlockSpec((ROW_BLK, D_EXPERT), lambda b, *_: (b, 0)),
                      pl.BlockSpec(memory_space=pl.ANY)],
            out_specs=pl.BlockSpec((ROW_BLK * ROW_SUB, LANES), lambda b, *_: (b, 0)),
            scratch_shapes=[
                pltpu.VMEM((2, 1, D_EXPERT, D_MODEL), jnp.float32),
                pltpu.VMEM((D_EXPERT, D_MODEL), jnp.bfloat16),
                pltpu.SemaphoreType.DMA((2,)),
            ],
        ),
        out_shape=jax.ShapeDtypeStruct((n_blocks * ROW_BLK * ROW_SUB, LANES), jnp.bfloat16),
        compiler_params=pltpu.CompilerParams(
            dimension_semantics=("arbitrary",), vmem_limit_bytes=VMEM_LIMIT),
        name="down",
    )(plan["blk_e"], plan["first"], plan["wslot"], plan["nxt_e"], plan["n_used"], hid, wd)


def _combine_kernel(d0_ref, d1_ref, x1_ref, route_ref, g_ref, y_hbm, o_ref, y0_buf, y1_buf, sem):
    i = pl.program_id(0)
    n = pl.num_programs(0)
    cur = i % 2

    def start(step, slot):
        _start_row_gather(y_hbm, d0_ref, step * TM, TM, y0_buf.at[slot], sem.at[0, slot])
        _start_row_gather(y_hbm, d1_ref, step * TM, TM, y1_buf.at[slot], sem.at[1, slot])

    @pl.when(i == 0)
    def _():
        start(0, 0)

    @pl.when(i + 1 < n)
    def _():
        start(i + 1, 1 - cur)

    r = route_ref[...]
    _wait_row_gather(y_hbm, y0_buf.at[cur], sem.at[0, cur], TM)
    _wait_row_gather(y_hbm, y1_buf.at[cur], sem.at[1, cur], TM)
    y0 = _load_rows(y0_buf.at[cur], TM).astype(jnp.float32)
    y1 = _load_rows(y1_buf.at[cur], TM).astype(jnp.float32)
    xo = x1_ref[...] + (y0 * r[:, 0:1] + y1 * r[:, 1:2])
    o_ref[...] = _rms(xo, g_ref[...])


def _combine(d0, d1, x1, route, g, ybuf):
    t = x1.shape[0]
    tile = lambda w: pl.BlockSpec((TM, w), lambda i, *_: (i, 0))
    return pl.pallas_call(
        _combine_kernel,
        grid_spec=pltpu.PrefetchScalarGridSpec(
            num_scalar_prefetch=2,
            grid=(t // TM,),
            in_specs=[tile(D_MODEL), tile(LANES),
                      pl.BlockSpec((1, D_MODEL), lambda i, *_: (0, 0)),
                      pl.BlockSpec(memory_space=pl.ANY)],
            out_specs=tile(D_MODEL),
            scratch_shapes=[pltpu.VMEM((2, TM * ROW_SUB, LANES), jnp.bfloat16),
                            pltpu.VMEM((2, TM * ROW_SUB, LANES), jnp.bfloat16),
                            pltpu.SemaphoreType.DMA((2, 2))],
        ),
        out_shape=jax.ShapeDtypeStruct((t, D_MODEL), jnp.float32),
        compiler_params=pltpu.CompilerParams(
            dimension_semantics=("arbitrary",), vmem_limit_bytes=VMEM_LIMIT),
        name="combine",
    )(d0, d1, x1, route, g, ybuf)


def _dispatch_plan(expert_id):
    t = expert_id.shape[0]
    a = t * TOP_K
    i32 = jnp.int32
    e_flat = expert_id.reshape(a)
    experts = jnp.arange(N_EXPERTS, dtype=i32)
    onehot = (e_flat[:, None] == experts[None, :]).astype(i32)
    rank = jnp.cumsum(onehot, axis=0) - onehot
    pos = jnp.sum(rank * onehot, axis=1)
    counts = jnp.sum(onehot, axis=0)
    padded = ((counts + ROW_BLK - 1) // ROW_BLK) * ROW_BLK
    pends = jnp.cumsum(padded)
    pstarts = pends - padded
    dest = (pstarts[e_flat] + pos).astype(i32)
    n_blocks = (a + N_EXPERTS * (ROW_BLK - 1) + ROW_BLK - 1) // ROW_BLK
    blk_start = jnp.arange(n_blocks, dtype=i32) * ROW_BLK
    blk_e = jnp.clip(jnp.sum(pends[None, :] <= blk_start[:, None], axis=1), 0, N_EXPERTS - 1).astype(i32)
    first = jnp.concatenate([jnp.ones((1,), i32), (blk_e[1:] != blk_e[:-1]).astype(i32)])
    wslot = (jnp.cumsum(first) - 1) % 2
    later_used = (experts[None, :] > experts[:, None]) & (counts[None, :] > 0)
    nxt = jnp.min(jnp.where(later_used, experts[None, :], N_EXPERTS), axis=1)
    nxt = jnp.where(nxt == N_EXPERTS, experts, nxt)
    return {
        "dest": dest,
        "blk_e": blk_e,
        "first": first,
        "wslot": wslot.astype(i32),
        "nxt_e": nxt[blk_e].astype(i32),
        "n_used": (pends[-1] // ROW_BLK).astype(i32).reshape(1),
        "n_valid": jnp.clip(counts[blk_e] - (blk_start - pstarts[blk_e]), 0, ROW_BLK).astype(i32),
    }


def kernel(x, norm_mix_g, w_in, w_pool, pool_scale, conv_w, w_out, norm_ffn_g, w_router_group, b_router_group, w_router_expert, b_router_expert, w_gate, w_up, w_down, norm_final_g):
    bt, s, d = x.shape
    t = bt * s
    bf = jnp.bfloat16
    xt = x.reshape(t, d)
    assert norm_mix_g.shape[0] == 1, "the final norm is fused into the last layer's combine"
    for l in range(norm_mix_g.shape[0]):
        w_r = jnp.concatenate([w_router_group[l], w_router_expert[l]], axis=1)
        w_r = jnp.pad(w_r, ((0, 0), (0, LANES - w_r.shape[1]))).astype(bf)
        b_r = jnp.concatenate([b_router_group[l], b_router_expert[l]])
        b_r = jnp.pad(b_r, (0, LANES - b_r.shape[0])).reshape(1, LANES)
        x1, h2, route = _mixer(
            xt, norm_mix_g[l].reshape(1, d), w_in[l].astype(bf), w_pool[l].astype(bf),
            pool_scale[l].reshape(1, MIX_A), conv_w[l].T, w_out[l].astype(bf),
            norm_ffn_g[l].reshape(1, d), w_r, b_r)
        plan = _dispatch_plan(route[:, 2:4].astype(jnp.int32))
        hid = _gate_up(plan, h2, w_gate[l], w_up[l])
        ybuf = _down(plan, hid, w_down[l])
        dest2 = plan["dest"].reshape(t, TOP_K) * ROW_SUB
        xt = _combine(dest2[:, 0], dest2[:, 1], x1, route, norm_final_g.reshape(1, d), ybuf)
    return xt.reshape(bt, s, d)
```

```python
import jax
import jax.numpy as jnp
from jax import lax
from jax.experimental import pallas as pl
from jax.experimental.pallas import tpu as pltpu

D_MODEL = 2048
MIX_A = 1024
MIX_B = 1024
POOL_WINDOWS = (2, 4, 8, 16)
POOL_CH = MIX_A // len(POOL_WINDOWS)
CONV_W = 3
N_GROUPS = 4
E_PER_GROUP = 8
N_EXPERTS = N_GROUPS * E_PER_GROUP
TOP_K = 2
D_EXPERT = D_MODEL // 2
EPS = 1e-6

LANES = 128
SUBLANES = 8
ROW_SUB = D_MODEL // LANES
HIST = 16
TM = 256
ROW_BLK = 256
VMEM_LIMIT = 56 * 1024 * 1024
WEIGHT_DMA_PRIORITY = 1
WEIGHT_DMA_CHUNKS = 8
DMA_QUEUES = 2
GATHER_SHIFT = 2
GATHER_UNROLL = 1 << GATHER_SHIFT
SCALAR_UNROLL = 8
TOP_K_SHIFT = 1
assert 1 << TOP_K_SHIFT == TOP_K

_NEG = -1e30


def _rms(x, g):
    return x * lax.rsqrt(jnp.mean(x * x, axis=-1, keepdims=True) + EPS) * g


def _bdot(a, b):
    return jnp.dot(a, b, preferred_element_type=jnp.float32)


def _store_rows(ref, val):
    ref[...] = val.astype(jnp.bfloat16).reshape(val.shape[0] * ROW_SUB, LANES)


def _load_rows(ref, n):
    return ref[...].reshape(n, D_MODEL)


def _mixer_kernel(x_ref, x_next_ref, h0_ref, g1_ref, w_in_ref, w_pool_ref, pscale_ref, convw_ref, w_out_ref,
                  g2_ref, w_r_ref, b_r_ref,
                  x1_ref, h2_ref, route_ref, counts_ref,
                  ext_u, ext_z, h_buf, running):
    i = pl.program_id(0)

    @pl.when(i == 0)
    def _():
        ext_u[0:HIST, :] = jnp.zeros((HIST, MIX_A), jnp.float32)
        ext_z[0:HIST, :] = jnp.zeros((HIST, MIX_B), jnp.float32)
        h_buf[...] = h0_ref[...]

    x = x_ref[...]
    h = h_buf[...]

    ext_u[HIST:HIST + TM, :] = _bdot(h, w_in_ref[:, 0:MIX_A])
    c_gate = _bdot(h, w_in_ref[:, MIX_A + MIX_B:MIX_A + 2 * MIX_B])
    v = _bdot(h, w_in_ref[:, MIX_A + 2 * MIX_B:MIX_A + 3 * MIX_B])
    z = c_gate * v
    ext_z[HIST:HIST + TM, :] = z
    b_gate = _bdot(h, w_in_ref[:, MIX_A:MIX_A + MIX_B])

    y = ext_z[HIST - 2:HIST - 2 + TM, :] * convw_ref[0:1, :]
    y = y + ext_z[HIST - 1:HIST - 1 + TM, :] * convw_ref[1:2, :]
    y = y + z * convw_ref[2:3, :]
    y_b = b_gate * y
    ext_z[0:HIST, :] = ext_z[TM:TM + HIST, :]
    out_b = _bdot(y_b.astype(jnp.bfloat16), w_out_ref[MIX_A:MIX_A + MIX_B, :])

    row = lax.broadcasted_iota(jnp.int32, (TM, 1), 0) + i * TM + 1
    y_a = []
    for gi, w in enumerate(POOL_WINDOWS):
        c0 = gi * POOL_CH
        u = ext_u[HIST:HIST + TM, c0:c0 + POOL_CH]
        acc = u
        for s in range(1, w):
            acc = acc + ext_u[HIST - s:HIST - s + TM, c0:c0 + POOL_CH]
        cnt = jnp.minimum(row, w).astype(jnp.float32)
        pooled = (acc / cnt - u).astype(jnp.bfloat16)
        y_a.append(_bdot(pooled, w_pool_ref[gi]))
    y_a = jnp.concatenate(y_a, axis=-1) * pscale_ref[...]
    ext_u[0:HIST, :] = ext_u[TM:TM + HIST, :]
    out_a = _bdot(y_a.astype(jnp.bfloat16), w_out_ref[0:MIX_A, :])

    x1 = x + (out_a + out_b)
    x1_ref[...] = x1

    h2_f32 = _rms(x1, g2_ref[...])
    _store_rows(h2_ref, h2_f32)
    h2 = h2_f32.astype(jnp.bfloat16)

    logits = _bdot(h2, w_r_ref[...]) + b_r_ref[...]
    lane = lax.broadcasted_iota(jnp.int32, (TM, LANES), 1)
    lane_f = lane.astype(jnp.float32)

    def first_argmax(vals, vmax):
        return jnp.min(jnp.where(vals == vmax, lane_f, float(LANES)), axis=-1, keepdims=True)

    gl = jnp.where(lane < N_GROUPS, logits, _NEG)
    gmax = jnp.max(gl, axis=-1, keepdims=True)
    g_w = 1.0 / jnp.sum(jnp.exp(gl - gmax), axis=-1, keepdims=True)
    grp = first_argmax(gl, gmax).astype(jnp.int32)
    lo = N_GROUPS + E_PER_GROUP * grp
    el = jnp.where((lane >= lo) & (lane < lo + E_PER_GROUP), logits, _NEG)
    emax = jnp.max(el, axis=-1, keepdims=True)
    idx1 = first_argmax(el, emax)
    esum = jnp.sum(jnp.exp(el - emax), axis=-1, keepdims=True)
    el2 = jnp.where(lane_f == idx1, _NEG, el)
    e2max = jnp.max(el2, axis=-1, keepdims=True)
    idx2 = first_argmax(el2, e2max)
    p1 = 1.0 / esum
    p2 = jnp.exp(e2max - emax) / esum
    tot = p1 + p2
    w1 = g_w * (p1 / tot)
    w2 = g_w * (p2 / tot)

    @pl.when(i == 0)
    def _():
        running[...] = jnp.zeros_like(running)

    pick1 = lane_f == idx1
    pick2 = lane_f == idx2
    chosen = (pick1 | pick2).astype(jnp.float32)
    r_iota = lax.broadcasted_iota(jnp.int32, (TM, TM), 0)
    c_iota = lax.broadcasted_iota(jnp.int32, (TM, TM), 1)
    earlier = (c_iota < r_iota).astype(jnp.bfloat16)
    before = _bdot(earlier, chosen.astype(jnp.bfloat16)) + running[0:1, :]
    pos1 = jnp.sum(jnp.where(pick1, before, 0.0), axis=-1, keepdims=True)
    pos2 = jnp.sum(jnp.where(pick2, before, 0.0), axis=-1, keepdims=True)
    running[0:1, :] = running[0:1, :] + jnp.sum(chosen, axis=0, keepdims=True)
    counts_ref[...] = jnp.broadcast_to(running[0:1, :], counts_ref.shape)

    route = jnp.where(lane == 0, w1, 0.0)
    route = jnp.where(lane == 1, w2, route)
    route = jnp.where(lane == 2, idx1 - N_GROUPS, route)
    route = jnp.where(lane == 3, idx2 - N_GROUPS, route)
    route = jnp.where(lane == 4, pos1, route)
    route = jnp.where(lane == 5, pos2, route)
    route_ref[...] = route

    h_buf[...] = _rms(x_next_ref[...], g1_ref[...]).astype(jnp.bfloat16)


def _first_tile_norm_kernel(x_ref, g_ref, o_ref):
    o_ref[...] = _rms(x_ref[...], g_ref[...]).astype(jnp.bfloat16)


def _first_tile_norm(x, g):
    return pl.pallas_call(
        _first_tile_norm_kernel,
        grid=(1,),
        in_specs=[pl.BlockSpec((TM, D_MODEL), lambda i: (0, 0)),
                  pl.BlockSpec((1, D_MODEL), lambda i: (0, 0))],
        out_specs=pl.BlockSpec((TM, D_MODEL), lambda i: (0, 0)),
        out_shape=jax.ShapeDtypeStruct((TM, D_MODEL), jnp.bfloat16),
        name="first_tile_norm",
    )(x, g)


def _mixer(x, g1, w_in, w_pool, pscale, convw, w_out, g2, w_r, b_r):
    t = x.shape[0]
    h0 = _first_tile_norm(x, g1)
    const = lambda shape: pl.BlockSpec(shape, lambda i: (0,) * len(shape),
                                       pipeline_mode=pl.Buffered(1))
    return pl.pallas_call(
        _mixer_kernel,
        grid=(t // TM,),
        in_specs=[
            pl.BlockSpec((TM, D_MODEL), lambda i: (i, 0)),
            pl.BlockSpec((TM, D_MODEL), lambda i: (jnp.minimum(i + 1, t // TM - 1), 0)),
            const((TM, D_MODEL)),
            const((1, D_MODEL)),
            const(w_in.shape),
            const(w_pool.shape),
            const((1, MIX_A)),
            const((CONV_W, MIX_B)),
            const(w_out.shape),
            const((1, D_MODEL)),
            const(w_r.shape),
            const((1, LANES)),
        ],
        out_specs=[
            pl.BlockSpec((TM, D_MODEL), lambda i: (i, 0)),
            pl.BlockSpec((TM * ROW_SUB, LANES), lambda i: (i, 0)),
            pl.BlockSpec((TM, LANES), lambda i: (i, 0)),
            pl.BlockSpec((SUBLANES, LANES), lambda i: (0, 0)),
        ],
        out_shape=[
            jax.ShapeDtypeStruct((t, D_MODEL), jnp.float32),
            jax.ShapeDtypeStruct((t * ROW_SUB, LANES), jnp.bfloat16),
            jax.ShapeDtypeStruct((t, LANES), jnp.float32),
            jax.ShapeDtypeStruct((SUBLANES, LANES), jnp.float32),
        ],
        scratch_shapes=[
            pltpu.VMEM((TM + HIST, MIX_A), jnp.float32),
            pltpu.VMEM((TM + HIST, MIX_B), jnp.float32),
            pltpu.VMEM((TM, D_MODEL), jnp.bfloat16),
            pltpu.VMEM((SUBLANES, LANES), jnp.float32),
        ],
        compiler_params=pltpu.CompilerParams(
            dimension_semantics=("arbitrary",), vmem_limit_bytes=VMEM_LIMIT),
        name="mixer",
    )(x, x, h0, g1, w_in, w_pool, pscale, convw, w_out, g2, w_r, b_r)


def _row_copy(src_hbm, row_start, dst_ref, r, sem):
    dst_start = r * ROW_SUB if isinstance(r, int) else pl.multiple_of(r * ROW_SUB, ROW_SUB)
    return pltpu.make_async_copy(src_hbm.at[pl.ds(pl.multiple_of(row_start, ROW_SUB), ROW_SUB)],
                                 dst_ref.at[pl.ds(dst_start, ROW_SUB)], sem)


def _gather_rows_issued(n_rows):
    if isinstance(n_rows, int):
        return n_rows
    return ((n_rows + GATHER_UNROLL - 1) >> GATHER_SHIFT) << GATHER_SHIFT


def _start_row_gather(src_hbm, idx_ref, base, n_rows, dst_ref, sem):
    if isinstance(n_rows, int):
        for r in range(n_rows):
            _row_copy(src_hbm, idx_ref[base + r], dst_ref, r, sem).start(priority=r % DMA_QUEUES)
        return

    def body(i, carry):
        for u in range(GATHER_UNROLL):
            r = i * GATHER_UNROLL + u
            src_row = idx_ref[base + jnp.minimum(r, n_rows - 1)]
            _row_copy(src_hbm, src_row, dst_ref, r, sem).start(priority=u % DMA_QUEUES)
        return carry

    lax.fori_loop(0, _gather_rows_issued(n_rows) >> GATHER_SHIFT, body, 0)


def _wait_row_gather(src_hbm, dst_ref, sem, n_rows):
    n = _gather_rows_issued(n_rows) * ROW_SUB
    pltpu.make_async_copy(src_hbm.at[pl.ds(0, n)], dst_ref.at[pl.ds(0, n)], sem).wait()


def _weight_copies(w_hbm, e, buf, slot, k, sem):
    rows = w_hbm.shape[1] // WEIGHT_DMA_CHUNKS
    return [pltpu.make_async_copy(w_hbm.at[e, pl.ds(c * rows, rows)],
                                  buf.at[slot, k, pl.ds(c * rows, rows)], sem.at[slot])
            for c in range(WEIGHT_DMA_CHUNKS)]


def _start_weight(*args):
    for cp in _weight_copies(*args):
        cp.start(priority=WEIGHT_DMA_PRIORITY)


def _wait_weight(*args):
    for cp in _weight_copies(*args):
        cp.wait()


def _build_row_sources(dest_ref, row_ref):
    n_assign = dest_ref.shape[0]

    def scatter(i, carry):
        for u in range(SCALAR_UNROLL):
            a = i * SCALAR_UNROLL + u
            row_ref[dest_ref[a]] = (a >> TOP_K_SHIFT) * ROW_SUB
        return carry

    lax.fori_loop(0, n_assign // SCALAR_UNROLL, scatter, 0)


def _gate_up_kernel(blk_e_ref, first_ref, wslot_ref, nxt_e_ref, n_used_ref, n_valid_ref, dest_ref,
                    h2_hbm, wg_hbm, wu_hbm, hid_ref,
                    xg, wbuf, wg_bf, wu_bf, row_ref, gsem, wsem):
    b = pl.program_id(0)
    n_used = n_used_ref[0]
    e = blk_e_ref[b]
    slot = wslot_ref[b]

    @pl.when(b == 0)
    def _():
        _start_weight(wg_hbm, e, wbuf, slot, 0, wsem)
        _start_weight(wu_hbm, e, wbuf, slot, 1, wsem)
        xg[...] = jnp.zeros_like(xg)
        _build_row_sources(dest_ref, row_ref)
        _start_row_gather(h2_hbm, row_ref, 0, n_valid_ref[0], xg.at[0], gsem.at[0])

    @pl.when(b < n_used)
    def _():
        cur = b % 2

        @pl.when(b + 1 < n_used)
        def _():
            _start_row_gather(h2_hbm, row_ref, (b + 1) * ROW_BLK, n_valid_ref[b + 1],
                              xg.at[1 - cur], gsem.at[1 - cur])

        @pl.when(first_ref[b] == 1)
        def _():
            _wait_weight(wg_hbm, e, wbuf, slot, 0, wsem)
            _wait_weight(wu_hbm, e, wbuf, slot, 1, wsem)
            nxt = nxt_e_ref[b]

            @pl.when(nxt != e)
            def _():
                _start_weight(wg_hbm, nxt, wbuf, 1 - slot, 0, wsem)
                _start_weight(wu_hbm, nxt, wbuf, 1 - slot, 1, wsem)

            wg_bf[...] = wbuf[slot, 0].astype(jnp.bfloat16)
            wu_bf[...] = wbuf[slot, 1].astype(jnp.bfloat16)

        _wait_row_gather(h2_hbm, xg.at[cur], gsem.at[cur], n_valid_ref[b])
        xb = _load_rows(xg.at[cur], ROW_BLK)
        gate = _bdot(xb, wg_bf[...])
        up = _bdot(xb, wu_bf[...])
        hid_ref[...] = (gate * jax.nn.sigmoid(gate) * up).astype(jnp.bfloat16)

    @pl.when(b >= n_used)
    def _():
        hid_ref[...] = jnp.zeros_like(hid_ref)


def _gate_up(plan, h2, wg, wu):
    n_blocks = plan["blk_e"].shape[0]
    any_spec = pl.BlockSpec(memory_space=pl.ANY)
    return pl.pallas_call(
        _gate_up_kernel,
        grid_spec=pltpu.PrefetchScalarGridSpec(
            num_scalar_prefetch=7,
            grid=(n_blocks,),
            in_specs=[any_spec, any_spec, any_spec],
            out_specs=pl.BlockSpec((ROW_BLK, D_EXPERT), lambda b, *_: (b, 0)),
            scratch_shapes=[
                pltpu.VMEM((2, ROW_BLK * ROW_SUB, LANES), jnp.bfloat16),
                pltpu.VMEM((2, 2, D_MODEL, D_EXPERT), jnp.float32),
                pltpu.VMEM((D_MODEL, D_EXPERT), jnp.bfloat16),
                pltpu.VMEM((D_MODEL, D_EXPERT), jnp.bfloat16),
                pltpu.SMEM((n_blocks * ROW_BLK,), jnp.int32),
                pltpu.SemaphoreType.DMA((2,)),
                pltpu.SemaphoreType.DMA((2,)),
            ],
        ),
        out_shape=jax.ShapeDtypeStruct((n_blocks * ROW_BLK, D_EXPERT), jnp.bfloat16),
        compiler_params=pltpu.CompilerParams(
            dimension_semantics=("arbitrary",), vmem_limit_bytes=VMEM_LIMIT),
        name="gate_up",
    )(plan["blk_e"], plan["first"], plan["wslot"], plan["nxt_e"], plan["n_used"], plan["n_valid"],
      plan["dest"], h2, wg, wu)


def _down_kernel(blk_e_ref, first_ref, wslot_ref, nxt_e_ref, n_used_ref,
                 hid_ref, wd_hbm, y_ref, wbuf, wd_bf, wsem):
    b = pl.program_id(0)
    n_used = n_used_ref[0]
    e = blk_e_ref[b]
    slot = wslot_ref[b]

    @pl.when(b == 0)
    def _():
        _start_weight(wd_hbm, e, wbuf, slot, 0, wsem)

    @pl.when(b < n_used)
    def _():
        @pl.when(first_ref[b] == 1)
        def _():
            _wait_weight(wd_hbm, e, wbuf, slot, 0, wsem)
            nxt = nxt_e_ref[b]

            @pl.when(nxt != e)
            def _():
                _start_weight(wd_hbm, nxt, wbuf, 1 - slot, 0, wsem)

            wd_bf[...] = wbuf[slot, 0].astype(jnp.bfloat16)

        _store_rows(y_ref, _bdot(hid_ref[...], wd_bf[...]))

    @pl.when(b >= n_used)
    def _():
        y_ref[...] = jnp.zeros_like(y_ref)


def _down(plan, hid, wd):
    n_blocks = plan["blk_e"].shape[0]
    return pl.pallas_call(
        _down_kernel,
        grid_spec=pltpu.PrefetchScalarGridSpec(
            num_scalar_prefetch=5,
            grid=(n_blocks,),
            in_specs=[pl.BlockSpec((ROW_BLK, D_EXPERT), lambda b, *_: (b, 0)),
                      pl.BlockSpec(memory_space=pl.ANY)],
            out_specs=pl.BlockSpec((ROW_BLK * ROW_SUB, LANES), lambda b, *_: (b, 0)),
            scratch_shapes=[
                pltpu.VMEM((2, 1, D_EXPERT, D_MODEL), jnp.float32),
                pltpu.VMEM((D_EXPERT, D_MODEL), jnp.bfloat16),
                pltpu.SemaphoreType.DMA((2,)),
            ],
        ),
        out_shape=jax.ShapeDtypeStruct((n_blocks * ROW_BLK * ROW_SUB, LANES), jnp.bfloat16),
        compiler_params=pltpu.CompilerParams(
            dimension_semantics=("arbitrary",), vmem_limit_bytes=VMEM_LIMIT),
        name="down",
    )(plan["blk_e"], plan["first"], plan["wslot"], plan["nxt_e"], plan["n_used"], hid, wd)


def _combine_kernel(d0_ref, d1_ref, x1_ref, route_ref, g_ref, y_hbm, o_ref, y0_buf, y1_buf, sem):
    i = pl.program_id(0)
    n = pl.num_programs(0)
    cur = i % 2

    def start(step, slot):
        _start_row_gather(y_hbm, d0_ref, step * TM, TM, y0_buf.at[slot], sem.at[0, slot])
        _start_row_gather(y_hbm, d1_ref, step * TM, TM, y1_buf.at[slot], sem.at[1, slot])

    @pl.when(i == 0)
    def _():
        start(0, 0)

    @pl.when(i + 1 < n)
    def _():
        start(i + 1, 1 - cur)

    r = route_ref[...]
    _wait_row_gather(y_hbm, y0_buf.at[cur], sem.at[0, cur], TM)
    _wait_row_gather(y_hbm, y1_buf.at[cur], sem.at[1, cur], TM)
    y0 = _load_rows(y0_buf.at[cur], TM).astype(jnp.float32)
    y1 = _load_rows(y1_buf.at[cur], TM).astype(jnp.float32)
    xo = x1_ref[...] + (y0 * r[:, 0:1] + y1 * r[:, 1:2])
    o_ref[...] = _rms(xo, g_ref[...])


def _combine(d0, d1, x1, route, g, ybuf):
    t = x1.shape[0]
    tile = lambda w: pl.BlockSpec((TM, w), lambda i, *_: (i, 0))
    return pl.pallas_call(
        _combine_kernel,
        grid_spec=pltpu.PrefetchScalarGridSpec(
            num_scalar_prefetch=2,
            grid=(t // TM,),
            in_specs=[tile(D_MODEL), tile(LANES),
                      pl.BlockSpec((1, D_MODEL), lambda i, *_: (0, 0)),
                      pl.BlockSpec(memory_space=pl.ANY)],
            out_specs=tile(D_MODEL),
            scratch_shapes=[pltpu.VMEM((2, TM * ROW_SUB, LANES), jnp.bfloat16),
                            pltpu.VMEM((2, TM * ROW_SUB, LANES), jnp.bfloat16),
                            pltpu.SemaphoreType.DMA((2, 2))],
        ),
        out_shape=jax.ShapeDtypeStruct((t, D_MODEL), jnp.float32),
        compiler_params=pltpu.CompilerParams(
            dimension_semantics=("arbitrary",), vmem_limit_bytes=VMEM_LIMIT),
        name="combine",
    )(d0, d1, x1, route, g, ybuf)


def _dispatch_plan(expert_id, pos, counts):
    t = expert_id.shape[0]
    a = t * TOP_K
    i32 = jnp.int32
    e_flat = expert_id.reshape(a)
    pos = pos.reshape(a)
    experts = jnp.arange(N_EXPERTS, dtype=i32)
    padded = ((counts + ROW_BLK - 1) // ROW_BLK) * ROW_BLK
    pends = jnp.cumsum(padded)
    pstarts = pends - padded
    dest = (pstarts[e_flat] + pos).astype(i32)
    n_blocks = (a + N_EXPERTS * (ROW_BLK - 1) + ROW_BLK - 1) // ROW_BLK
    blk_start = jnp.arange(n_blocks, dtype=i32) * ROW_BLK
    blk_e = jnp.clip(jnp.sum(pends[None, :] <= blk_start[:, None], axis=1), 0, N_EXPERTS - 1).astype(i32)
    first = jnp.concatenate([jnp.ones((1,), i32), (blk_e[1:] != blk_e[:-1]).astype(i32)])
    wslot = (jnp.cumsum(first) - 1) % 2
    later_used = (experts[None, :] > experts[:, None]) & (counts[None, :] > 0)
    nxt = jnp.min(jnp.where(later_used, experts[None, :], N_EXPERTS), axis=1)
    nxt = jnp.where(nxt == N_EXPERTS, experts, nxt)
    return {
        "dest": dest,
        "blk_e": blk_e,
        "first": first,
        "wslot": wslot.astype(i32),
        "nxt_e": nxt[blk_e].astype(i32),
        "n_used": (pends[-1] // ROW_BLK).astype(i32).reshape(1),
        "n_valid": jnp.clip(counts[blk_e] - (blk_start - pstarts[blk_e]), 0, ROW_BLK).astype(i32),
    }


def kernel(x, norm_mix_g, w_in, w_pool, pool_scale, conv_w, w_out, norm_ffn_g, w_router_group, b_router_group, w_router_expert, b_router_expert, w_gate, w_up, w_down, norm_final_g):
    bt, s, d = x.shape
    t = bt * s
    bf = jnp.bfloat16
    xt = x.reshape(t, d)
    assert norm_mix_g.shape[0] == 1, "the final norm is fused into the last layer's combine"
    for l in range(norm_mix_g.shape[0]):
        w_r = jnp.concatenate([w_router_group[l], w_router_expert[l]], axis=1)
        w_r = jnp.pad(w_r, ((0, 0), (0, LANES - w_r.shape[1]))).astype(bf)
        b_r = jnp.concatenate([b_router_group[l], b_router_expert[l]])
        b_r = jnp.pad(b_r, (0, LANES - b_r.shape[0])).reshape(1, LANES)
        x1, h2, route, counts_row = _mixer(
            xt, norm_mix_g[l].reshape(1, d), w_in[l].astype(bf), w_pool[l].astype(bf),
            pool_scale[l].reshape(1, MIX_A), conv_w[l].T, w_out[l].astype(bf),
            norm_ffn_g[l].reshape(1, d), w_r, b_r)
        counts = counts_row[0, N_GROUPS:N_GROUPS + N_EXPERTS].astype(jnp.int32)
        plan = _dispatch_plan(route[:, 2:4].astype(jnp.int32), route[:, 4:6].astype(jnp.int32), counts)
        hid = _gate_up(plan, h2, w_gate[l], w_up[l])
        ybuf = _down(plan, hid, w_down[l])
        dest2 = plan["dest"].reshape(t, TOP_K) * ROW_SUB
        xt = _combine(dest2[:, 0], dest2[:, 1], x1, route, norm_final_g.reshape(1, d), ybuf)
    return xt.reshape(bt, s, d)
```

```python
import jax
import jax.numpy as jnp
from jax import lax
from jax.experimental import pallas as pl
from jax.experimental.pallas import tpu as pltpu

D_MODEL = 2048
MIX_A = 1024
MIX_B = 1024
POOL_WINDOWS = (2, 4, 8, 16)
POOL_CH = MIX_A // len(POOL_WINDOWS)
CONV_W = 3
N_GROUPS = 4
E_PER_GROUP = 8
N_EXPERTS = N_GROUPS * E_PER_GROUP
TOP_K = 2
D_EXPERT = D_MODEL // 2
EPS = 1e-6

LANES = 128
ROW_SUB = D_MODEL // LANES
HIST = 16
TM = 256
ROW_BLK = 256
VMEM_LIMIT = 56 * 1024 * 1024
WEIGHT_DMA_PRIORITY = 1
WEIGHT_DMA_CHUNKS = 8
DMA_QUEUES = 2
GATHER_SHIFT = 2
GATHER_UNROLL = 1 << GATHER_SHIFT
SCALAR_UNROLL = 8

_NEG = -1e30


def _rms(x, g):
    return x * lax.rsqrt(jnp.mean(x * x, axis=-1, keepdims=True) + EPS) * g


def _bdot(a, b):
    return jnp.dot(a, b, preferred_element_type=jnp.float32)


def _store_rows(ref, val):
    ref[...] = val.astype(jnp.bfloat16).reshape(val.shape[0] * ROW_SUB, LANES)


def _load_rows(ref, n):
    return ref[...].reshape(n, D_MODEL)


def _mixer_kernel(x_ref, x_next_ref, h0_ref, g1_ref, w_in_ref, w_pool_ref, pscale_ref, convw_ref, w_out_ref,
                  g2_ref, w_r_ref, b_r_ref,
                  x1_ref, h2_ref, route_ref,
                  ext_u, ext_z, h_buf):
    i = pl.program_id(0)

    @pl.when(i == 0)
    def _():
        ext_u[0:HIST, :] = jnp.zeros((HIST, MIX_A), jnp.float32)
        ext_z[0:HIST, :] = jnp.zeros((HIST, MIX_B), jnp.float32)
        h_buf[...] = h0_ref[...]

    x = x_ref[...]
    h = h_buf[...]

    ext_u[HIST:HIST + TM, :] = _bdot(h, w_in_ref[:, 0:MIX_A])
    c_gate = _bdot(h, w_in_ref[:, MIX_A + MIX_B:MIX_A + 2 * MIX_B])
    v = _bdot(h, w_in_ref[:, MIX_A + 2 * MIX_B:MIX_A + 3 * MIX_B])
    z = c_gate * v
    ext_z[HIST:HIST + TM, :] = z
    b_gate = _bdot(h, w_in_ref[:, MIX_A:MIX_A + MIX_B])

    y = ext_z[HIST - 2:HIST - 2 + TM, :] * convw_ref[0:1, :]
    y = y + ext_z[HIST - 1:HIST - 1 + TM, :] * convw_ref[1:2, :]
    y = y + z * convw_ref[2:3, :]
    y_b = b_gate * y
    ext_z[0:HIST, :] = ext_z[TM:TM + HIST, :]
    out_b = _bdot(y_b.astype(jnp.bfloat16), w_out_ref[MIX_A:MIX_A + MIX_B, :])

    row = lax.broadcasted_iota(jnp.int32, (TM, 1), 0) + i * TM + 1
    y_a = []
    for gi, w in enumerate(POOL_WINDOWS):
        c0 = gi * POOL_CH
        u = ext_u[HIST:HIST + TM, c0:c0 + POOL_CH]
        acc = u
        for s in range(1, w):
            acc = acc + ext_u[HIST - s:HIST - s + TM, c0:c0 + POOL_CH]
        cnt = jnp.minimum(row, w).astype(jnp.float32)
        pooled = (acc / cnt - u).astype(jnp.bfloat16)
        y_a.append(_bdot(pooled, w_pool_ref[gi]))
    y_a = jnp.concatenate(y_a, axis=-1) * pscale_ref[...]
    ext_u[0:HIST, :] = ext_u[TM:TM + HIST, :]
    out_a = _bdot(y_a.astype(jnp.bfloat16), w_out_ref[0:MIX_A, :])

    x1 = x + (out_a + out_b)
    x1_ref[...] = x1

    h2_f32 = _rms(x1, g2_ref[...])
    _store_rows(h2_ref, h2_f32)
    h2 = h2_f32.astype(jnp.bfloat16)

    logits = _bdot(h2, w_r_ref[...]) + b_r_ref[...]
    lane = lax.broadcasted_iota(jnp.int32, (TM, LANES), 1)
    lane_f = lane.astype(jnp.float32)

    def first_argmax(vals, vmax):
        return jnp.min(jnp.where(vals == vmax, lane_f, float(LANES)), axis=-1, keepdims=True)

    gl = jnp.where(lane < N_GROUPS, logits, _NEG)
    gmax = jnp.max(gl, axis=-1, keepdims=True)
    g_w = 1.0 / jnp.sum(jnp.exp(gl - gmax), axis=-1, keepdims=True)
    grp = first_argmax(gl, gmax).astype(jnp.int32)
    lo = N_GROUPS + E_PER_GROUP * grp
    el = jnp.where((lane >= lo) & (lane < lo + E_PER_GROUP), logits, _NEG)
    emax = jnp.max(el, axis=-1, keepdims=True)
    idx1 = first_argmax(el, emax)
    esum = jnp.sum(jnp.exp(el - emax), axis=-1, keepdims=True)
    el2 = jnp.where(lane_f == idx1, _NEG, el)
    e2max = jnp.max(el2, axis=-1, keepdims=True)
    idx2 = first_argmax(el2, e2max)
    p1 = 1.0 / esum
    p2 = jnp.exp(e2max - emax) / esum
    tot = p1 + p2
    w1 = g_w * (p1 / tot)
    w2 = g_w * (p2 / tot)
    route = jnp.where(lane == 0, w1, 0.0)
    route = jnp.where(lane == 1, w2, route)
    route = jnp.where(lane == 2, idx1 - N_GROUPS, route)
    route = jnp.where(lane == 3, idx2 - N_GROUPS, route)
    route_ref[...] = route

    h_buf[...] = _rms(x_next_ref[...], g1_ref[...]).astype(jnp.bfloat16)


def _first_tile_norm_kernel(x_ref, g_ref, o_ref):
    o_ref[...] = _rms(x_ref[...], g_ref[...]).astype(jnp.bfloat16)


def _first_tile_norm(x, g):
    return pl.pallas_call(
        _first_tile_norm_kernel,
        grid=(1,),
        in_specs=[pl.BlockSpec((TM, D_MODEL), lambda i: (0, 0)),
                  pl.BlockSpec((1, D_MODEL), lambda i: (0, 0))],
        out_specs=pl.BlockSpec((TM, D_MODEL), lambda i: (0, 0)),
        out_shape=jax.ShapeDtypeStruct((TM, D_MODEL), jnp.bfloat16),
        name="first_tile_norm",
    )(x, g)


def _mixer(x, g1, w_in, w_pool, pscale, convw, w_out, g2, w_r, b_r):
    t = x.shape[0]
    h0 = _first_tile_norm(x, g1)
    const = lambda shape: pl.BlockSpec(shape, lambda i: (0,) * len(shape),
                                       pipeline_mode=pl.Buffered(1))
    return pl.pallas_call(
        _mixer_kernel,
        grid=(t // TM,),
        in_specs=[
            pl.BlockSpec((TM, D_MODEL), lambda i: (i, 0)),
            pl.BlockSpec((TM, D_MODEL), lambda i: (jnp.minimum(i + 1, t // TM - 1), 0)),
            const((TM, D_MODEL)),
            const((1, D_MODEL)),
            const(w_in.shape),
            const(w_pool.shape),
            const((1, MIX_A)),
            const((CONV_W, MIX_B)),
            const(w_out.shape),
            const((1, D_MODEL)),
            const(w_r.shape),
            const((1, LANES)),
        ],
        out_specs=[
            pl.BlockSpec((TM, D_MODEL), lambda i: (i, 0)),
            pl.BlockSpec((TM * ROW_SUB, LANES), lambda i: (i, 0)),
            pl.BlockSpec((TM, LANES), lambda i: (i, 0)),
        ],
        out_shape=[
            jax.ShapeDtypeStruct((t, D_MODEL), jnp.float32),
            jax.ShapeDtypeStruct((t * ROW_SUB, LANES), jnp.bfloat16),
            jax.ShapeDtypeStruct((t, LANES), jnp.float32),
        ],
        scratch_shapes=[
            pltpu.VMEM((TM + HIST, MIX_A), jnp.float32),
            pltpu.VMEM((TM + HIST, MIX_B), jnp.float32),
            pltpu.VMEM((TM, D_MODEL), jnp.bfloat16),
        ],
        compiler_params=pltpu.CompilerParams(
            dimension_semantics=("arbitrary",), vmem_limit_bytes=VMEM_LIMIT),
        name="mixer",
    )(x, x, h0, g1, w_in, w_pool, pscale, convw, w_out, g2, w_r, b_r)


def _row_copy(src_hbm, row_start, dst_ref, r, sem):
    dst_start = r * ROW_SUB if isinstance(r, int) else pl.multiple_of(r * ROW_SUB, ROW_SUB)
    return pltpu.make_async_copy(src_hbm.at[pl.ds(pl.multiple_of(row_start, ROW_SUB), ROW_SUB)],
                                 dst_ref.at[pl.ds(dst_start, ROW_SUB)], sem)


def _gather_rows_issued(n_rows):
    if isinstance(n_rows, int):
        return n_rows
    return ((n_rows + GATHER_UNROLL - 1) >> GATHER_SHIFT) << GATHER_SHIFT


def _start_row_gather(src_hbm, idx_ref, base, n_rows, dst_ref, sem):
    if isinstance(n_rows, int):
        for r in range(n_rows):
            _row_copy(src_hbm, idx_ref[base + r], dst_ref, r, sem).start(priority=r % DMA_QUEUES)
        return

    def body(i, carry):
        for u in range(GATHER_UNROLL):
            r = i * GATHER_UNROLL + u
            src_row = idx_ref[base + jnp.minimum(r, n_rows - 1)]
            _row_copy(src_hbm, src_row, dst_ref, r, sem).start(priority=u % DMA_QUEUES)
        return carry

    lax.fori_loop(0, _gather_rows_issued(n_rows) >> GATHER_SHIFT, body, 0)


def _wait_row_gather(src_hbm, dst_ref, sem, n_rows):
    n = _gather_rows_issued(n_rows) * ROW_SUB
    pltpu.make_async_copy(src_hbm.at[pl.ds(0, n)], dst_ref.at[pl.ds(0, n)], sem).wait()


def _weight_copies(w_hbm, e, buf, slot, k, sem):
    rows = w_hbm.shape[1] // WEIGHT_DMA_CHUNKS
    return [pltpu.make_async_copy(w_hbm.at[e, pl.ds(c * rows, rows)],
                                  buf.at[slot, k, pl.ds(c * rows, rows)], sem.at[slot])
            for c in range(WEIGHT_DMA_CHUNKS)]


def _start_weight(*args):
    for cp in _weight_copies(*args):
        cp.start(priority=WEIGHT_DMA_PRIORITY)


def _wait_weight(*args):
    for cp in _weight_copies(*args):
        cp.wait()


def _build_row_sources(dest_ref, row_ref):
    n_tokens = dest_ref.shape[0] // TOP_K

    def scatter(i, carry):
        for u in range(SCALAR_UNROLL):
            tok = i * SCALAR_UNROLL + u
            for k in range(TOP_K):
                row_ref[dest_ref[k * n_tokens + tok]] = tok * ROW_SUB
        return carry

    lax.fori_loop(0, n_tokens // SCALAR_UNROLL, scatter, 0)


def _gate_up_kernel(blk_e_ref, first_ref, wslot_ref, nxt_e_ref, n_used_ref, n_valid_ref, dest_ref,
                    h2_hbm, wg_hbm, wu_hbm, hid_ref,
                    xg, wbuf, wg_bf, wu_bf, row_ref, gsem, wsem):
    b = pl.program_id(0)
    n_used = n_used_ref[0]
    e = blk_e_ref[b]
    slot = wslot_ref[b]

    @pl.when(b == 0)
    def _():
        _start_weight(wg_hbm, e, wbuf, slot, 0, wsem)
        _start_weight(wu_hbm, e, wbuf, slot, 1, wsem)
        xg[...] = jnp.zeros_like(xg)
        _build_row_sources(dest_ref, row_ref)
        _start_row_gather(h2_hbm, row_ref, 0, n_valid_ref[0], xg.at[0], gsem.at[0])

    @pl.when(b < n_used)
    def _():
        cur = b % 2

        @pl.when(b + 1 < n_used)
        def _():
            _start_row_gather(h2_hbm, row_ref, (b + 1) * ROW_BLK, n_valid_ref[b + 1],
                              xg.at[1 - cur], gsem.at[1 - cur])

        @pl.when(first_ref[b] == 1)
        def _():
            _wait_weight(wg_hbm, e, wbuf, slot, 0, wsem)
            _wait_weight(wu_hbm, e, wbuf, slot, 1, wsem)
            nxt = nxt_e_ref[b]

            @pl.when(nxt != e)
            def _():
                _start_weight(wg_hbm, nxt, wbuf, 1 - slot, 0, wsem)
                _start_weight(wu_hbm, nxt, wbuf, 1 - slot, 1, wsem)

            wg_bf[...] = wbuf[slot, 0].astype(jnp.bfloat16)
            wu_bf[...] = wbuf[slot, 1].astype(jnp.bfloat16)

        _wait_row_gather(h2_hbm, xg.at[cur], gsem.at[cur], n_valid_ref[b])
        xb = _load_rows(xg.at[cur], ROW_BLK)
        gate = _bdot(xb, wg_bf[...])
        up = _bdot(xb, wu_bf[...])
        hid_ref[...] = (gate * jax.nn.sigmoid(gate) * up).astype(jnp.bfloat16)

    @pl.when(b >= n_used)
    def _():
        hid_ref[...] = jnp.zeros_like(hid_ref)


def _gate_up(plan, h2, wg, wu):
    n_blocks = plan["blk_e"].shape[0]
    any_spec = pl.BlockSpec(memory_space=pl.ANY)
    return pl.pallas_call(
        _gate_up_kernel,
        grid_spec=pltpu.PrefetchScalarGridSpec(
            num_scalar_prefetch=7,
            grid=(n_blocks,),
            in_specs=[any_spec, any_spec, any_spec],
            out_specs=pl.BlockSpec((ROW_BLK, D_EXPERT), lambda b, *_: (b, 0)),
            scratch_shapes=[
                pltpu.VMEM((2, ROW_BLK * ROW_SUB, LANES), jnp.bfloat16),
                pltpu.VMEM((2, 2, D_MODEL, D_EXPERT), jnp.float32),
                pltpu.VMEM((D_MODEL, D_EXPERT), jnp.bfloat16),
                pltpu.VMEM((D_MODEL, D_EXPERT), jnp.bfloat16),
                pltpu.SMEM((n_blocks * ROW_BLK,), jnp.int32),
                pltpu.SemaphoreType.DMA((2,)),
                pltpu.SemaphoreType.DMA((2,)),
            ],
        ),
        out_shape=jax.ShapeDtypeStruct((n_blocks * ROW_BLK, D_EXPERT), jnp.bfloat16),
        compiler_params=pltpu.CompilerParams(
            dimension_semantics=("arbitrary",), vmem_limit_bytes=VMEM_LIMIT),
        name="gate_up",
    )(plan["blk_e"], plan["first"], plan["wslot"], plan["nxt_e"], plan["n_used"], plan["n_valid"],
      plan["dest"], h2, wg, wu)


def _down_kernel(blk_e_ref, first_ref, wslot_ref, nxt_e_ref, n_used_ref,
                 hid_ref, wd_hbm, y_ref, wbuf, wd_bf, wsem):
    b = pl.program_id(0)
    n_used = n_used_ref[0]
    e = blk_e_ref[b]
    slot = wslot_ref[b]

    @pl.when(b == 0)
    def _():
        _start_weight(wd_hbm, e, wbuf, slot, 0, wsem)

    @pl.when(b < n_used)
    def _():
        @pl.when(first_ref[b] == 1)
        def _():
            _wait_weight(wd_hbm, e, wbuf, slot, 0, wsem)
            nxt = nxt_e_ref[b]

            @pl.when(nxt != e)
            def _():
                _start_weight(wd_hbm, nxt, wbuf, 1 - slot, 0, wsem)

            wd_bf[...] = wbuf[slot, 0].astype(jnp.bfloat16)

        _store_rows(y_ref, _bdot(hid_ref[...], wd_bf[...]))

    @pl.when(b >= n_used)
    def _():
        y_ref[...] = jnp.zeros_like(y_ref)


def _down(plan, hid, wd):
    n_blocks = plan["blk_e"].shape[0]
    return pl.pallas_call(
        _down_kernel,
        grid_spec=pltpu.PrefetchScalarGridSpec(
            num_scalar_prefetch=5,
            grid=(n_blocks,),
            in_specs=[pl.BlockSpec((ROW_BLK, D_EXPERT), lambda b, *_: (b, 0)),
                      pl.BlockSpec(memory_space=pl.ANY)],
            out_specs=pl.BlockSpec((ROW_BLK * ROW_SUB, LANES), lambda b, *_: (b, 0)),
            scratch_shapes=[
                pltpu.VMEM((2, 1, D_EXPERT, D_MODEL), jnp.float32),
                pltpu.VMEM((D_EXPERT, D_MODEL), jnp.bfloat16),
                pltpu.SemaphoreType.DMA((2,)),
            ],
        ),
        out_shape=jax.ShapeDtypeStruct((n_blocks * ROW_BLK * ROW_SUB, LANES), jnp.bfloat16),
        compiler_params=pltpu.CompilerParams(
            dimension_semantics=("arbitrary",), vmem_limit_bytes=VMEM_LIMIT),
        name="down",
    )(plan["blk_e"], plan["first"], plan["wslot"], plan["nxt_e"], plan["n_used"], hid, wd)


def _combine_kernel(dest_ref, x1_ref, route_ref, g_ref, y_hbm, o_ref, y0_buf, y1_buf, sem):
    i = pl.program_id(0)
    n = pl.num_programs(0)
    cur = i % 2
    n_tokens = n * TM

    def start(step, slot):
        _start_row_gather(y_hbm, dest_ref, step * TM, TM, y0_buf.at[slot], sem.at[0, slot])
        _start_row_gather(y_hbm, dest_ref, n_tokens + step * TM, TM, y1_buf.at[slot], sem.at[1, slot])

    @pl.when(i == 0)
    def _():
        start(0, 0)

    @pl.when(i + 1 < n)
    def _():
        start(i + 1, 1 - cur)

    r = route_ref[...]
    _wait_row_gather(y_hbm, y0_buf.at[cur], sem.at[0, cur], TM)
    _wait_row_gather(y_hbm, y1_buf.at[cur], sem.at[1, cur], TM)
    y0 = _load_rows(y0_buf.at[cur], TM).astype(jnp.float32)
    y1 = _load_rows(y1_buf.at[cur], TM).astype(jnp.float32)
    xo = x1_ref[...] + (y0 * r[:, 0:1] + y1 * r[:, 1:2])
    o_ref[...] = _rms(xo, g_ref[...])


def _combine(dest_rows, x1, route, g, ybuf):
    t = x1.shape[0]
    tile = lambda w: pl.BlockSpec((TM, w), lambda i, *_: (i, 0))
    return pl.pallas_call(
        _combine_kernel,
        grid_spec=pltpu.PrefetchScalarGridSpec(
            num_scalar_prefetch=1,
            grid=(t // TM,),
            in_specs=[tile(D_MODEL), tile(LANES),
                      pl.BlockSpec((1, D_MODEL), lambda i, *_: (0, 0)),
                      pl.BlockSpec(memory_space=pl.ANY)],
            out_specs=tile(D_MODEL),
            scratch_shapes=[pltpu.VMEM((2, TM * ROW_SUB, LANES), jnp.bfloat16),
                            pltpu.VMEM((2, TM * ROW_SUB, LANES), jnp.bfloat16),
                            pltpu.SemaphoreType.DMA((2, 2))],
        ),
        out_shape=jax.ShapeDtypeStruct((t, D_MODEL), jnp.float32),
        compiler_params=pltpu.CompilerParams(
            dimension_semantics=("arbitrary",), vmem_limit_bytes=VMEM_LIMIT),
        name="combine",
    )(dest_rows, x1, route, g, ybuf)


def _dispatch_plan(e_flat):
    a = e_flat.shape[0]
    i32 = jnp.int32
    experts = jnp.arange(N_EXPERTS, dtype=i32)
    onehot = (e_flat[:, None] == experts[None, :]).astype(i32)
    rank = jnp.cumsum(onehot, axis=0) - onehot
    counts = jnp.sum(onehot, axis=0)
    padded = ((counts + ROW_BLK - 1) // ROW_BLK) * ROW_BLK
    pends = jnp.cumsum(padded)
    pstarts = pends - padded
    dest = jnp.sum((rank + pstarts[None, :]) * onehot, axis=1).astype(i32)
    n_blocks = (a + N_EXPERTS * (ROW_BLK - 1) + ROW_BLK - 1) // ROW_BLK
    blk_start = jnp.arange(n_blocks, dtype=i32) * ROW_BLK
    blk_e = jnp.clip(jnp.sum(pends[None, :] <= blk_start[:, None], axis=1), 0, N_EXPERTS - 1).astype(i32)
    first = jnp.concatenate([jnp.ones((1,), i32), (blk_e[1:] != blk_e[:-1]).astype(i32)])
    wslot = (jnp.cumsum(first) - 1) % 2
    later_used = (experts[None, :] > experts[:, None]) & (counts[None, :] > 0)
    nxt = jnp.min(jnp.where(later_used, experts[None, :], N_EXPERTS), axis=1)
    nxt = jnp.where(nxt == N_EXPERTS, experts, nxt)
    return {
        "dest": dest,
        "blk_e": blk_e,
        "first": first,
        "wslot": wslot.astype(i32),
        "nxt_e": nxt[blk_e].astype(i32),
        "n_used": (pends[-1] // ROW_BLK).astype(i32).reshape(1),
        "n_valid": jnp.clip(counts[blk_e] - (blk_start - pstarts[blk_e]), 0, ROW_BLK).astype(i32),
    }


def kernel(x, norm_mix_g, w_in, w_pool, pool_scale, conv_w, w_out, norm_ffn_g, w_router_group, b_router_group, w_router_expert, b_router_expert, w_gate, w_up, w_down, norm_final_g):
    bt, s, d = x.shape
    t = bt * s
    bf = jnp.bfloat16
    xt = x.reshape(t, d)
    assert norm_mix_g.shape[0] == 1, "the final norm is fused into the last layer's combine"
    for l in range(norm_mix_g.shape[0]):
        w_r = jnp.concatenate([w_router_group[l], w_router_expert[l]], axis=1)
        w_r = jnp.pad(w_r, ((0, 0), (0, LANES - w_r.shape[1]))).astype(bf)
        b_r = jnp.concatenate([b_router_group[l], b_router_expert[l]])
        b_r = jnp.pad(b_r, (0, LANES - b_r.shape[0])).reshape(1, LANES)
        x1, h2, route = _mixer(
            xt, norm_mix_g[l].reshape(1, d), w_in[l].astype(bf), w_pool[l].astype(bf),
            pool_scale[l].reshape(1, MIX_A), conv_w[l].T, w_out[l].astype(bf),
            norm_ffn_g[l].reshape(1, d), w_r, b_r)
        e_flat = jnp.concatenate([route[:, 2 + k] for k in range(TOP_K)]).astype(jnp.int32)
        plan = _dispatch_plan(e_flat)
        hid = _gate_up(plan, h2, w_gate[l], w_up[l])
        ybuf = _down(plan, hid, w_down[l])
        xt = _combine(plan["dest"] * ROW_SUB, x1, route, norm_final_g.reshape(1, d), ybuf)
    return xt.reshape(bt, s, d)
```

```python
import jax
import jax.numpy as jnp
from jax import lax
from jax.experimental import pallas as pl
from jax.experimental.pallas import tpu as pltpu

D_MODEL = 2048
MIX_A = 1024
MIX_B = 1024
POOL_WINDOWS = (2, 4, 8, 16)
POOL_CH = MIX_A // len(POOL_WINDOWS)
CONV_W = 3
N_GROUPS = 4
E_PER_GROUP = 8
N_EXPERTS = N_GROUPS * E_PER_GROUP
TOP_K = 2
D_EXPERT = D_MODEL // 2
EPS = 1e-6

LANES = 128
ROW_SUB = D_MODEL // LANES
HIST = 16
TM = 256
CTM = 512
ROW_BLK = 256
VMEM_LIMIT = 56 * 1024 * 1024
WEIGHT_DMA_PRIORITY = 1
WEIGHT_DMA_CHUNKS = 8
DMA_QUEUES = 2
GATHER_SHIFT = 3
GATHER_UNROLL = 1 << GATHER_SHIFT
SCALAR_UNROLL = 8

_NEG = -1e30


def _rms(x, g):
    return x * lax.rsqrt(jnp.mean(x * x, axis=-1, keepdims=True) + EPS) * g


def _bdot(a, b):
    return jnp.dot(a, b, preferred_element_type=jnp.float32)


def _store_rows(ref, val):
    ref[...] = val.astype(jnp.bfloat16).reshape(val.shape[0] * ROW_SUB, LANES)


def _load_rows(ref, n):
    return ref[...].reshape(n, D_MODEL)


def _mixer_kernel(x_ref, x_next_ref, h0_ref, g1_ref, w_in_ref, w_pool_ref, pscale_ref, convw_ref, w_out_ref,
                  g2_ref, w_r_ref, b_r_ref,
                  x1_ref, h2_ref, route_ref,
                  ext_u, ext_z, h_buf):
    i = pl.program_id(0)

    @pl.when(i == 0)
    def _():
        ext_u[0:HIST, :] = jnp.zeros((HIST, MIX_A), jnp.float32)
        ext_z[0:HIST, :] = jnp.zeros((HIST, MIX_B), jnp.float32)
        h_buf[...] = h0_ref[...]

    x = x_ref[...]
    h = h_buf[...]

    ext_u[HIST:HIST + TM, :] = _bdot(h, w_in_ref[:, 0:MIX_A])
    c_gate = _bdot(h, w_in_ref[:, MIX_A + MIX_B:MIX_A + 2 * MIX_B])
    v = _bdot(h, w_in_ref[:, MIX_A + 2 * MIX_B:MIX_A + 3 * MIX_B])
    z = c_gate * v
    ext_z[HIST:HIST + TM, :] = z
    b_gate = _bdot(h, w_in_ref[:, MIX_A:MIX_A + MIX_B])

    y = ext_z[HIST - 2:HIST - 2 + TM, :] * convw_ref[0:1, :]
    y = y + ext_z[HIST - 1:HIST - 1 + TM, :] * convw_ref[1:2, :]
    y = y + z * convw_ref[2:3, :]
    y_b = b_gate * y
    ext_z[0:HIST, :] = ext_z[TM:TM + HIST, :]
    out_b = _bdot(y_b.astype(jnp.bfloat16), w_out_ref[MIX_A:MIX_A + MIX_B, :])

    row = lax.broadcasted_iota(jnp.int32, (TM, 1), 0) + i * TM + 1
    y_a = []
    for gi, w in enumerate(POOL_WINDOWS):
        c0 = gi * POOL_CH
        u = ext_u[HIST:HIST + TM, c0:c0 + POOL_CH]
        acc = u
        for s in range(1, w):
            acc = acc + ext_u[HIST - s:HIST - s + TM, c0:c0 + POOL_CH]
        cnt = jnp.minimum(row, w).astype(jnp.float32)
        pooled = (acc / cnt - u).astype(jnp.bfloat16)
        y_a.append(_bdot(pooled, w_pool_ref[gi]))
    y_a = jnp.concatenate(y_a, axis=-1) * pscale_ref[...]
    ext_u[0:HIST, :] = ext_u[TM:TM + HIST, :]
    out_a = _bdot(y_a.astype(jnp.bfloat16), w_out_ref[0:MIX_A, :])

    x1 = x + (out_a + out_b)
    x1_ref[...] = x1

    h2_f32 = _rms(x1, g2_ref[...])
    _store_rows(h2_ref, h2_f32)
    h2 = h2_f32.astype(jnp.bfloat16)

    logits = _bdot(h2, w_r_ref[...]) + b_r_ref[...]
    lane = lax.broadcasted_iota(jnp.int32, (TM, LANES), 1)
    lane_f = lane.astype(jnp.float32)

    def first_argmax(vals, vmax):
        return jnp.min(jnp.where(vals == vmax, lane_f, float(LANES)), axis=-1, keepdims=True)

    gl = jnp.where(lane < N_GROUPS, logits, _NEG)
    gmax = jnp.max(gl, axis=-1, keepdims=True)
    g_w = 1.0 / jnp.sum(jnp.exp(gl - gmax), axis=-1, keepdims=True)
    grp = first_argmax(gl, gmax).astype(jnp.int32)
    lo = N_GROUPS + E_PER_GROUP * grp
    el = jnp.where((lane >= lo) & (lane < lo + E_PER_GROUP), logits, _NEG)
    emax = jnp.max(el, axis=-1, keepdims=True)
    idx1 = first_argmax(el, emax)
    esum = jnp.sum(jnp.exp(el - emax), axis=-1, keepdims=True)
    el2 = jnp.where(lane_f == idx1, _NEG, el)
    e2max = jnp.max(el2, axis=-1, keepdims=True)
    idx2 = first_argmax(el2, e2max)
    p1 = 1.0 / esum
    p2 = jnp.exp(e2max - emax) / esum
    tot = p1 + p2
    w1 = g_w * (p1 / tot)
    w2 = g_w * (p2 / tot)
    route = jnp.where(lane == 0, w1, 0.0)
    route = jnp.where(lane == 1, w2, route)
    route = jnp.where(lane == 2, idx1 - N_GROUPS, route)
    route = jnp.where(lane == 3, idx2 - N_GROUPS, route)
    route_ref[...] = route

    h_buf[...] = _rms(x_next_ref[...], g1_ref[...]).astype(jnp.bfloat16)


def _first_tile_norm_kernel(x_ref, g_ref, o_ref):
    o_ref[...] = _rms(x_ref[...], g_ref[...]).astype(jnp.bfloat16)


def _first_tile_norm(x, g):
    return pl.pallas_call(
        _first_tile_norm_kernel,
        grid=(1,),
        in_specs=[pl.BlockSpec((TM, D_MODEL), lambda i: (0, 0)),
                  pl.BlockSpec((1, D_MODEL), lambda i: (0, 0))],
        out_specs=pl.BlockSpec((TM, D_MODEL), lambda i: (0, 0)),
        out_shape=jax.ShapeDtypeStruct((TM, D_MODEL), jnp.bfloat16),
        name="first_tile_norm",
    )(x, g)


def _mixer(x, g1, w_in, w_pool, pscale, convw, w_out, g2, w_r, b_r):
    t = x.shape[0]
    h0 = _first_tile_norm(x, g1)
    const = lambda shape: pl.BlockSpec(shape, lambda i: (0,) * len(shape),
                                       pipeline_mode=pl.Buffered(1))
    return pl.pallas_call(
        _mixer_kernel,
        grid=(t // TM,),
        in_specs=[
            pl.BlockSpec((TM, D_MODEL), lambda i: (i, 0)),
            pl.BlockSpec((TM, D_MODEL), lambda i: (jnp.minimum(i + 1, t // TM - 1), 0)),
            const((TM, D_MODEL)),
            const((1, D_MODEL)),
            const(w_in.shape),
            const(w_pool.shape),
            const((1, MIX_A)),
            const((CONV_W, MIX_B)),
            const(w_out.shape),
            const((1, D_MODEL)),
            const(w_r.shape),
            const((1, LANES)),
        ],
        out_specs=[
            pl.BlockSpec((TM, D_MODEL), lambda i: (i, 0)),
            pl.BlockSpec((TM * ROW_SUB, LANES), lambda i: (i, 0)),
            pl.BlockSpec((TM, LANES), lambda i: (i, 0)),
        ],
        out_shape=[
            jax.ShapeDtypeStruct((t, D_MODEL), jnp.float32),
            jax.ShapeDtypeStruct((t * ROW_SUB, LANES), jnp.bfloat16),
            jax.ShapeDtypeStruct((t, LANES), jnp.float32),
        ],
        scratch_shapes=[
            pltpu.VMEM((TM + HIST, MIX_A), jnp.float32),
            pltpu.VMEM((TM + HIST, MIX_B), jnp.float32),
            pltpu.VMEM((TM, D_MODEL), jnp.bfloat16),
        ],
        compiler_params=pltpu.CompilerParams(
            dimension_semantics=("arbitrary",), vmem_limit_bytes=VMEM_LIMIT),
        name="mixer",
    )(x, x, h0, g1, w_in, w_pool, pscale, convw, w_out, g2, w_r, b_r)


def _row_copy(src_hbm, row_start, dst_ref, r, sem):
    dst_start = r * ROW_SUB if isinstance(r, int) else pl.multiple_of(r * ROW_SUB, ROW_SUB)
    return pltpu.make_async_copy(src_hbm.at[pl.ds(pl.multiple_of(row_start, ROW_SUB), ROW_SUB)],
                                 dst_ref.at[pl.ds(dst_start, ROW_SUB)], sem)


def _gather_rows_issued(n_rows):
    if isinstance(n_rows, int):
        return n_rows
    return ((n_rows + GATHER_UNROLL - 1) >> GATHER_SHIFT) << GATHER_SHIFT


def _start_row_gather(src_hbm, idx_ref, base, n_rows, dst_ref, sem):
    if isinstance(n_rows, int):
        for r in range(n_rows):
            _row_copy(src_hbm, idx_ref[base + r], dst_ref, r, sem).start(priority=r % DMA_QUEUES)
        return

    def body(i, carry):
        for u in range(GATHER_UNROLL):
            r = i * GATHER_UNROLL + u
            src_row = idx_ref[base + jnp.minimum(r, n_rows - 1)]
            _row_copy(src_hbm, src_row, dst_ref, r, sem).start(priority=u % DMA_QUEUES)
        return carry

    lax.fori_loop(0, _gather_rows_issued(n_rows) >> GATHER_SHIFT, body, 0)


def _wait_row_gather(src_hbm, dst_ref, sem, n_rows):
    n = _gather_rows_issued(n_rows) * ROW_SUB
    pltpu.make_async_copy(src_hbm.at[pl.ds(0, n)], dst_ref.at[pl.ds(0, n)], sem).wait()


def _weight_copies(w_hbm, e, buf, slot, k, sem):
    rows = w_hbm.shape[1] // WEIGHT_DMA_CHUNKS
    return [pltpu.make_async_copy(w_hbm.at[e, pl.ds(c * rows, rows)],
                                  buf.at[slot, k, pl.ds(c * rows, rows)], sem.at[slot])
            for c in range(WEIGHT_DMA_CHUNKS)]


def _start_weight(*args):
    for cp in _weight_copies(*args):
        cp.start(priority=WEIGHT_DMA_PRIORITY)


def _wait_weight(*args):
    for cp in _weight_copies(*args):
        cp.wait()


def _build_row_sources(dest_ref, row_ref):
    n_tokens = dest_ref.shape[0] // TOP_K

    def scatter(i, carry):
        for u in range(SCALAR_UNROLL):
            tok = i * SCALAR_UNROLL + u
            for k in range(TOP_K):
                row_ref[dest_ref[k * n_tokens + tok]] = tok * ROW_SUB
        return carry

    lax.fori_loop(0, n_tokens // SCALAR_UNROLL, scatter, 0)


def _gate_up_kernel(blk_e_ref, first_ref, wslot_ref, nxt_e_ref, n_used_ref, n_valid_ref, dest_ref,
                    h2_hbm, wg_hbm, wu_hbm, hid_ref,
                    xg, wbuf, wg_bf, wu_bf, row_ref, gsem, wsem):
    b = pl.program_id(0)
    n_used = n_used_ref[0]
    e = blk_e_ref[b]
    slot = wslot_ref[b]

    @pl.when(b == 0)
    def _():
        _start_weight(wg_hbm, e, wbuf, slot, 0, wsem)
        _start_weight(wu_hbm, e, wbuf, slot, 1, wsem)
        xg[...] = jnp.zeros_like(xg)
        _build_row_sources(dest_ref, row_ref)
        _start_row_gather(h2_hbm, row_ref, 0, n_valid_ref[0], xg.at[0], gsem.at[0])

    @pl.when(b < n_used)
    def _():
        cur = b % 2

        @pl.when(b + 1 < n_used)
        def _():
            _start_row_gather(h2_hbm, row_ref, (b + 1) * ROW_BLK, n_valid_ref[b + 1],
                              xg.at[1 - cur], gsem.at[1 - cur])

        @pl.when(first_ref[b] == 1)
        def _():
            _wait_weight(wg_hbm, e, wbuf, slot, 0, wsem)
            _wait_weight(wu_hbm, e, wbuf, slot, 1, wsem)
            nxt = nxt_e_ref[b]

            @pl.when(nxt != e)
            def _():
                _start_weight(wg_hbm, nxt, wbuf, 1 - slot, 0, wsem)
                _start_weight(wu_hbm, nxt, wbuf, 1 - slot, 1, wsem)

            wg_bf[...] = wbuf[slot, 0].astype(jnp.bfloat16)
            wu_bf[...] = wbuf[slot, 1].astype(jnp.bfloat16)

        _wait_row_gather(h2_hbm, xg.at[cur], gsem.at[cur], n_valid_ref[b])
        xb = _load_rows(xg.at[cur], ROW_BLK)
        gate = _bdot(xb, wg_bf[...])
        up = _bdot(xb, wu_bf[...])
        hid_ref[...] = (gate * jax.nn.sigmoid(gate) * up).astype(jnp.bfloat16)

    @pl.when(b >= n_used)
    def _():
        hid_ref[...] = jnp.zeros_like(hid_ref)


def _gate_up(plan, h2, wg, wu):
    n_blocks = plan["blk_e"].shape[0]
    any_spec = pl.BlockSpec(memory_space=pl.ANY)
    return pl.pallas_call(
        _gate_up_kernel,
        grid_spec=pltpu.PrefetchScalarGridSpec(
            num_scalar_prefetch=7,
            grid=(n_blocks,),
            in_specs=[any_spec, any_spec, any_spec],
            out_specs=pl.BlockSpec((ROW_BLK, D_EXPERT), lambda b, *_: (b, 0)),
            scratch_shapes=[
                pltpu.VMEM((2, ROW_BLK * ROW_SUB, LANES), jnp.bfloat16),
                pltpu.VMEM((2, 2, D_MODEL, D_EXPERT), jnp.float32),
                pltpu.VMEM((D_MODEL, D_EXPERT), jnp.bfloat16),
                pltpu.VMEM((D_MODEL, D_EXPERT), jnp.bfloat16),
                pltpu.SMEM((n_blocks * ROW_BLK,), jnp.int32),
                pltpu.SemaphoreType.DMA((2,)),
                pltpu.SemaphoreType.DMA((2,)),
            ],
        ),
        out_shape=jax.ShapeDtypeStruct((n_blocks * ROW_BLK, D_EXPERT), jnp.bfloat16),
        compiler_params=pltpu.CompilerParams(
            dimension_semantics=("arbitrary",), vmem_limit_bytes=VMEM_LIMIT),
        name="gate_up",
    )(plan["blk_e"], plan["first"], plan["wslot"], plan["nxt_e"], plan["n_used"], plan["n_valid"],
      plan["dest"], h2, wg, wu)


def _down_kernel(blk_e_ref, first_ref, wslot_ref, nxt_e_ref, n_used_ref,
                 hid_ref, wd_hbm, y_ref, wbuf, wd_bf, wsem):
    b = pl.program_id(0)
    n_used = n_used_ref[0]
    e = blk_e_ref[b]
    slot = wslot_ref[b]

    @pl.when(b == 0)
    def _():
        _start_weight(wd_hbm, e, wbuf, slot, 0, wsem)

    @pl.when(b < n_used)
    def _():
        @pl.when(first_ref[b] == 1)
        def _():
            _wait_weight(wd_hbm, e, wbuf, slot, 0, wsem)
            nxt = nxt_e_ref[b]

            @pl.when(nxt != e)
            def _():
                _start_weight(wd_hbm, nxt, wbuf, 1 - slot, 0, wsem)

            wd_bf[...] = wbuf[slot, 0].astype(jnp.bfloat16)

        _store_rows(y_ref, _bdot(hid_ref[...], wd_bf[...]))

    @pl.when(b >= n_used)
    def _():
        y_ref[...] = jnp.zeros_like(y_ref)


def _down(plan, hid, wd):
    n_blocks = plan["blk_e"].shape[0]
    return pl.pallas_call(
        _down_kernel,
        grid_spec=pltpu.PrefetchScalarGridSpec(
            num_scalar_prefetch=5,
            grid=(n_blocks,),
            in_specs=[pl.BlockSpec((ROW_BLK, D_EXPERT), lambda b, *_: (b, 0)),
                      pl.BlockSpec(memory_space=pl.ANY)],
            out_specs=pl.BlockSpec((ROW_BLK * ROW_SUB, LANES), lambda b, *_: (b, 0)),
            scratch_shapes=[
                pltpu.VMEM((2, 1, D_EXPERT, D_MODEL), jnp.float32),
                pltpu.VMEM((D_EXPERT, D_MODEL), jnp.bfloat16),
                pltpu.SemaphoreType.DMA((2,)),
            ],
        ),
        out_shape=jax.ShapeDtypeStruct((n_blocks * ROW_BLK * ROW_SUB, LANES), jnp.bfloat16),
        compiler_params=pltpu.CompilerParams(
            dimension_semantics=("arbitrary",), vmem_limit_bytes=VMEM_LIMIT),
        name="down",
    )(plan["blk_e"], plan["first"], plan["wslot"], plan["nxt_e"], plan["n_used"], hid, wd)


def _combine_kernel(dest_ref, x1_ref, route_ref, g_ref, y_hbm, o_ref, y0_buf, y1_buf, sem):
    i = pl.program_id(0)
    n = pl.num_programs(0)
    cur = i % 2
    n_tokens = n * CTM

    def start(step, slot):
        _start_row_gather(y_hbm, dest_ref, step * CTM, CTM, y0_buf.at[slot], sem.at[0, slot])
        _start_row_gather(y_hbm, dest_ref, n_tokens + step * CTM, CTM, y1_buf.at[slot], sem.at[1, slot])

    @pl.when(i == 0)
    def _():
        start(0, 0)

    @pl.when(i + 1 < n)
    def _():
        start(i + 1, 1 - cur)

    r = route_ref[...]
    _wait_row_gather(y_hbm, y0_buf.at[cur], sem.at[0, cur], CTM)
    _wait_row_gather(y_hbm, y1_buf.at[cur], sem.at[1, cur], CTM)
    y0 = _load_rows(y0_buf.at[cur], CTM).astype(jnp.float32)
    y1 = _load_rows(y1_buf.at[cur], CTM).astype(jnp.float32)
    xo = x1_ref[...] + (y0 * r[:, 0:1] + y1 * r[:, 1:2])
    o_ref[...] = _rms(xo, g_ref[...])


def _combine(dest_rows, x1, route, g, ybuf):
    t = x1.shape[0]
    tile = lambda w: pl.BlockSpec((CTM, w), lambda i, *_: (i, 0))
    return pl.pallas_call(
        _combine_kernel,
        grid_spec=pltpu.PrefetchScalarGridSpec(
            num_scalar_prefetch=1,
            grid=(t // CTM,),
            in_specs=[tile(D_MODEL), tile(LANES),
                      pl.BlockSpec((1, D_MODEL), lambda i, *_: (0, 0)),
                      pl.BlockSpec(memory_space=pl.ANY)],
            out_specs=tile(D_MODEL),
            scratch_shapes=[pltpu.VMEM((2, CTM * ROW_SUB, LANES), jnp.bfloat16),
                            pltpu.VMEM((2, CTM * ROW_SUB, LANES), jnp.bfloat16),
                            pltpu.SemaphoreType.DMA((2, 2))],
        ),
        out_shape=jax.ShapeDtypeStruct((t, D_MODEL), jnp.float32),
        compiler_params=pltpu.CompilerParams(
            dimension_semantics=("arbitrary",), vmem_limit_bytes=VMEM_LIMIT),
        name="combine",
    )(dest_rows, x1, route, g, ybuf)


def _dispatch_plan(e_flat):
    a = e_flat.shape[0]
    i32 = jnp.int32
    experts = jnp.arange(N_EXPERTS, dtype=i32)
    onehot = (e_flat[:, None] == experts[None, :]).astype(i32)
    rank = jnp.cumsum(onehot, axis=0) - onehot
    counts = jnp.sum(onehot, axis=0)
    padded = ((counts + ROW_BLK - 1) // ROW_BLK) * ROW_BLK
    pends = jnp.cumsum(padded)
    pstarts = pends - padded
    dest = jnp.sum((rank + pstarts[None, :]) * onehot, axis=1).astype(i32)
    n_blocks = (a + N_EXPERTS * (ROW_BLK - 1) + ROW_BLK - 1) // ROW_BLK
    blk_start = jnp.arange(n_blocks, dtype=i32) * ROW_BLK
    blk_e = jnp.clip(jnp.sum(pends[None, :] <= blk_start[:, None], axis=1), 0, N_EXPERTS - 1).astype(i32)
    first = jnp.concatenate([jnp.ones((1,), i32), (blk_e[1:] != blk_e[:-1]).astype(i32)])
    wslot = (jnp.cumsum(first) - 1) % 2
    later_used = (experts[None, :] > experts[:, None]) & (counts[None, :] > 0)
    nxt = jnp.min(jnp.where(later_used, experts[None, :], N_EXPERTS), axis=1)
    nxt = jnp.where(nxt == N_EXPERTS, experts, nxt)
    return {
        "dest": dest,
        "blk_e": blk_e,
        "first": first,
        "wslot": wslot.astype(i32),
        "nxt_e": nxt[blk_e].astype(i32),
        "n_used": (pends[-1] // ROW_BLK).astype(i32).reshape(1),
        "n_valid": jnp.clip(counts[blk_e] - (blk_start - pstarts[blk_e]), 0, ROW_BLK).astype(i32),
    }


def kernel(x, norm_mix_g, w_in, w_pool, pool_scale, conv_w, w_out, norm_ffn_g, w_router_group, b_router_group, w_router_expert, b_router_expert, w_gate, w_up, w_down, norm_final_g):
    bt, s, d = x.shape
    t = bt * s
    bf = jnp.bfloat16
    xt = x.reshape(t, d)
    assert norm_mix_g.shape[0] == 1, "the final norm is fused into the last layer's combine"
    for l in range(norm_mix_g.shape[0]):
        w_r = jnp.concatenate([w_router_group[l], w_router_expert[l]], axis=1)
        w_r = jnp.pad(w_r, ((0, 0), (0, LANES - w_r.shape[1]))).astype(bf)
        b_r = jnp.concatenate([b_router_group[l], b_router_expert[l]])
        b_r = jnp.pad(b_r, (0, LANES - b_r.shape[0])).reshape(1, LANES)
        x1, h2, route = _mixer(
            xt, norm_mix_g[l].reshape(1, d), w_in[l].astype(bf), w_pool[l].astype(bf),
            pool_scale[l].reshape(1, MIX_A), conv_w[l].T, w_out[l].astype(bf),
            norm_ffn_g[l].reshape(1, d), w_r, b_r)
        e_flat = jnp.concatenate([route[:, 2 + k] for k in range(TOP_K)]).astype(jnp.int32)
        plan = _dispatch_plan(e_flat)
        hid = _gate_up(plan, h2, w_gate[l], w_up[l])
        ybuf = _down(plan, hid, w_down[l])
        xt = _combine(plan["dest"] * ROW_SUB, x1, route, norm_final_g.reshape(1, d), ybuf)
    return xt.reshape(bt, s, d)
```

```python
import jax
import jax.numpy as jnp
from jax import lax
from jax.experimental import pallas as pl
from jax.experimental.pallas import tpu as pltpu

D_MODEL = 2048
MIX_A = 1024
MIX_B = 1024
POOL_WINDOWS = (2, 4, 8, 16)
POOL_CH = MIX_A // len(POOL_WINDOWS)
CONV_W = 3
N_GROUPS = 4
E_PER_GROUP = 8
N_EXPERTS = N_GROUPS * E_PER_GROUP
TOP_K = 2
D_EXPERT = D_MODEL // 2
EPS = 1e-6

LANES = 128
ROW_SUB = D_MODEL // LANES
HIST = 16
TM = 256
CTM = 256
ROW_BLK = 256
VMEM_LIMIT = 56 * 1024 * 1024
WEIGHT_DMA_PRIORITY = 1
WEIGHT_DMA_CHUNKS = 8
DMA_QUEUES = 2
GATHER_SHIFT = 3
GATHER_UNROLL = 1 << GATHER_SHIFT
SCALAR_UNROLL = 16

_NEG = -1e30


def _rms(x, g):
    return x * lax.rsqrt(jnp.mean(x * x, axis=-1, keepdims=True) + EPS) * g


def _bdot(a, b):
    return jnp.dot(a, b, preferred_element_type=jnp.float32)


def _store_rows(ref, val):
    ref[...] = val.astype(jnp.bfloat16).reshape(val.shape[0] * ROW_SUB, LANES)


def _load_rows(ref, n):
    return ref[...].reshape(n, D_MODEL)


def _mixer_kernel(x_ref, x_next_ref, h0_ref, g1_ref, w_in_ref, w_pool_ref, pscale_ref, convw_ref, w_out_ref,
                  g2_ref, w_r_ref, b_r_ref,
                  x1_ref, h2_ref, route_ref,
                  ext_u, ext_z, h_buf):
    i = pl.program_id(0)

    @pl.when(i == 0)
    def _():
        ext_u[0:HIST, :] = jnp.zeros((HIST, MIX_A), jnp.float32)
        ext_z[0:HIST, :] = jnp.zeros((HIST, MIX_B), jnp.float32)
        h_buf[...] = h0_ref[...]

    x = x_ref[...]
    h = h_buf[...]

    ext_u[HIST:HIST + TM, :] = _bdot(h, w_in_ref[:, 0:MIX_A])
    c_gate = _bdot(h, w_in_ref[:, MIX_A + MIX_B:MIX_A + 2 * MIX_B])
    v = _bdot(h, w_in_ref[:, MIX_A + 2 * MIX_B:MIX_A + 3 * MIX_B])
    z = c_gate * v
    ext_z[HIST:HIST + TM, :] = z
    b_gate = _bdot(h, w_in_ref[:, MIX_A:MIX_A + MIX_B])

    y = ext_z[HIST - 2:HIST - 2 + TM, :] * convw_ref[0:1, :]
    y = y + ext_z[HIST - 1:HIST - 1 + TM, :] * convw_ref[1:2, :]
    y = y + z * convw_ref[2:3, :]
    y_b = b_gate * y
    ext_z[0:HIST, :] = ext_z[TM:TM + HIST, :]
    out_b = _bdot(y_b.astype(jnp.bfloat16), w_out_ref[MIX_A:MIX_A + MIX_B, :])

    row = lax.broadcasted_iota(jnp.int32, (TM, 1), 0) + i * TM + 1
    y_a = []
    for gi, w in enumerate(POOL_WINDOWS):
        c0 = gi * POOL_CH
        u = ext_u[HIST:HIST + TM, c0:c0 + POOL_CH]
        acc = u
        for s in range(1, w):
            acc = acc + ext_u[HIST - s:HIST - s + TM, c0:c0 + POOL_CH]
        cnt = jnp.minimum(row, w).astype(jnp.float32)
        pooled = (acc / cnt - u).astype(jnp.bfloat16)
        y_a.append(_bdot(pooled, w_pool_ref[gi]))
    y_a = jnp.concatenate(y_a, axis=-1) * pscale_ref[...]
    ext_u[0:HIST, :] = ext_u[TM:TM + HIST, :]
    out_a = _bdot(y_a.astype(jnp.bfloat16), w_out_ref[0:MIX_A, :])

    x1 = x + (out_a + out_b)
    x1_ref[...] = x1

    h2_f32 = _rms(x1, g2_ref[...])
    _store_rows(h2_ref, h2_f32)
    h2 = h2_f32.astype(jnp.bfloat16)

    logits = _bdot(h2, w_r_ref[...]) + b_r_ref[...]
    lane = lax.broadcasted_iota(jnp.int32, (TM, LANES), 1)
    lane_f = lane.astype(jnp.float32)

    def first_argmax(vals, vmax):
        return jnp.min(jnp.where(vals == vmax, lane_f, float(LANES)), axis=-1, keepdims=True)

    gl = jnp.where(lane < N_GROUPS, logits, _NEG)
    gmax = jnp.max(gl, axis=-1, keepdims=True)
    g_w = 1.0 / jnp.sum(jnp.exp(gl - gmax), axis=-1, keepdims=True)
    grp = first_argmax(gl, gmax).astype(jnp.int32)
    lo = N_GROUPS + E_PER_GROUP * grp
    el = jnp.where((lane >= lo) & (lane < lo + E_PER_GROUP), logits, _NEG)
    emax = jnp.max(el, axis=-1, keepdims=True)
    idx1 = first_argmax(el, emax)
    esum = jnp.sum(jnp.exp(el - emax), axis=-1, keepdims=True)
    el2 = jnp.where(lane_f == idx1, _NEG, el)
    e2max = jnp.max(el2, axis=-1, keepdims=True)
    idx2 = first_argmax(el2, e2max)
    p1 = 1.0 / esum
    p2 = jnp.exp(e2max - emax) / esum
    tot = p1 + p2
    w1 = g_w * (p1 / tot)
    w2 = g_w * (p2 / tot)
    route = jnp.where(lane == 0, w1, 0.0)
    route = jnp.where(lane == 1, w2, route)
    route = jnp.where(lane == 2, idx1 - N_GROUPS, route)
    route = jnp.where(lane == 3, idx2 - N_GROUPS, route)
    route_ref[...] = route

    h_buf[...] = _rms(x_next_ref[...], g1_ref[...]).astype(jnp.bfloat16)


def _first_tile_norm_kernel(x_ref, g_ref, o_ref):
    o_ref[...] = _rms(x_ref[...], g_ref[...]).astype(jnp.bfloat16)


def _first_tile_norm(x, g):
    return pl.pallas_call(
        _first_tile_norm_kernel,
        grid=(1,),
        in_specs=[pl.BlockSpec((TM, D_MODEL), lambda i: (0, 0)),
                  pl.BlockSpec((1, D_MODEL), lambda i: (0, 0))],
        out_specs=pl.BlockSpec((TM, D_MODEL), lambda i: (0, 0)),
        out_shape=jax.ShapeDtypeStruct((TM, D_MODEL), jnp.bfloat16),
        name="first_tile_norm",
    )(x, g)


def _mixer(x, g1, w_in, w_pool, pscale, convw, w_out, g2, w_r, b_r):
    t = x.shape[0]
    h0 = _first_tile_norm(x, g1)
    const = lambda shape: pl.BlockSpec(shape, lambda i: (0,) * len(shape),
                                       pipeline_mode=pl.Buffered(1))
    return pl.pallas_call(
        _mixer_kernel,
        grid=(t // TM,),
        in_specs=[
            pl.BlockSpec((TM, D_MODEL), lambda i: (i, 0)),
            pl.BlockSpec((TM, D_MODEL), lambda i: (jnp.minimum(i + 1, t // TM - 1), 0)),
            const((TM, D_MODEL)),
            const((1, D_MODEL)),
            const(w_in.shape),
            const(w_pool.shape),
            const((1, MIX_A)),
            const((CONV_W, MIX_B)),
            const(w_out.shape),
            const((1, D_MODEL)),
            const(w_r.shape),
            const((1, LANES)),
        ],
        out_specs=[
            pl.BlockSpec((TM, D_MODEL), lambda i: (i, 0)),
            pl.BlockSpec((TM * ROW_SUB, LANES), lambda i: (i, 0)),
            pl.BlockSpec((TM, LANES), lambda i: (i, 0)),
        ],
        out_shape=[
            jax.ShapeDtypeStruct((t, D_MODEL), jnp.float32),
            jax.ShapeDtypeStruct((t * ROW_SUB, LANES), jnp.bfloat16),
            jax.ShapeDtypeStruct((t, LANES), jnp.float32),
        ],
        scratch_shapes=[
            pltpu.VMEM((TM + HIST, MIX_A), jnp.float32),
            pltpu.VMEM((TM + HIST, MIX_B), jnp.float32),
            pltpu.VMEM((TM, D_MODEL), jnp.bfloat16),
        ],
        compiler_params=pltpu.CompilerParams(
            dimension_semantics=("arbitrary",), vmem_limit_bytes=VMEM_LIMIT),
        name="mixer",
    )(x, x, h0, g1, w_in, w_pool, pscale, convw, w_out, g2, w_r, b_r)


def _row_copy(src_hbm, row_start, dst_ref, r, sem):
    dst_start = r * ROW_SUB if isinstance(r, int) else pl.multiple_of(r * ROW_SUB, ROW_SUB)
    return pltpu.make_async_copy(src_hbm.at[pl.ds(pl.multiple_of(row_start, ROW_SUB), ROW_SUB)],
                                 dst_ref.at[pl.ds(dst_start, ROW_SUB)], sem)


def _gather_rows_issued(n_rows):
    if isinstance(n_rows, int):
        return n_rows
    return ((n_rows + GATHER_UNROLL - 1) >> GATHER_SHIFT) << GATHER_SHIFT


def _start_row_gather(src_hbm, idx_ref, base, n_rows, dst_ref, sem):
    if isinstance(n_rows, int):
        for r in range(n_rows):
            _row_copy(src_hbm, idx_ref[base + r], dst_ref, r, sem).start(priority=r % DMA_QUEUES)
        return

    def body(i, carry):
        for u in range(GATHER_UNROLL):
            r = i * GATHER_UNROLL + u
            src_row = idx_ref[base + jnp.minimum(r, n_rows - 1)]
            _row_copy(src_hbm, src_row, dst_ref, r, sem).start(priority=u % DMA_QUEUES)
        return carry

    lax.fori_loop(0, _gather_rows_issued(n_rows) >> GATHER_SHIFT, body, 0)


def _wait_row_gather(src_hbm, dst_ref, sem, n_rows):
    n = _gather_rows_issued(n_rows) * ROW_SUB
    pltpu.make_async_copy(src_hbm.at[pl.ds(0, n)], dst_ref.at[pl.ds(0, n)], sem).wait()


def _weight_copies(w_hbm, e, buf, slot, k, sem):
    rows = w_hbm.shape[1] // WEIGHT_DMA_CHUNKS
    return [pltpu.make_async_copy(w_hbm.at[e, pl.ds(c * rows, rows)],
                                  buf.at[slot, k, pl.ds(c * rows, rows)], sem.at[slot])
            for c in range(WEIGHT_DMA_CHUNKS)]


def _start_weight(*args):
    for cp in _weight_copies(*args):
        cp.start(priority=WEIGHT_DMA_PRIORITY)


def _wait_weight(*args):
    for cp in _weight_copies(*args):
        cp.wait()


def _build_row_sources(dest_ref, row_ref):
    n_tokens = dest_ref.shape[0] // TOP_K

    def scatter(i, carry):
        for u in range(SCALAR_UNROLL):
            tok = i * SCALAR_UNROLL + u
            for k in range(TOP_K):
                row_ref[dest_ref[k * n_tokens + tok]] = tok * ROW_SUB
        return carry

    lax.fori_loop(0, n_tokens // SCALAR_UNROLL, scatter, 0)


def _gate_up_kernel(blk_e_ref, first_ref, wslot_ref, nxt_e_ref, n_used_ref, n_valid_ref, dest_ref,
                    h2_hbm, wg_hbm, wu_hbm, hid_ref,
                    xg, wbuf, wg_bf, wu_bf, row_ref, gsem, wsem):
    b = pl.program_id(0)
    n_used = n_used_ref[0]
    e = blk_e_ref[b]
    slot = wslot_ref[b]

    @pl.when(b == 0)
    def _():
        _start_weight(wg_hbm, e, wbuf, slot, 0, wsem)
        _start_weight(wu_hbm, e, wbuf, slot, 1, wsem)
        xg[...] = jnp.zeros_like(xg)
        _build_row_sources(dest_ref, row_ref)
        _start_row_gather(h2_hbm, row_ref, 0, n_valid_ref[0], xg.at[0], gsem.at[0])

    @pl.when(b < n_used)
    def _():
        cur = b % 2

        @pl.when(b + 1 < n_used)
        def _():
            _start_row_gather(h2_hbm, row_ref, (b + 1) * ROW_BLK, n_valid_ref[b + 1],
                              xg.at[1 - cur], gsem.at[1 - cur])

        @pl.when(first_ref[b] == 1)
        def _():
            _wait_weight(wg_hbm, e, wbuf, slot, 0, wsem)
            _wait_weight(wu_hbm, e, wbuf, slot, 1, wsem)
            nxt = nxt_e_ref[b]

            @pl.when(nxt != e)
            def _():
                _start_weight(wg_hbm, nxt, wbuf, 1 - slot, 0, wsem)
                _start_weight(wu_hbm, nxt, wbuf, 1 - slot, 1, wsem)

            wg_bf[...] = wbuf[slot, 0].astype(jnp.bfloat16)
            wu_bf[...] = wbuf[slot, 1].astype(jnp.bfloat16)

        _wait_row_gather(h2_hbm, xg.at[cur], gsem.at[cur], n_valid_ref[b])
        xb = _load_rows(xg.at[cur], ROW_BLK)
        gate = _bdot(xb, wg_bf[...])
        up = _bdot(xb, wu_bf[...])
        hid_ref[...] = (gate * jax.nn.sigmoid(gate) * up).astype(jnp.bfloat16)

    @pl.when(b >= n_used)
    def _():
        hid_ref[...] = jnp.zeros_like(hid_ref)


def _gate_up(plan, h2, wg, wu):
    n_blocks = plan["blk_e"].shape[0]
    any_spec = pl.BlockSpec(memory_space=pl.ANY)
    return pl.pallas_call(
        _gate_up_kernel,
        grid_spec=pltpu.PrefetchScalarGridSpec(
            num_scalar_prefetch=7,
            grid=(n_blocks,),
            in_specs=[any_spec, any_spec, any_spec],
            out_specs=pl.BlockSpec((ROW_BLK, D_EXPERT), lambda b, *_: (b, 0)),
            scratch_shapes=[
                pltpu.VMEM((2, ROW_BLK * ROW_SUB, LANES), jnp.bfloat16),
                pltpu.VMEM((2, 2, D_MODEL, D_EXPERT), jnp.float32),
                pltpu.VMEM((D_MODEL, D_EXPERT), jnp.bfloat16),
                pltpu.VMEM((D_MODEL, D_EXPERT), jnp.bfloat16),
                pltpu.SMEM((n_blocks * ROW_BLK,), jnp.int32),
                pltpu.SemaphoreType.DMA((2,)),
                pltpu.SemaphoreType.DMA((2,)),
            ],
        ),
        out_shape=jax.ShapeDtypeStruct((n_blocks * ROW_BLK, D_EXPERT), jnp.bfloat16),
        compiler_params=pltpu.CompilerParams(
            dimension_semantics=("arbitrary",), vmem_limit_bytes=VMEM_LIMIT),
        name="gate_up",
    )(plan["blk_e"], plan["first"], plan["wslot"], plan["nxt_e"], plan["n_used"], plan["n_valid"],
      plan["dest"], h2, wg, wu)


def _down_kernel(blk_e_ref, first_ref, wslot_ref, nxt_e_ref, n_used_ref,
                 hid_ref, wd_hbm, y_ref, wbuf, wd_bf, wsem):
    b = pl.program_id(0)
    n_used = n_used_ref[0]
    e = blk_e_ref[b]
    slot = wslot_ref[b]

    @pl.when(b == 0)
    def _():
        _start_weight(wd_hbm, e, wbuf, slot, 0, wsem)

    @pl.when(b < n_used)
    def _():
        @pl.when(first_ref[b] == 1)
        def _():
            _wait_weight(wd_hbm, e, wbuf, slot, 0, wsem)
            nxt = nxt_e_ref[b]

            @pl.when(nxt != e)
            def _():
                _start_weight(wd_hbm, nxt, wbuf, 1 - slot, 0, wsem)

            wd_bf[...] = wbuf[slot, 0].astype(jnp.bfloat16)

        _store_rows(y_ref, _bdot(hid_ref[...], wd_bf[...]))

    @pl.when(b >= n_used)
    def _():
        y_ref[...] = jnp.zeros_like(y_ref)


def _down(plan, hid, wd):
    n_blocks = plan["blk_e"].shape[0]
    return pl.pallas_call(
        _down_kernel,
        grid_spec=pltpu.PrefetchScalarGridSpec(
            num_scalar_prefetch=5,
            grid=(n_blocks,),
            in_specs=[pl.BlockSpec((ROW_BLK, D_EXPERT), lambda b, *_: (b, 0)),
                      pl.BlockSpec(memory_space=pl.ANY)],
            out_specs=pl.BlockSpec((ROW_BLK * ROW_SUB, LANES), lambda b, *_: (b, 0)),
            scratch_shapes=[
                pltpu.VMEM((2, 1, D_EXPERT, D_MODEL), jnp.float32),
                pltpu.VMEM((D_EXPERT, D_MODEL), jnp.bfloat16),
                pltpu.SemaphoreType.DMA((2,)),
            ],
        ),
        out_shape=jax.ShapeDtypeStruct((n_blocks * ROW_BLK * ROW_SUB, LANES), jnp.bfloat16),
        compiler_params=pltpu.CompilerParams(
            dimension_semantics=("arbitrary",), vmem_limit_bytes=VMEM_LIMIT),
        name="down",
    )(plan["blk_e"], plan["first"], plan["wslot"], plan["nxt_e"], plan["n_used"], hid, wd)


def _combine_kernel(dest_ref, x1_ref, route_ref, g_ref, y_hbm, o_ref, y0_buf, y1_buf, sem):
    i = pl.program_id(0)
    n = pl.num_programs(0)
    cur = i % 2
    n_tokens = n * CTM

    def start(step, slot):
        _start_row_gather(y_hbm, dest_ref, step * CTM, CTM, y0_buf.at[slot], sem.at[0, slot])
        _start_row_gather(y_hbm, dest_ref, n_tokens + step * CTM, CTM, y1_buf.at[slot], sem.at[1, slot])

    @pl.when(i == 0)
    def _():
        start(0, 0)

    @pl.when(i + 1 < n)
    def _():
        start(i + 1, 1 - cur)

    r = route_ref[...]
    _wait_row_gather(y_hbm, y0_buf.at[cur], sem.at[0, cur], CTM)
    _wait_row_gather(y_hbm, y1_buf.at[cur], sem.at[1, cur], CTM)
    y0 = _load_rows(y0_buf.at[cur], CTM).astype(jnp.float32)
    y1 = _load_rows(y1_buf.at[cur], CTM).astype(jnp.float32)
    xo = x1_ref[...] + (y0 * r[:, 0:1] + y1 * r[:, 1:2])
    o_ref[...] = _rms(xo, g_ref[...])


def _combine(dest_rows, x1, route, g, ybuf):
    t = x1.shape[0]
    tile = lambda w: pl.BlockSpec((CTM, w), lambda i, *_: (i, 0))
    return pl.pallas_call(
        _combine_kernel,
        grid_spec=pltpu.PrefetchScalarGridSpec(
            num_scalar_prefetch=1,
            grid=(t // CTM,),
            in_specs=[tile(D_MODEL), tile(LANES),
                      pl.BlockSpec((1, D_MODEL), lambda i, *_: (0, 0)),
                      pl.BlockSpec(memory_space=pl.ANY)],
            out_specs=tile(D_MODEL),
            scratch_shapes=[pltpu.VMEM((2, CTM * ROW_SUB, LANES), jnp.bfloat16),
                            pltpu.VMEM((2, CTM * ROW_SUB, LANES), jnp.bfloat16),
                            pltpu.SemaphoreType.DMA((2, 2))],
        ),
        out_shape=jax.ShapeDtypeStruct((t, D_MODEL), jnp.float32),
        compiler_params=pltpu.CompilerParams(
            dimension_semantics=("arbitrary",), vmem_limit_bytes=VMEM_LIMIT),
        name="combine",
    )(dest_rows, x1, route, g, ybuf)


def _dispatch_plan(e_flat):
    a = e_flat.shape[0]
    i32 = jnp.int32
    experts = jnp.arange(N_EXPERTS, dtype=i32)
    onehot = (e_flat[:, None] == experts[None, :]).astype(i32)
    rank = jnp.cumsum(onehot, axis=0) - onehot
    counts = jnp.sum(onehot, axis=0)
    padded = ((counts + ROW_BLK - 1) // ROW_BLK) * ROW_BLK
    pends = jnp.cumsum(padded)
    pstarts = pends - padded
    dest = jnp.sum((rank + pstarts[None, :]) * onehot, axis=1).astype(i32)
    n_blocks = (a + N_EXPERTS * (ROW_BLK - 1) + ROW_BLK - 1) // ROW_BLK
    blk_start = jnp.arange(n_blocks, dtype=i32) * ROW_BLK
    blk_e = jnp.clip(jnp.sum(pends[None, :] <= blk_start[:, None], axis=1), 0, N_EXPERTS - 1).astype(i32)
    first = jnp.concatenate([jnp.ones((1,), i32), (blk_e[1:] != blk_e[:-1]).astype(i32)])
    wslot = (jnp.cumsum(first) - 1) % 2
    later_used = (experts[None, :] > experts[:, None]) & (counts[None, :] > 0)
    nxt = jnp.min(jnp.where(later_used, experts[None, :], N_EXPERTS), axis=1)
    nxt = jnp.where(nxt == N_EXPERTS, experts, nxt)
    return {
        "dest": dest,
        "blk_e": blk_e,
        "first": first,
        "wslot": wslot.astype(i32),
        "nxt_e": nxt[blk_e].astype(i32),
        "n_used": (pends[-1] // ROW_BLK).astype(i32).reshape(1),
        "n_valid": jnp.clip(counts[blk_e] - (blk_start - pstarts[blk_e]), 0, ROW_BLK).astype(i32),
    }


def kernel(x, norm_mix_g, w_in, w_pool, pool_scale, conv_w, w_out, norm_ffn_g, w_router_group, b_router_group, w_router_expert, b_router_expert, w_gate, w_up, w_down, norm_final_g):
    bt, s, d = x.shape
    t = bt * s
    bf = jnp.bfloat16
    xt = x.reshape(t, d)
    assert norm_mix_g.shape[0] == 1, "the final norm is fused into the last layer's combine"
    for l in range(norm_mix_g.shape[0]):
        w_r = jnp.concatenate([w_router_group[l], w_router_expert[l]], axis=1)
        w_r = jnp.pad(w_r, ((0, 0), (0, LANES - w_r.shape[1]))).astype(bf)
        b_r = jnp.concatenate([b_router_group[l], b_router_expert[l]])
        b_r = jnp.pad(b_r, (0, LANES - b_r.shape[0])).reshape(1, LANES)
        x1, h2, route = _mixer(
            xt, norm_mix_g[l].reshape(1, d), w_in[l].astype(bf), w_pool[l].astype(bf),
            pool_scale[l].reshape(1, MIX_A), conv_w[l].T, w_out[l].astype(bf),
            norm_ffn_g[l].reshape(1, d), w_r, b_r)
        e_flat = jnp.concatenate([route[:, 2 + k] for k in range(TOP_K)]).astype(jnp.int32)
        plan = _dispatch_plan(e_flat)
        hid = _gate_up(plan, h2, w_gate[l], w_up[l])
        ybuf = _down(plan, hid, w_down[l])
        xt = _combine(plan["dest"] * ROW_SUB, x1, route, norm_final_g.reshape(1, d), ybuf)
    return xt.reshape(bt, s, d)
```

```python
import jax
import jax.numpy as jnp
from jax import lax
from jax.experimental import pallas as pl
from jax.experimental.pallas import tpu as pltpu

D_MODEL = 2048
MIX_A = 1024
MIX_B = 1024
POOL_WINDOWS = (2, 4, 8, 16)
POOL_CH = MIX_A // len(POOL_WINDOWS)
CONV_W = 3
N_GROUPS = 4
E_PER_GROUP = 8
N_EXPERTS = N_GROUPS * E_PER_GROUP
TOP_K = 2
D_EXPERT = D_MODEL // 2
EPS = 1e-6

LANES = 128
ROW_SUB = D_MODEL // LANES
HIST = 16
TM = 256
CTM = 256
ROW_BLK = 256
VMEM_LIMIT = 56 * 1024 * 1024
WEIGHT_DMA_PRIORITY = 1
WEIGHT_DMA_CHUNKS = 8
DMA_QUEUES = 2
GATHER_SHIFT = 3
GATHER_UNROLL = 1 << GATHER_SHIFT
SCALAR_UNROLL = 16

_NEG = -1e30


def _rms(x, g):
    return x * lax.rsqrt(jnp.mean(x * x, axis=-1, keepdims=True) + EPS) * g


def _bdot(a, b):
    return jnp.dot(a, b, preferred_element_type=jnp.float32)


def _store_rows(ref, val):
    ref[...] = val.astype(jnp.bfloat16).reshape(val.shape[0] * ROW_SUB, LANES)


def _load_rows(ref, n):
    return ref[...].reshape(n, D_MODEL)


def _mixer_kernel(x_ref, x_next_ref, h0_ref, g1_ref, w_in_ref, w_pool_ref, pscale_ref, convw_ref, w_out_ref,
                  g2_ref, w_r_ref, b_r_ref,
                  x1_ref, h2_ref, route_ref,
                  ext_u, ext_z, h_buf):
    i = pl.program_id(0)

    @pl.when(i == 0)
    def _():
        ext_u[0:HIST, :] = jnp.zeros((HIST, MIX_A), jnp.float32)
        ext_z[0:HIST, :] = jnp.zeros((HIST, MIX_B), jnp.float32)
        h_buf[...] = h0_ref[...]

    x = x_ref[...]
    h = h_buf[...]

    ext_u[HIST:HIST + TM, :] = _bdot(h, w_in_ref[:, 0:MIX_A])
    c_gate = _bdot(h, w_in_ref[:, MIX_A + MIX_B:MIX_A + 2 * MIX_B])
    v = _bdot(h, w_in_ref[:, MIX_A + 2 * MIX_B:MIX_A + 3 * MIX_B])
    z = c_gate * v
    ext_z[HIST:HIST + TM, :] = z
    b_gate = _bdot(h, w_in_ref[:, MIX_A:MIX_A + MIX_B])

    y = ext_z[HIST - 2:HIST - 2 + TM, :] * convw_ref[0:1, :]
    y = y + ext_z[HIST - 1:HIST - 1 + TM, :] * convw_ref[1:2, :]
    y = y + z * convw_ref[2:3, :]
    y_b = b_gate * y
    ext_z[0:HIST, :] = ext_z[TM:TM + HIST, :]
    out_b = _bdot(y_b.astype(jnp.bfloat16), w_out_ref[MIX_A:MIX_A + MIX_B, :])

    row = lax.broadcasted_iota(jnp.int32, (TM, 1), 0) + i * TM + 1
    y_a = []
    for gi, w in enumerate(POOL_WINDOWS):
        c0 = gi * POOL_CH
        u = ext_u[HIST:HIST + TM, c0:c0 + POOL_CH]
        acc = u
        for s in range(1, w):
            acc = acc + ext_u[HIST - s:HIST - s + TM, c0:c0 + POOL_CH]
        cnt = jnp.minimum(row, w).astype(jnp.float32)
        pooled = (acc / cnt - u).astype(jnp.bfloat16)
        y_a.append(_bdot(pooled, w_pool_ref[gi]))
    y_a = jnp.concatenate(y_a, axis=-1) * pscale_ref[...]
    ext_u[0:HIST, :] = ext_u[TM:TM + HIST, :]
    out_a = _bdot(y_a.astype(jnp.bfloat16), w_out_ref[0:MIX_A, :])

    x1 = x + (out_a + out_b)
    x1_ref[...] = x1

    h2_f32 = _rms(x1, g2_ref[...])
    _store_rows(h2_ref, h2_f32)
    h2 = h2_f32.astype(jnp.bfloat16)

    logits = _bdot(h2, w_r_ref[...]) + b_r_ref[...]
    lane = lax.broadcasted_iota(jnp.int32, (TM, LANES), 1)
    lane_f = lane.astype(jnp.float32)

    def first_argmax(vals, vmax):
        return jnp.min(jnp.where(vals == vmax, lane_f, float(LANES)), axis=-1, keepdims=True)

    gl = jnp.where(lane < N_GROUPS, logits, _NEG)
    gmax = jnp.max(gl, axis=-1, keepdims=True)
    g_w = 1.0 / jnp.sum(jnp.exp(gl - gmax), axis=-1, keepdims=True)
    grp = first_argmax(gl, gmax).astype(jnp.int32)
    lo = N_GROUPS + E_PER_GROUP * grp
    el = jnp.where((lane >= lo) & (lane < lo + E_PER_GROUP), logits, _NEG)
    emax = jnp.max(el, axis=-1, keepdims=True)
    idx1 = first_argmax(el, emax)
    esum = jnp.sum(jnp.exp(el - emax), axis=-1, keepdims=True)
    el2 = jnp.where(lane_f == idx1, _NEG, el)
    e2max = jnp.max(el2, axis=-1, keepdims=True)
    idx2 = first_argmax(el2, e2max)
    p1 = 1.0 / esum
    p2 = jnp.exp(e2max - emax) / esum
    tot = p1 + p2
    w1 = g_w * (p1 / tot)
    w2 = g_w * (p2 / tot)
    route = jnp.where(lane == 0, w1, 0.0)
    route = jnp.where(lane == 1, w2, route)
    route = jnp.where(lane == 2, idx1 - N_GROUPS, route)
    route = jnp.where(lane == 3, idx2 - N_GROUPS, route)
    route_ref[...] = route

    h_buf[...] = _rms(x_next_ref[...], g1_ref[...]).astype(jnp.bfloat16)


def _first_tile_norm_kernel(x_ref, g_ref, o_ref):
    o_ref[...] = _rms(x_ref[...], g_ref[...]).astype(jnp.bfloat16)


def _first_tile_norm(x, g):
    return pl.pallas_call(
        _first_tile_norm_kernel,
        grid=(1,),
        in_specs=[pl.BlockSpec((TM, D_MODEL), lambda i: (0, 0)),
                  pl.BlockSpec((1, D_MODEL), lambda i: (0, 0))],
        out_specs=pl.BlockSpec((TM, D_MODEL), lambda i: (0, 0)),
        out_shape=jax.ShapeDtypeStruct((TM, D_MODEL), jnp.bfloat16),
        name="first_tile_norm",
    )(x, g)


def _mixer(x, g1, w_in, w_pool, pscale, convw, w_out, g2, w_r, b_r):
    t = x.shape[0]
    h0 = _first_tile_norm(x, g1)
    const = lambda shape: pl.BlockSpec(shape, lambda i: (0,) * len(shape),
                                       pipeline_mode=pl.Buffered(1))
    return pl.pallas_call(
        _mixer_kernel,
        grid=(t // TM,),
        in_specs=[
            pl.BlockSpec((TM, D_MODEL), lambda i: (i, 0)),
            pl.BlockSpec((TM, D_MODEL), lambda i: (jnp.minimum(i + 1, t // TM - 1), 0)),
            const((TM, D_MODEL)),
            const((1, D_MODEL)),
            const(w_in.shape),
            const(w_pool.shape),
            const((1, MIX_A)),
            const((CONV_W, MIX_B)),
            const(w_out.shape),
            const((1, D_MODEL)),
            const(w_r.shape),
            const((1, LANES)),
        ],
        out_specs=[
            pl.BlockSpec((TM, D_MODEL), lambda i: (i, 0)),
            pl.BlockSpec((TM * ROW_SUB, LANES), lambda i: (i, 0)),
            pl.BlockSpec((TM, LANES), lambda i: (i, 0)),
        ],
        out_shape=[
            jax.ShapeDtypeStruct((t, D_MODEL), jnp.float32),
            jax.ShapeDtypeStruct((t * ROW_SUB, LANES), jnp.bfloat16),
            jax.ShapeDtypeStruct((t, LANES), jnp.float32),
        ],
        scratch_shapes=[
            pltpu.VMEM((TM + HIST, MIX_A), jnp.float32),
            pltpu.VMEM((TM + HIST, MIX_B), jnp.float32),
            pltpu.VMEM((TM, D_MODEL), jnp.bfloat16),
        ],
        compiler_params=pltpu.CompilerParams(
            dimension_semantics=("arbitrary",), vmem_limit_bytes=VMEM_LIMIT),
        name="mixer",
    )(x, x, h0, g1, w_in, w_pool, pscale, convw, w_out, g2, w_r, b_r)


def _row_copy(src_hbm, row_start, dst_ref, r, sem):
    dst_start = r * ROW_SUB if isinstance(r, int) else pl.multiple_of(r * ROW_SUB, ROW_SUB)
    return pltpu.make_async_copy(src_hbm.at[pl.ds(pl.multiple_of(row_start, ROW_SUB), ROW_SUB)],
                                 dst_ref.at[pl.ds(dst_start, ROW_SUB)], sem)


def _gather_rows_issued(n_rows):
    if isinstance(n_rows, int):
        return n_rows
    return ((n_rows + GATHER_UNROLL - 1) >> GATHER_SHIFT) << GATHER_SHIFT


def _start_row_gather(src_hbm, idx_ref, base, n_rows, dst_ref, sem):
    if isinstance(n_rows, int):
        for r in range(n_rows):
            _row_copy(src_hbm, idx_ref[base + r], dst_ref, r, sem).start(priority=r % DMA_QUEUES)
        return

    def body(i, carry):
        for u in range(GATHER_UNROLL):
            r = i * GATHER_UNROLL + u
            src_row = idx_ref[base + jnp.minimum(r, n_rows - 1)]
            _row_copy(src_hbm, src_row, dst_ref, r, sem).start(priority=u % DMA_QUEUES)
        return carry

    lax.fori_loop(0, _gather_rows_issued(n_rows) >> GATHER_SHIFT, body, 0)


def _wait_row_gather(src_hbm, dst_ref, sem, n_rows):
    n = _gather_rows_issued(n_rows) * ROW_SUB
    pltpu.make_async_copy(src_hbm.at[pl.ds(0, n)], dst_ref.at[pl.ds(0, n)], sem).wait()


def _weight_copies(w_hbm, e, buf, slot, k, sem):
    rows = w_hbm.shape[1] // WEIGHT_DMA_CHUNKS
    return [pltpu.make_async_copy(w_hbm.at[e, pl.ds(c * rows, rows)],
                                  buf.at[slot, k, pl.ds(c * rows, rows)], sem.at[slot])
            for c in range(WEIGHT_DMA_CHUNKS)]


def _start_weight(*args):
    for cp in _weight_copies(*args):
        cp.start(priority=WEIGHT_DMA_PRIORITY)


def _wait_weight(*args):
    for cp in _weight_copies(*args):
        cp.wait()


def _build_row_sources(dest_ref, row_ref):
    n_tokens = dest_ref.shape[0] // TOP_K

    def scatter(i, carry):
        for u in range(SCALAR_UNROLL):
            tok = i * SCALAR_UNROLL + u
            for k in range(TOP_K):
                row_ref[dest_ref[k * n_tokens + tok]] = tok * ROW_SUB
        return carry

    lax.fori_loop(0, n_tokens // SCALAR_UNROLL, scatter, 0)


def _gate_up_kernel(blk_e_ref, first_ref, wslot_ref, nxt_e_ref, n_used_ref, n_valid_ref, dest_ref,
                    h2_hbm, wg_hbm, wu_hbm, hid_ref,
                    xg, wbuf, row_ref, gsem, wsem):
    b = pl.program_id(0)
    n_used = n_used_ref[0]
    e = blk_e_ref[b]
    slot = wslot_ref[b]

    @pl.when(b == 0)
    def _():
        _start_weight(wg_hbm, e, wbuf, slot, 0, wsem)
        _start_weight(wu_hbm, e, wbuf, slot, 1, wsem)
        xg[...] = jnp.zeros_like(xg)
        _build_row_sources(dest_ref, row_ref)
        _start_row_gather(h2_hbm, row_ref, 0, n_valid_ref[0], xg.at[0], gsem.at[0])

    @pl.when(b < n_used)
    def _():
        cur = b % 2

        @pl.when(b + 1 < n_used)
        def _():
            _start_row_gather(h2_hbm, row_ref, (b + 1) * ROW_BLK, n_valid_ref[b + 1],
                              xg.at[1 - cur], gsem.at[1 - cur])

        @pl.when(first_ref[b] == 1)
        def _():
            _wait_weight(wg_hbm, e, wbuf, slot, 0, wsem)
            _wait_weight(wu_hbm, e, wbuf, slot, 1, wsem)
            nxt = nxt_e_ref[b]

            @pl.when(nxt != e)
            def _():
                _start_weight(wg_hbm, nxt, wbuf, 1 - slot, 0, wsem)
                _start_weight(wu_hbm, nxt, wbuf, 1 - slot, 1, wsem)


        _wait_row_gather(h2_hbm, xg.at[cur], gsem.at[cur], n_valid_ref[b])
        xb = _load_rows(xg.at[cur], ROW_BLK)
        gate = _bdot(xb, wbuf[slot, 0].astype(jnp.bfloat16))
        up = _bdot(xb, wbuf[slot, 1].astype(jnp.bfloat16))
        hid_ref[...] = (gate * jax.nn.sigmoid(gate) * up).astype(jnp.bfloat16)

    @pl.when(b >= n_used)
    def _():
        hid_ref[...] = jnp.zeros_like(hid_ref)


def _gate_up(plan, h2, wg, wu):
    n_blocks = plan["blk_e"].shape[0]
    any_spec = pl.BlockSpec(memory_space=pl.ANY)
    return pl.pallas_call(
        _gate_up_kernel,
        grid_spec=pltpu.PrefetchScalarGridSpec(
            num_scalar_prefetch=7,
            grid=(n_blocks,),
            in_specs=[any_spec, any_spec, any_spec],
            out_specs=pl.BlockSpec((ROW_BLK, D_EXPERT), lambda b, *_: (b, 0)),
            scratch_shapes=[
                pltpu.VMEM((2, ROW_BLK * ROW_SUB, LANES), jnp.bfloat16),
                pltpu.VMEM((2, 2, D_MODEL, D_EXPERT), jnp.float32),
                pltpu.SMEM((n_blocks * ROW_BLK,), jnp.int32),
                pltpu.SemaphoreType.DMA((2,)),
                pltpu.SemaphoreType.DMA((2,)),
            ],
        ),
        out_shape=jax.ShapeDtypeStruct((n_blocks * ROW_BLK, D_EXPERT), jnp.bfloat16),
        compiler_params=pltpu.CompilerParams(
            dimension_semantics=("arbitrary",), vmem_limit_bytes=VMEM_LIMIT),
        name="gate_up",
    )(plan["blk_e"], plan["first"], plan["wslot"], plan["nxt_e"], plan["n_used"], plan["n_valid"],
      plan["dest"], h2, wg, wu)


def _down_kernel(blk_e_ref, first_ref, wslot_ref, nxt_e_ref, n_used_ref,
                 hid_ref, wd_hbm, y_ref, wbuf, wsem):
    b = pl.program_id(0)
    n_used = n_used_ref[0]
    e = blk_e_ref[b]
    slot = wslot_ref[b]

    @pl.when(b == 0)
    def _():
        _start_weight(wd_hbm, e, wbuf, slot, 0, wsem)

    @pl.when(b < n_used)
    def _():
        @pl.when(first_ref[b] == 1)
        def _():
            _wait_weight(wd_hbm, e, wbuf, slot, 0, wsem)
            nxt = nxt_e_ref[b]

            @pl.when(nxt != e)
            def _():
                _start_weight(wd_hbm, nxt, wbuf, 1 - slot, 0, wsem)


        _store_rows(y_ref, _bdot(hid_ref[...], wbuf[slot, 0].astype(jnp.bfloat16)))

    @pl.when(b >= n_used)
    def _():
        y_ref[...] = jnp.zeros_like(y_ref)


def _down(plan, hid, wd):
    n_blocks = plan["blk_e"].shape[0]
    return pl.pallas_call(
        _down_kernel,
        grid_spec=pltpu.PrefetchScalarGridSpec(
            num_scalar_prefetch=5,
            grid=(n_blocks,),
            in_specs=[pl.BlockSpec((ROW_BLK, D_EXPERT), lambda b, *_: (b, 0)),
                      pl.BlockSpec(memory_space=pl.ANY)],
            out_specs=pl.BlockSpec((ROW_BLK * ROW_SUB, LANES), lambda b, *_: (b, 0)),
            scratch_shapes=[
                pltpu.VMEM((2, 1, D_EXPERT, D_MODEL), jnp.float32),
                pltpu.SemaphoreType.DMA((2,)),
            ],
        ),
        out_shape=jax.ShapeDtypeStruct((n_blocks * ROW_BLK * ROW_SUB, LANES), jnp.bfloat16),
        compiler_params=pltpu.CompilerParams(
            dimension_semantics=("arbitrary",), vmem_limit_bytes=VMEM_LIMIT),
        name="down",
    )(plan["blk_e"], plan["first"], plan["wslot"], plan["nxt_e"], plan["n_used"], hid, wd)


def _combine_kernel(dest_ref, x1_ref, route_ref, g_ref, y_hbm, o_ref, y0_buf, y1_buf, sem):
    i = pl.program_id(0)
    n = pl.num_programs(0)
    cur = i % 2
    n_tokens = n * CTM

    def start(step, slot):
        _start_row_gather(y_hbm, dest_ref, step * CTM, CTM, y0_buf.at[slot], sem.at[0, slot])
        _start_row_gather(y_hbm, dest_ref, n_tokens + step * CTM, CTM, y1_buf.at[slot], sem.at[1, slot])

    @pl.when(i == 0)
    def _():
        start(0, 0)

    @pl.when(i + 1 < n)
    def _():
        start(i + 1, 1 - cur)

    r = route_ref[...]
    _wait_row_gather(y_hbm, y0_buf.at[cur], sem.at[0, cur], CTM)
    _wait_row_gather(y_hbm, y1_buf.at[cur], sem.at[1, cur], CTM)
    y0 = _load_rows(y0_buf.at[cur], CTM).astype(jnp.float32)
    y1 = _load_rows(y1_buf.at[cur], CTM).astype(jnp.float32)
    xo = x1_ref[...] + (y0 * r[:, 0:1] + y1 * r[:, 1:2])
    o_ref[...] = _rms(xo, g_ref[...])


def _combine(dest_rows, x1, route, g, ybuf):
    t = x1.shape[0]
    tile = lambda w: pl.BlockSpec((CTM, w), lambda i, *_: (i, 0))
    return pl.pallas_call(
        _combine_kernel,
        grid_spec=pltpu.PrefetchScalarGridSpec(
            num_scalar_prefetch=1,
            grid=(t // CTM,),
            in_specs=[tile(D_MODEL), tile(LANES),
                      pl.BlockSpec((1, D_MODEL), lambda i, *_: (0, 0)),
                      pl.BlockSpec(memory_space=pl.ANY)],
            out_specs=tile(D_MODEL),
            scratch_shapes=[pltpu.VMEM((2, CTM * ROW_SUB, LANES), jnp.bfloat16),
                            pltpu.VMEM((2, CTM * ROW_SUB, LANES), jnp.bfloat16),
                            pltpu.SemaphoreType.DMA((2, 2))],
        ),
        out_shape=jax.ShapeDtypeStruct((t, D_MODEL), jnp.float32),
        compiler_params=pltpu.CompilerParams(
            dimension_semantics=("arbitrary",), vmem_limit_bytes=VMEM_LIMIT),
        name="combine",
    )(dest_rows, x1, route, g, ybuf)


def _dispatch_plan(e_flat):
    a = e_flat.shape[0]
    i32 = jnp.int32
    experts = jnp.arange(N_EXPERTS, dtype=i32)
    onehot = (e_flat[:, None] == experts[None, :]).astype(i32)
    rank = jnp.cumsum(onehot, axis=0) - onehot
    counts = jnp.sum(onehot, axis=0)
    padded = ((counts + ROW_BLK - 1) // ROW_BLK) * ROW_BLK
    pends = jnp.cumsum(padded)
    pstarts = pends - padded
    dest = jnp.sum((rank + pstarts[None, :]) * onehot, axis=1).astype(i32)
    n_blocks = (a + N_EXPERTS * (ROW_BLK - 1) + ROW_BLK - 1) // ROW_BLK
    blk_start = jnp.arange(n_blocks, dtype=i32) * ROW_BLK
    blk_e = jnp.clip(jnp.sum(pends[None, :] <= blk_start[:, None], axis=1), 0, N_EXPERTS - 1).astype(i32)
    first = jnp.concatenate([jnp.ones((1,), i32), (blk_e[1:] != blk_e[:-1]).astype(i32)])
    wslot = (jnp.cumsum(first) - 1) % 2
    later_used = (experts[None, :] > experts[:, None]) & (counts[None, :] > 0)
    nxt = jnp.min(jnp.where(later_used, experts[None, :], N_EXPERTS), axis=1)
    nxt = jnp.where(nxt == N_EXPERTS, experts, nxt)
    return {
        "dest": dest,
        "blk_e": blk_e,
        "first": first,
        "wslot": wslot.astype(i32),
        "nxt_e": nxt[blk_e].astype(i32),
        "n_used": (pends[-1] // ROW_BLK).astype(i32).reshape(1),
        "n_valid": jnp.clip(counts[blk_e] - (blk_start - pstarts[blk_e]), 0, ROW_BLK).astype(i32),
    }


def kernel(x, norm_mix_g, w_in, w_pool, pool_scale, conv_w, w_out, norm_ffn_g, w_router_group, b_router_group, w_router_expert, b_router_expert, w_gate, w_up, w_down, norm_final_g):
    bt, s, d = x.shape
    t = bt * s
    bf = jnp.bfloat16
    xt = x.reshape(t, d)
    assert norm_mix_g.shape[0] == 1, "the final norm is fused into the last layer's combine"
    for l in range(norm_mix_g.shape[0]):
        w_r = jnp.concatenate([w_router_group[l], w_router_expert[l]], axis=1)
        w_r = jnp.pad(w_r, ((0, 0), (0, LANES - w_r.shape[1]))).astype(bf)
        b_r = jnp.concatenate([b_router_group[l], b_router_expert[l]])
        b_r = jnp.pad(b_r, (0, LANES - b_r.shape[0])).reshape(1, LANES)
        x1, h2, route = _mixer(
            xt, norm_mix_g[l].reshape(1, d), w_in[l].astype(bf), w_pool[l].astype(bf),
            pool_scale[l].reshape(1, MIX_A), conv_w[l].T, w_out[l].astype(bf),
            norm_ffn_g[l].reshape(1, d), w_r, b_r)
        e_flat = jnp.concatenate([route[:, 2 + k] for k in range(TOP_K)]).astype(jnp.int32)
        plan = _dispatch_plan(e_flat)
        hid = _gate_up(plan, h2, w_gate[l], w_up[l])
        ybuf = _down(plan, hid, w_down[l])
        xt = _combine(plan["dest"] * ROW_SUB, x1, route, norm_final_g.reshape(1, d), ybuf)
    return xt.reshape(bt, s, d)
```

```python
import jax
import jax.numpy as jnp
from jax import lax
from jax.experimental import pallas as pl
from jax.experimental.pallas import tpu as pltpu

D_MODEL = 2048
MIX_A = 1024
MIX_B = 1024
POOL_WINDOWS = (2, 4, 8, 16)
POOL_CH = MIX_A // len(POOL_WINDOWS)
CONV_W = 3
N_GROUPS = 4
E_PER_GROUP = 8
N_EXPERTS = N_GROUPS * E_PER_GROUP
TOP_K = 2
D_EXPERT = D_MODEL // 2
EPS = 1e-6

LANES = 128
ROW_SUB = D_MODEL // LANES
HIST = 16
TM = 256
CTM = 256
ROW_BLK = 256
VMEM_LIMIT = 56 * 1024 * 1024
WEIGHT_DMA_PRIORITY = 1
WEIGHT_DMA_CHUNKS = 8
WEIGHT_SLAB = 128
DMA_QUEUES = 2
GATHER_SHIFT = 3
GATHER_UNROLL = 1 << GATHER_SHIFT
SCALAR_UNROLL = 16

_NEG = -1e30


def _rms(x, g):
    return x * lax.rsqrt(jnp.mean(x * x, axis=-1, keepdims=True) + EPS) * g


def _bdot(a, b):
    return jnp.dot(a, b, preferred_element_type=jnp.float32)


def _store_rows(ref, val):
    ref[...] = val.astype(jnp.bfloat16).reshape(val.shape[0] * ROW_SUB, LANES)


def _load_rows(ref, n):
    return ref[...].reshape(n, D_MODEL)


def _load_cast_weight(w_hbm, w_bf, land, sem):
    rows, cols = w_hbm.shape
    n_slabs = rows // WEIGHT_SLAB

    def slab_copy(c):
        return pltpu.make_async_copy(w_hbm.at[pl.ds(c * WEIGHT_SLAB, WEIGHT_SLAB)],
                                     land.at[c % 2, :, pl.ds(0, cols)], sem.at[c % 2])

    slab_copy(0).start()
    for c in range(n_slabs):
        if c + 1 < n_slabs:
            slab_copy(c + 1).start()
        slab_copy(c).wait()
        w_bf[pl.ds(c * WEIGHT_SLAB, WEIGHT_SLAB), :] = land[c % 2, :, 0:cols].astype(jnp.bfloat16)


def _mixer_kernel(x_ref, x_next_ref, h0_ref, g1_ref, w_in_hbm, w_pool_ref, pscale_ref, convw_ref, w_out_hbm,
                  g2_ref, w_r_ref, b_r_ref,
                  x1_ref, h2_ref, route_ref,
                  ext_u, ext_z, h_buf, w_in_ref, w_out_ref, land, wsem):
    i = pl.program_id(0)

    @pl.when(i == 0)
    def _():
        ext_u[0:HIST, :] = jnp.zeros((HIST, MIX_A), jnp.float32)
        ext_z[0:HIST, :] = jnp.zeros((HIST, MIX_B), jnp.float32)
        h_buf[...] = h0_ref[...]
        _load_cast_weight(w_in_hbm, w_in_ref, land, wsem)
        _load_cast_weight(w_out_hbm, w_out_ref, land, wsem)

    x = x_ref[...]
    h = h_buf[...]

    ext_u[HIST:HIST + TM, :] = _bdot(h, w_in_ref[:, 0:MIX_A])
    c_gate = _bdot(h, w_in_ref[:, MIX_A + MIX_B:MIX_A + 2 * MIX_B])
    v = _bdot(h, w_in_ref[:, MIX_A + 2 * MIX_B:MIX_A + 3 * MIX_B])
    z = c_gate * v
    ext_z[HIST:HIST + TM, :] = z
    b_gate = _bdot(h, w_in_ref[:, MIX_A:MIX_A + MIX_B])

    y = ext_z[HIST - 2:HIST - 2 + TM, :] * convw_ref[0:1, :]
    y = y + ext_z[HIST - 1:HIST - 1 + TM, :] * convw_ref[1:2, :]
    y = y + z * convw_ref[2:3, :]
    y_b = b_gate * y
    ext_z[0:HIST, :] = ext_z[TM:TM + HIST, :]
    out_b = _bdot(y_b.astype(jnp.bfloat16), w_out_ref[MIX_A:MIX_A + MIX_B, :])

    row = lax.broadcasted_iota(jnp.int32, (TM, 1), 0) + i * TM + 1
    y_a = []
    for gi, w in enumerate(POOL_WINDOWS):
        c0 = gi * POOL_CH
        u = ext_u[HIST:HIST + TM, c0:c0 + POOL_CH]
        acc = u
        for s in range(1, w):
            acc = acc + ext_u[HIST - s:HIST - s + TM, c0:c0 + POOL_CH]
        cnt = jnp.minimum(row, w).astype(jnp.float32)
        pooled = (acc / cnt - u).astype(jnp.bfloat16)
        y_a.append(_bdot(pooled, w_pool_ref[gi]))
    y_a = jnp.concatenate(y_a, axis=-1) * pscale_ref[...]
    ext_u[0:HIST, :] = ext_u[TM:TM + HIST, :]
    out_a = _bdot(y_a.astype(jnp.bfloat16), w_out_ref[0:MIX_A, :])

    x1 = x + (out_a + out_b)
    x1_ref[...] = x1

    h2_f32 = _rms(x1, g2_ref[...])
    _store_rows(h2_ref, h2_f32)
    h2 = h2_f32.astype(jnp.bfloat16)

    logits = _bdot(h2, w_r_ref[...]) + b_r_ref[...]
    lane = lax.broadcasted_iota(jnp.int32, (TM, LANES), 1)
    lane_f = lane.astype(jnp.float32)

    def first_argmax(vals, vmax):
        return jnp.min(jnp.where(vals == vmax, lane_f, float(LANES)), axis=-1, keepdims=True)

    gl = jnp.where(lane < N_GROUPS, logits, _NEG)
    gmax = jnp.max(gl, axis=-1, keepdims=True)
    g_w = 1.0 / jnp.sum(jnp.exp(gl - gmax), axis=-1, keepdims=True)
    grp = first_argmax(gl, gmax).astype(jnp.int32)
    lo = N_GROUPS + E_PER_GROUP * grp
    el = jnp.where((lane >= lo) & (lane < lo + E_PER_GROUP), logits, _NEG)
    emax = jnp.max(el, axis=-1, keepdims=True)
    idx1 = first_argmax(el, emax)
    esum = jnp.sum(jnp.exp(el - emax), axis=-1, keepdims=True)
    el2 = jnp.where(lane_f == idx1, _NEG, el)
    e2max = jnp.max(el2, axis=-1, keepdims=True)
    idx2 = first_argmax(el2, e2max)
    p1 = 1.0 / esum
    p2 = jnp.exp(e2max - emax) / esum
    tot = p1 + p2
    w1 = g_w * (p1 / tot)
    w2 = g_w * (p2 / tot)
    route = jnp.where(lane == 0, w1, 0.0)
    route = jnp.where(lane == 1, w2, route)
    route = jnp.where(lane == 2, idx1 - N_GROUPS, route)
    route = jnp.where(lane == 3, idx2 - N_GROUPS, route)
    route_ref[...] = route

    h_buf[...] = _rms(x_next_ref[...], g1_ref[...]).astype(jnp.bfloat16)


def _first_tile_norm_kernel(x_ref, g_ref, o_ref):
    o_ref[...] = _rms(x_ref[...], g_ref[...]).astype(jnp.bfloat16)


def _first_tile_norm(x, g):
    return pl.pallas_call(
        _first_tile_norm_kernel,
        grid=(1,),
        in_specs=[pl.BlockSpec((TM, D_MODEL), lambda i: (0, 0)),
                  pl.BlockSpec((1, D_MODEL), lambda i: (0, 0))],
        out_specs=pl.BlockSpec((TM, D_MODEL), lambda i: (0, 0)),
        out_shape=jax.ShapeDtypeStruct((TM, D_MODEL), jnp.bfloat16),
        name="first_tile_norm",
    )(x, g)


def _mixer(x, g1, w_in, w_pool, pscale, convw, w_out, g2, w_r, b_r):
    t = x.shape[0]
    h0 = _first_tile_norm(x, g1)
    const = lambda shape: pl.BlockSpec(shape, lambda i: (0,) * len(shape),
                                       pipeline_mode=pl.Buffered(1))
    return pl.pallas_call(
        _mixer_kernel,
        grid=(t // TM,),
        in_specs=[
            pl.BlockSpec((TM, D_MODEL), lambda i: (i, 0)),
            pl.BlockSpec((TM, D_MODEL), lambda i: (jnp.minimum(i + 1, t // TM - 1), 0)),
            const((TM, D_MODEL)),
            const((1, D_MODEL)),
            pl.BlockSpec(memory_space=pl.ANY),
            const(w_pool.shape),
            const((1, MIX_A)),
            const((CONV_W, MIX_B)),
            pl.BlockSpec(memory_space=pl.ANY),
            const((1, D_MODEL)),
            const(w_r.shape),
            const((1, LANES)),
        ],
        out_specs=[
            pl.BlockSpec((TM, D_MODEL), lambda i: (i, 0)),
            pl.BlockSpec((TM * ROW_SUB, LANES), lambda i: (i, 0)),
            pl.BlockSpec((TM, LANES), lambda i: (i, 0)),
        ],
        out_shape=[
            jax.ShapeDtypeStruct((t, D_MODEL), jnp.float32),
            jax.ShapeDtypeStruct((t * ROW_SUB, LANES), jnp.bfloat16),
            jax.ShapeDtypeStruct((t, LANES), jnp.float32),
        ],
        scratch_shapes=[
            pltpu.VMEM((TM + HIST, MIX_A), jnp.float32),
            pltpu.VMEM((TM + HIST, MIX_B), jnp.float32),
            pltpu.VMEM((TM, D_MODEL), jnp.bfloat16),
            pltpu.VMEM(w_in.shape, jnp.bfloat16),
            pltpu.VMEM(w_out.shape, jnp.bfloat16),
            pltpu.VMEM((2, WEIGHT_SLAB, w_in.shape[1]), jnp.float32),
            pltpu.SemaphoreType.DMA((2,)),
        ],
        compiler_params=pltpu.CompilerParams(
            dimension_semantics=("arbitrary",), vmem_limit_bytes=VMEM_LIMIT),
        name="mixer",
    )(x, x, h0, g1, w_in, w_pool, pscale, convw, w_out, g2, w_r, b_r)


def _row_copy(src_hbm, row_start, dst_ref, r, sem):
    dst_start = r * ROW_SUB if isinstance(r, int) else pl.multiple_of(r * ROW_SUB, ROW_SUB)
    return pltpu.make_async_copy(src_hbm.at[pl.ds(pl.multiple_of(row_start, ROW_SUB), ROW_SUB)],
                                 dst_ref.at[pl.ds(dst_start, ROW_SUB)], sem)


def _gather_rows_issued(n_rows):
    if isinstance(n_rows, int):
        return n_rows
    return ((n_rows + GATHER_UNROLL - 1) >> GATHER_SHIFT) << GATHER_SHIFT


def _start_row_gather(src_hbm, idx_ref, base, n_rows, dst_ref, sem):
    if isinstance(n_rows, int):
        for r in range(n_rows):
            _row_copy(src_hbm, idx_ref[base + r], dst_ref, r, sem).start(priority=r % DMA_QUEUES)
        return

    def body(i, carry):
        for u in range(GATHER_UNROLL):
            r = i * GATHER_UNROLL + u
            src_row = idx_ref[base + jnp.minimum(r, n_rows - 1)]
            _row_copy(src_hbm, src_row, dst_ref, r, sem).start(priority=u % DMA_QUEUES)
        return carry

    lax.fori_loop(0, _gather_rows_issued(n_rows) >> GATHER_SHIFT, body, 0)


def _wait_row_gather(src_hbm, dst_ref, sem, n_rows):
    n = _gather_rows_issued(n_rows) * ROW_SUB
    pltpu.make_async_copy(src_hbm.at[pl.ds(0, n)], dst_ref.at[pl.ds(0, n)], sem).wait()


def _weight_copies(w_hbm, e, buf, slot, k, sem):
    rows = w_hbm.shape[1] // WEIGHT_DMA_CHUNKS
    return [pltpu.make_async_copy(w_hbm.at[e, pl.ds(c * rows, rows)],
                                  buf.at[slot, k, pl.ds(c * rows, rows)], sem.at[slot])
            for c in range(WEIGHT_DMA_CHUNKS)]


def _start_weight(*args):
    for cp in _weight_copies(*args):
        cp.start(priority=WEIGHT_DMA_PRIORITY)


def _wait_weight(*args):
    for cp in _weight_copies(*args):
        cp.wait()


def _build_row_sources(dest_ref, row_ref):
    n_tokens = dest_ref.shape[0] // TOP_K

    def scatter(i, carry):
        for u in range(SCALAR_UNROLL):
            tok = i * SCALAR_UNROLL + u
            for k in range(TOP_K):
                row_ref[dest_ref[k * n_tokens + tok]] = tok * ROW_SUB
        return carry

    lax.fori_loop(0, n_tokens // SCALAR_UNROLL, scatter, 0)


def _gate_up_kernel(blk_e_ref, first_ref, wslot_ref, nxt_e_ref, n_used_ref, n_valid_ref, dest_ref,
                    h2_hbm, wg_hbm, wu_hbm, hid_ref,
                    xg, wbuf, row_ref, gsem, wsem):
    b = pl.program_id(0)
    n_used = n_used_ref[0]
    e = blk_e_ref[b]
    slot = wslot_ref[b]

    @pl.when(b == 0)
    def _():
        _start_weight(wg_hbm, e, wbuf, slot, 0, wsem)
        _start_weight(wu_hbm, e, wbuf, slot, 1, wsem)
        xg[...] = jnp.zeros_like(xg)
        _build_row_sources(dest_ref, row_ref)
        _start_row_gather(h2_hbm, row_ref, 0, n_valid_ref[0], xg.at[0], gsem.at[0])

    @pl.when(b < n_used)
    def _():
        cur = b % 2

        @pl.when(b + 1 < n_used)
        def _():
            _start_row_gather(h2_hbm, row_ref, (b + 1) * ROW_BLK, n_valid_ref[b + 1],
                              xg.at[1 - cur], gsem.at[1 - cur])

        @pl.when(first_ref[b] == 1)
        def _():
            _wait_weight(wg_hbm, e, wbuf, slot, 0, wsem)
            _wait_weight(wu_hbm, e, wbuf, slot, 1, wsem)
            nxt = nxt_e_ref[b]

            @pl.when(nxt != e)
            def _():
                _start_weight(wg_hbm, nxt, wbuf, 1 - slot, 0, wsem)
                _start_weight(wu_hbm, nxt, wbuf, 1 - slot, 1, wsem)


        _wait_row_gather(h2_hbm, xg.at[cur], gsem.at[cur], n_valid_ref[b])
        xb = _load_rows(xg.at[cur], ROW_BLK)
        gate = _bdot(xb, wbuf[slot, 0].astype(jnp.bfloat16))
        up = _bdot(xb, wbuf[slot, 1].astype(jnp.bfloat16))
        hid_ref[...] = (gate * jax.nn.sigmoid(gate) * up).astype(jnp.bfloat16)

    @pl.when(b >= n_used)
    def _():
        hid_ref[...] = jnp.zeros_like(hid_ref)


def _gate_up(plan, h2, wg, wu):
    n_blocks = plan["blk_e"].shape[0]
    any_spec = pl.BlockSpec(memory_space=pl.ANY)
    return pl.pallas_call(
        _gate_up_kernel,
        grid_spec=pltpu.PrefetchScalarGridSpec(
            num_scalar_prefetch=7,
            grid=(n_blocks,),
            in_specs=[any_spec, any_spec, any_spec],
            out_specs=pl.BlockSpec((ROW_BLK, D_EXPERT), lambda b, *_: (b, 0)),
            scratch_shapes=[
                pltpu.VMEM((2, ROW_BLK * ROW_SUB, LANES), jnp.bfloat16),
                pltpu.VMEM((2, 2, D_MODEL, D_EXPERT), jnp.float32),
                pltpu.SMEM((n_blocks * ROW_BLK,), jnp.int32),
                pltpu.SemaphoreType.DMA((2,)),
                pltpu.SemaphoreType.DMA((2,)),
            ],
        ),
        out_shape=jax.ShapeDtypeStruct((n_blocks * ROW_BLK, D_EXPERT), jnp.bfloat16),
        compiler_params=pltpu.CompilerParams(
            dimension_semantics=("arbitrary",), vmem_limit_bytes=VMEM_LIMIT),
        name="gate_up",
    )(plan["blk_e"], plan["first"], plan["wslot"], plan["nxt_e"], plan["n_used"], plan["n_valid"],
      plan["dest"], h2, wg, wu)


def _down_kernel(blk_e_ref, first_ref, wslot_ref, nxt_e_ref, n_used_ref,
                 hid_ref, wd_hbm, y_ref, wbuf, wsem):
    b = pl.program_id(0)
    n_used = n_used_ref[0]
    e = blk_e_ref[b]
    slot = wslot_ref[b]

    @pl.when(b == 0)
    def _():
        _start_weight(wd_hbm, e, wbuf, slot, 0, wsem)

    @pl.when(b < n_used)
    def _():
        @pl.when(first_ref[b] == 1)
        def _():
            _wait_weight(wd_hbm, e, wbuf, slot, 0, wsem)
            nxt = nxt_e_ref[b]

            @pl.when(nxt != e)
            def _():
                _start_weight(wd_hbm, nxt, wbuf, 1 - slot, 0, wsem)


        _store_rows(y_ref, _bdot(hid_ref[...], wbuf[slot, 0].astype(jnp.bfloat16)))

    @pl.when(b >= n_used)
    def _():
        y_ref[...] = jnp.zeros_like(y_ref)


def _down(plan, hid, wd):
    n_blocks = plan["blk_e"].shape[0]
    return pl.pallas_call(
        _down_kernel,
        grid_spec=pltpu.PrefetchScalarGridSpec(
            num_scalar_prefetch=5,
            grid=(n_blocks,),
            in_specs=[pl.BlockSpec((ROW_BLK, D_EXPERT), lambda b, *_: (b, 0)),
                      pl.BlockSpec(memory_space=pl.ANY)],
            out_specs=pl.BlockSpec((ROW_BLK * ROW_SUB, LANES), lambda b, *_: (b, 0)),
            scratch_shapes=[
                pltpu.VMEM((2, 1, D_EXPERT, D_MODEL), jnp.float32),
                pltpu.SemaphoreType.DMA((2,)),
            ],
        ),
        out_shape=jax.ShapeDtypeStruct((n_blocks * ROW_BLK * ROW_SUB, LANES), jnp.bfloat16),
        compiler_params=pltpu.CompilerParams(
            dimension_semantics=("arbitrary",), vmem_limit_bytes=VMEM_LIMIT),
        name="down",
    )(plan["blk_e"], plan["first"], plan["wslot"], plan["nxt_e"], plan["n_used"], hid, wd)


def _combine_kernel(dest_ref, x1_ref, route_ref, g_ref, y_hbm, o_ref, y0_buf, y1_buf, sem):
    i = pl.program_id(0)
    n = pl.num_programs(0)
    cur = i % 2
    n_tokens = n * CTM

    def start(step, slot):
        _start_row_gather(y_hbm, dest_ref, step * CTM, CTM, y0_buf.at[slot], sem.at[0, slot])
        _start_row_gather(y_hbm, dest_ref, n_tokens + step * CTM, CTM, y1_buf.at[slot], sem.at[1, slot])

    @pl.when(i == 0)
    def _():
        start(0, 0)

    @pl.when(i + 1 < n)
    def _():
        start(i + 1, 1 - cur)

    r = route_ref[...]
    _wait_row_gather(y_hbm, y0_buf.at[cur], sem.at[0, cur], CTM)
    _wait_row_gather(y_hbm, y1_buf.at[cur], sem.at[1, cur], CTM)
    y0 = _load_rows(y0_buf.at[cur], CTM).astype(jnp.float32)
    y1 = _load_rows(y1_buf.at[cur], CTM).astype(jnp.float32)
    xo = x1_ref[...] + (y0 * r[:, 0:1] + y1 * r[:, 1:2])
    o_ref[...] = _rms(xo, g_ref[...])


def _combine(dest_rows, x1, route, g, ybuf):
    t = x1.shape[0]
    tile = lambda w: pl.BlockSpec((CTM, w), lambda i, *_: (i, 0))
    return pl.pallas_call(
        _combine_kernel,
        grid_spec=pltpu.PrefetchScalarGridSpec(
            num_scalar_prefetch=1,
            grid=(t // CTM,),
            in_specs=[tile(D_MODEL), tile(LANES),
                      pl.BlockSpec((1, D_MODEL), lambda i, *_: (0, 0)),
                      pl.BlockSpec(memory_space=pl.ANY)],
            out_specs=tile(D_MODEL),
            scratch_shapes=[pltpu.VMEM((2, CTM * ROW_SUB, LANES), jnp.bfloat16),
                            pltpu.VMEM((2, CTM * ROW_SUB, LANES), jnp.bfloat16),
                            pltpu.SemaphoreType.DMA((2, 2))],
        ),
        out_shape=jax.ShapeDtypeStruct((t, D_MODEL), jnp.float32),
        compiler_params=pltpu.CompilerParams(
            dimension_semantics=("arbitrary",), vmem_limit_bytes=VMEM_LIMIT),
        name="combine",
    )(dest_rows, x1, route, g, ybuf)


def _dispatch_plan(e_flat):
    a = e_flat.shape[0]
    i32 = jnp.int32
    experts = jnp.arange(N_EXPERTS, dtype=i32)
    onehot = (e_flat[:, None] == experts[None, :]).astype(i32)
    rank = jnp.cumsum(onehot, axis=0) - onehot
    counts = jnp.sum(onehot, axis=0)
    padded = ((counts + ROW_BLK - 1) // ROW_BLK) * ROW_BLK
    pends = jnp.cumsum(padded)
    pstarts = pends - padded
    dest = jnp.sum((rank + pstarts[None, :]) * onehot, axis=1).astype(i32)
    n_blocks = (a + N_EXPERTS * (ROW_BLK - 1) + ROW_BLK - 1) // ROW_BLK
    blk_start = jnp.arange(n_blocks, dtype=i32) * ROW_BLK
    blk_e = jnp.clip(jnp.sum(pends[None, :] <= blk_start[:, None], axis=1), 0, N_EXPERTS - 1).astype(i32)
    first = jnp.concatenate([jnp.ones((1,), i32), (blk_e[1:] != blk_e[:-1]).astype(i32)])
    wslot = (jnp.cumsum(first) - 1) % 2
    later_used = (experts[None, :] > experts[:, None]) & (counts[None, :] > 0)
    nxt = jnp.min(jnp.where(later_used, experts[None, :], N_EXPERTS), axis=1)
    nxt = jnp.where(nxt == N_EXPERTS, experts, nxt)
    return {
        "dest": dest,
        "blk_e": blk_e,
        "first": first,
        "wslot": wslot.astype(i32),
        "nxt_e": nxt[blk_e].astype(i32),
        "n_used": (pends[-1] // ROW_BLK).astype(i32).reshape(1),
        "n_valid": jnp.clip(counts[blk_e] - (blk_start - pstarts[blk_e]), 0, ROW_BLK).astype(i32),
    }


def kernel(x, norm_mix_g, w_in, w_pool, pool_scale, conv_w, w_out, norm_ffn_g, w_router_group, b_router_group, w_router_expert, b_router_expert, w_gate, w_up, w_down, norm_final_g):
    bt, s, d = x.shape
    t = bt * s
    bf = jnp.bfloat16
    xt = x.reshape(t, d)
    assert norm_mix_g.shape[0] == 1, "the final norm is fused into the last layer's combine"
    for l in range(norm_mix_g.shape[0]):
        w_r = jnp.concatenate([w_router_group[l], w_router_expert[l]], axis=1)
        w_r = jnp.pad(w_r, ((0, 0), (0, LANES - w_r.shape[1]))).astype(bf)
        b_r = jnp.concatenate([b_router_group[l], b_router_expert[l]])
        b_r = jnp.pad(b_r, (0, LANES - b_r.shape[0])).reshape(1, LANES)
        x1, h2, route = _mixer(
            xt, norm_mix_g[l].reshape(1, d), w_in[l], w_pool[l].astype(bf),
            pool_scale[l].reshape(1, MIX_A), conv_w[l].T, w_out[l],
            norm_ffn_g[l].reshape(1, d), w_r, b_r)
        e_flat = jnp.concatenate([route[:, 2 + k] for k in range(TOP_K)]).astype(jnp.int32)
        plan = _dispatch_plan(e_flat)
        hid = _gate_up(plan, h2, w_gate[l], w_up[l])
        ybuf = _down(plan, hid, w_down[l])
        xt = _combine(plan["dest"] * ROW_SUB, x1, route, norm_final_g.reshape(1, d), ybuf)
    return xt.reshape(bt, s, d)
```

```python
import jax
import jax.numpy as jnp
from jax import lax
from jax.experimental import pallas as pl
from jax.experimental.pallas import tpu as pltpu

D_MODEL = 2048
MIX_A = 1024
MIX_B = 1024
POOL_WINDOWS = (2, 4, 8, 16)
POOL_CH = MIX_A // len(POOL_WINDOWS)
CONV_W = 3
N_GROUPS = 4
E_PER_GROUP = 8
N_EXPERTS = N_GROUPS * E_PER_GROUP
TOP_K = 2
D_EXPERT = D_MODEL // 2
EPS = 1e-6

LANES = 128
ROW_SUB = D_MODEL // LANES
HIST = 16
TM = 256
CTM = 256
ROW_BLK = 256
VMEM_LIMIT = 56 * 1024 * 1024
WEIGHT_DMA_PRIORITY = 1
WEIGHT_DMA_CHUNKS = 8
WEIGHT_SLAB = 128
WEIGHT_SLAB_SPLIT = 4
DMA_QUEUES = 2
GATHER_SHIFT = 3
GATHER_UNROLL = 1 << GATHER_SHIFT
SCALAR_UNROLL = 16

_NEG = -1e30


def _rms(x, g):
    return x * lax.rsqrt(jnp.mean(x * x, axis=-1, keepdims=True) + EPS) * g


def _bdot(a, b):
    return jnp.dot(a, b, preferred_element_type=jnp.float32)


def _store_rows(ref, val):
    ref[...] = val.astype(jnp.bfloat16).reshape(val.shape[0] * ROW_SUB, LANES)


def _load_rows(ref, n):
    return ref[...].reshape(n, D_MODEL)


def _load_cast_weight(w_hbm, w_bf, land, sem):
    rows, cols = w_hbm.shape
    n_slabs = rows // WEIGHT_SLAB

    cw = cols // WEIGHT_SLAB_SPLIT

    def slab_copies(c):
        return [pltpu.make_async_copy(w_hbm.at[pl.ds(c * WEIGHT_SLAB, WEIGHT_SLAB), pl.ds(q * cw, cw)],
                                      land.at[c % 2, :, pl.ds(q * cw, cw)], sem.at[c % 2])
                for q in range(WEIGHT_SLAB_SPLIT)]

    def start(c):
        for cp in slab_copies(c):
            cp.start()

    start(0)
    for c in range(n_slabs):
        if c + 1 < n_slabs:
            start(c + 1)
        for cp in slab_copies(c):
            cp.wait()
        w_bf[pl.ds(c * WEIGHT_SLAB, WEIGHT_SLAB), :] = land[c % 2, :, 0:cols].astype(jnp.bfloat16)


def _mixer_kernel(x_ref, x_next_ref, h0_ref, g1_ref, w_in_hbm, w_pool_ref, pscale_ref, convw_ref, w_out_hbm,
                  g2_ref, w_r_ref, b_r_ref,
                  x1_ref, h2_ref, route_ref,
                  ext_u, ext_z, h_buf, w_in_ref, w_out_ref, land, wsem):
    i = pl.program_id(0)

    @pl.when(i == 0)
    def _():
        ext_u[0:HIST, :] = jnp.zeros((HIST, MIX_A), jnp.float32)
        ext_z[0:HIST, :] = jnp.zeros((HIST, MIX_B), jnp.float32)
        h_buf[...] = h0_ref[...]
        _load_cast_weight(w_in_hbm, w_in_ref, land, wsem)
        _load_cast_weight(w_out_hbm, w_out_ref, land, wsem)

    x = x_ref[...]
    h = h_buf[...]

    ext_u[HIST:HIST + TM, :] = _bdot(h, w_in_ref[:, 0:MIX_A])
    c_gate = _bdot(h, w_in_ref[:, MIX_A + MIX_B:MIX_A + 2 * MIX_B])
    v = _bdot(h, w_in_ref[:, MIX_A + 2 * MIX_B:MIX_A + 3 * MIX_B])
    z = c_gate * v
    ext_z[HIST:HIST + TM, :] = z
    b_gate = _bdot(h, w_in_ref[:, MIX_A:MIX_A + MIX_B])

    y = ext_z[HIST - 2:HIST - 2 + TM, :] * convw_ref[0:1, :]
    y = y + ext_z[HIST - 1:HIST - 1 + TM, :] * convw_ref[1:2, :]
    y = y + z * convw_ref[2:3, :]
    y_b = b_gate * y
    ext_z[0:HIST, :] = ext_z[TM:TM + HIST, :]
    out_b = _bdot(y_b.astype(jnp.bfloat16), w_out_ref[MIX_A:MIX_A + MIX_B, :])

    row = lax.broadcasted_iota(jnp.int32, (TM, 1), 0) + i * TM + 1
    y_a = []
    for gi, w in enumerate(POOL_WINDOWS):
        c0 = gi * POOL_CH
        u = ext_u[HIST:HIST + TM, c0:c0 + POOL_CH]
        acc = u
        for s in range(1, w):
            acc = acc + ext_u[HIST - s:HIST - s + TM, c0:c0 + POOL_CH]
        cnt = jnp.minimum(row, w).astype(jnp.float32)
        pooled = (acc / cnt - u).astype(jnp.bfloat16)
        y_a.append(_bdot(pooled, w_pool_ref[gi]))
    y_a = jnp.concatenate(y_a, axis=-1) * pscale_ref[...]
    ext_u[0:HIST, :] = ext_u[TM:TM + HIST, :]
    out_a = _bdot(y_a.astype(jnp.bfloat16), w_out_ref[0:MIX_A, :])

    x1 = x + (out_a + out_b)
    x1_ref[...] = x1

    h2_f32 = _rms(x1, g2_ref[...])
    _store_rows(h2_ref, h2_f32)
    h2 = h2_f32.astype(jnp.bfloat16)

    logits = _bdot(h2, w_r_ref[...]) + b_r_ref[...]
    lane = lax.broadcasted_iota(jnp.int32, (TM, LANES), 1)
    lane_f = lane.astype(jnp.float32)

    def first_argmax(vals, vmax):
        return jnp.min(jnp.where(vals == vmax, lane_f, float(LANES)), axis=-1, keepdims=True)

    gl = jnp.where(lane < N_GROUPS, logits, _NEG)
    gmax = jnp.max(gl, axis=-1, keepdims=True)
    g_w = 1.0 / jnp.sum(jnp.exp(gl - gmax), axis=-1, keepdims=True)
    grp = first_argmax(gl, gmax).astype(jnp.int32)
    lo = N_GROUPS + E_PER_GROUP * grp
    el = jnp.where((lane >= lo) & (lane < lo + E_PER_GROUP), logits, _NEG)
    emax = jnp.max(el, axis=-1, keepdims=True)
    idx1 = first_argmax(el, emax)
    esum = jnp.sum(jnp.exp(el - emax), axis=-1, keepdims=True)
    el2 = jnp.where(lane_f == idx1, _NEG, el)
    e2max = jnp.max(el2, axis=-1, keepdims=True)
    idx2 = first_argmax(el2, e2max)
    p1 = 1.0 / esum
    p2 = jnp.exp(e2max - emax) / esum
    tot = p1 + p2
    w1 = g_w * (p1 / tot)
    w2 = g_w * (p2 / tot)
    route = jnp.where(lane == 0, w1, 0.0)
    route = jnp.where(lane == 1, w2, route)
    route = jnp.where(lane == 2, idx1 - N_GROUPS, route)
    route = jnp.where(lane == 3, idx2 - N_GROUPS, route)
    route_ref[...] = route

    h_buf[...] = _rms(x_next_ref[...], g1_ref[...]).astype(jnp.bfloat16)


def _first_tile_norm_kernel(x_ref, g_ref, o_ref):
    o_ref[...] = _rms(x_ref[...], g_ref[...]).astype(jnp.bfloat16)


def _first_tile_norm(x, g):
    return pl.pallas_call(
        _first_tile_norm_kernel,
        grid=(1,),
        in_specs=[pl.BlockSpec((TM, D_MODEL), lambda i: (0, 0)),
                  pl.BlockSpec((1, D_MODEL), lambda i: (0, 0))],
        out_specs=pl.BlockSpec((TM, D_MODEL), lambda i: (0, 0)),
        out_shape=jax.ShapeDtypeStruct((TM, D_MODEL), jnp.bfloat16),
        name="first_tile_norm",
    )(x, g)


def _mixer(x, g1, w_in, w_pool, pscale, convw, w_out, g2, w_r, b_r):
    t = x.shape[0]
    h0 = _first_tile_norm(x, g1)
    const = lambda shape: pl.BlockSpec(shape, lambda i: (0,) * len(shape),
                                       pipeline_mode=pl.Buffered(1))
    return pl.pallas_call(
        _mixer_kernel,
        grid=(t // TM,),
        in_specs=[
            pl.BlockSpec((TM, D_MODEL), lambda i: (i, 0)),
            pl.BlockSpec((TM, D_MODEL), lambda i: (jnp.minimum(i + 1, t // TM - 1), 0)),
            const((TM, D_MODEL)),
            const((1, D_MODEL)),
            pl.BlockSpec(memory_space=pl.ANY),
            const(w_pool.shape),
            const((1, MIX_A)),
            const((CONV_W, MIX_B)),
            pl.BlockSpec(memory_space=pl.ANY),
            const((1, D_MODEL)),
            const(w_r.shape),
            const((1, LANES)),
        ],
        out_specs=[
            pl.BlockSpec((TM, D_MODEL), lambda i: (i, 0)),
            pl.BlockSpec((TM * ROW_SUB, LANES), lambda i: (i, 0)),
            pl.BlockSpec((TM, LANES), lambda i: (i, 0)),
        ],
        out_shape=[
            jax.ShapeDtypeStruct((t, D_MODEL), jnp.float32),
            jax.ShapeDtypeStruct((t * ROW_SUB, LANES), jnp.bfloat16),
            jax.ShapeDtypeStruct((t, LANES), jnp.float32),
        ],
        scratch_shapes=[
            pltpu.VMEM((TM + HIST, MIX_A), jnp.float32),
            pltpu.VMEM((TM + HIST, MIX_B), jnp.float32),
            pltpu.VMEM((TM, D_MODEL), jnp.bfloat16),
            pltpu.VMEM(w_in.shape, jnp.bfloat16),
            pltpu.VMEM(w_out.shape, jnp.bfloat16),
            pltpu.VMEM((2, WEIGHT_SLAB, w_in.shape[1]), jnp.float32),
            pltpu.SemaphoreType.DMA((2,)),
        ],
        compiler_params=pltpu.CompilerParams(
            dimension_semantics=("arbitrary",), vmem_limit_bytes=VMEM_LIMIT),
        name="mixer",
    )(x, x, h0, g1, w_in, w_pool, pscale, convw, w_out, g2, w_r, b_r)


def _row_copy(src_hbm, row_start, dst_ref, r, sem):
    dst_start = r * ROW_SUB if isinstance(r, int) else pl.multiple_of(r * ROW_SUB, ROW_SUB)
    return pltpu.make_async_copy(src_hbm.at[pl.ds(pl.multiple_of(row_start, ROW_SUB), ROW_SUB)],
                                 dst_ref.at[pl.ds(dst_start, ROW_SUB)], sem)


def _gather_rows_issued(n_rows):
    if isinstance(n_rows, int):
        return n_rows
    return ((n_rows + GATHER_UNROLL - 1) >> GATHER_SHIFT) << GATHER_SHIFT


def _start_row_gather(src_hbm, idx_ref, base, n_rows, dst_ref, sem):
    if isinstance(n_rows, int):
        for r in range(n_rows):
            _row_copy(src_hbm, idx_ref[base + r], dst_ref, r, sem).start(priority=r % DMA_QUEUES)
        return

    def body(i, carry):
        for u in range(GATHER_UNROLL):
            r = i * GATHER_UNROLL + u
            src_row = idx_ref[base + jnp.minimum(r, n_rows - 1)]
            _row_copy(src_hbm, src_row, dst_ref, r, sem).start(priority=u % DMA_QUEUES)
        return carry

    lax.fori_loop(0, _gather_rows_issued(n_rows) >> GATHER_SHIFT, body, 0)


def _wait_row_gather(src_hbm, dst_ref, sem, n_rows):
    n = _gather_rows_issued(n_rows) * ROW_SUB
    pltpu.make_async_copy(src_hbm.at[pl.ds(0, n)], dst_ref.at[pl.ds(0, n)], sem).wait()


def _weight_copies(w_hbm, e, buf, slot, k, sem):
    rows = w_hbm.shape[1] // WEIGHT_DMA_CHUNKS
    return [pltpu.make_async_copy(w_hbm.at[e, pl.ds(c * rows, rows)],
                                  buf.at[slot, k, pl.ds(c * rows, rows)], sem.at[slot])
            for c in range(WEIGHT_DMA_CHUNKS)]


def _start_weight(*args):
    for cp in _weight_copies(*args):
        cp.start(priority=WEIGHT_DMA_PRIORITY)


def _wait_weight(*args):
    for cp in _weight_copies(*args):
        cp.wait()


def _build_row_sources(dest_ref, row_ref):
    n_tokens = dest_ref.shape[0] // TOP_K

    def scatter(i, carry):
        for u in range(SCALAR_UNROLL):
            tok = i * SCALAR_UNROLL + u
            for k in range(TOP_K):
                row_ref[dest_ref[k * n_tokens + tok]] = tok * ROW_SUB
        return carry

    lax.fori_loop(0, n_tokens // SCALAR_UNROLL, scatter, 0)


def _gate_up_kernel(blk_e_ref, first_ref, wslot_ref, nxt_e_ref, n_used_ref, n_valid_ref, dest_ref,
                    h2_hbm, wg_hbm, wu_hbm, hid_ref,
                    xg, wbuf, row_ref, gsem, wsem):
    b = pl.program_id(0)
    n_used = n_used_ref[0]
    e = blk_e_ref[b]
    slot = wslot_ref[b]

    @pl.when(b == 0)
    def _():
        _start_weight(wg_hbm, e, wbuf, slot, 0, wsem)
        _start_weight(wu_hbm, e, wbuf, slot, 1, wsem)
        xg[...] = jnp.zeros_like(xg)
        _build_row_sources(dest_ref, row_ref)
        _start_row_gather(h2_hbm, row_ref, 0, n_valid_ref[0], xg.at[0], gsem.at[0])

    @pl.when(b < n_used)
    def _():
        cur = b % 2

        @pl.when(b + 1 < n_used)
        def _():
            _start_row_gather(h2_hbm, row_ref, (b + 1) * ROW_BLK, n_valid_ref[b + 1],
                              xg.at[1 - cur], gsem.at[1 - cur])

        @pl.when(first_ref[b] == 1)
        def _():
            _wait_weight(wg_hbm, e, wbuf, slot, 0, wsem)
            _wait_weight(wu_hbm, e, wbuf, slot, 1, wsem)
            nxt = nxt_e_ref[b]

            @pl.when(nxt != e)
            def _():
                _start_weight(wg_hbm, nxt, wbuf, 1 - slot, 0, wsem)
                _start_weight(wu_hbm, nxt, wbuf, 1 - slot, 1, wsem)


        _wait_row_gather(h2_hbm, xg.at[cur], gsem.at[cur], n_valid_ref[b])
        xb = _load_rows(xg.at[cur], ROW_BLK)
        gate = _bdot(xb, wbuf[slot, 0].astype(jnp.bfloat16))
        up = _bdot(xb, wbuf[slot, 1].astype(jnp.bfloat16))
        hid_ref[...] = (gate * jax.nn.sigmoid(gate) * up).astype(jnp.bfloat16)

    @pl.when(b >= n_used)
    def _():
        hid_ref[...] = jnp.zeros_like(hid_ref)


def _gate_up(plan, h2, wg, wu):
    n_blocks = plan["blk_e"].shape[0]
    any_spec = pl.BlockSpec(memory_space=pl.ANY)
    return pl.pallas_call(
        _gate_up_kernel,
        grid_spec=pltpu.PrefetchScalarGridSpec(
            num_scalar_prefetch=7,
            grid=(n_blocks,),
            in_specs=[any_spec, any_spec, any_spec],
            out_specs=pl.BlockSpec((ROW_BLK, D_EXPERT), lambda b, *_: (b, 0)),
            scratch_shapes=[
                pltpu.VMEM((2, ROW_BLK * ROW_SUB, LANES), jnp.bfloat16),
                pltpu.VMEM((2, 2, D_MODEL, D_EXPERT), jnp.float32),
                pltpu.SMEM((n_blocks * ROW_BLK,), jnp.int32),
                pltpu.SemaphoreType.DMA((2,)),
                pltpu.SemaphoreType.DMA((2,)),
            ],
        ),
        out_shape=jax.ShapeDtypeStruct((n_blocks * ROW_BLK, D_EXPERT), jnp.bfloat16),
        compiler_params=pltpu.CompilerParams(
            dimension_semantics=("arbitrary",), vmem_limit_bytes=VMEM_LIMIT),
        name="gate_up",
    )(plan["blk_e"], plan["first"], plan["wslot"], plan["nxt_e"], plan["n_used"], plan["n_valid"],
      plan["dest"], h2, wg, wu)


def _down_kernel(blk_e_ref, first_ref, wslot_ref, nxt_e_ref, n_used_ref,
                 hid_ref, wd_hbm, y_ref, wbuf, wsem):
    b = pl.program_id(0)
    n_used = n_used_ref[0]
    e = blk_e_ref[b]
    slot = wslot_ref[b]

    @pl.when(b == 0)
    def _():
        _start_weight(wd_hbm, e, wbuf, slot, 0, wsem)

    @pl.when(b < n_used)
    def _():
        @pl.when(first_ref[b] == 1)
        def _():
            _wait_weight(wd_hbm, e, wbuf, slot, 0, wsem)
            nxt = nxt_e_ref[b]

            @pl.when(nxt != e)
            def _():
                _start_weight(wd_hbm, nxt, wbuf, 1 - slot, 0, wsem)


        _store_rows(y_ref, _bdot(hid_ref[...], wbuf[slot, 0].astype(jnp.bfloat16)))

    @pl.when(b >= n_used)
    def _():
        y_ref[...] = jnp.zeros_like(y_ref)


def _down(plan, hid, wd):
    n_blocks = plan["blk_e"].shape[0]
    return pl.pallas_call(
        _down_kernel,
        grid_spec=pltpu.PrefetchScalarGridSpec(
            num_scalar_prefetch=5,
            grid=(n_blocks,),
            in_specs=[pl.BlockSpec((ROW_BLK, D_EXPERT), lambda b, *_: (b, 0)),
                      pl.BlockSpec(memory_space=pl.ANY)],
            out_specs=pl.BlockSpec((ROW_BLK * ROW_SUB, LANES), lambda b, *_: (b, 0)),
            scratch_shapes=[
                pltpu.VMEM((2, 1, D_EXPERT, D_MODEL), jnp.float32),
                pltpu.SemaphoreType.DMA((2,)),
            ],
        ),
        out_shape=jax.ShapeDtypeStruct((n_blocks * ROW_BLK * ROW_SUB, LANES), jnp.bfloat16),
        compiler_params=pltpu.CompilerParams(
            dimension_semantics=("arbitrary",), vmem_limit_bytes=VMEM_LIMIT),
        name="down",
    )(plan["blk_e"], plan["first"], plan["wslot"], plan["nxt_e"], plan["n_used"], hid, wd)


def _combine_kernel(dest_ref, x1_ref, route_ref, g_ref, y_hbm, o_ref, y0_buf, y1_buf, sem):
    i = pl.program_id(0)
    n = pl.num_programs(0)
    cur = i % 2
    n_tokens = n * CTM

    def start(step, slot):
        _start_row_gather(y_hbm, dest_ref, step * CTM, CTM, y0_buf.at[slot], sem.at[0, slot])
        _start_row_gather(y_hbm, dest_ref, n_tokens + step * CTM, CTM, y1_buf.at[slot], sem.at[1, slot])

    @pl.when(i == 0)
    def _():
        start(0, 0)

    @pl.when(i + 1 < n)
    def _():
        start(i + 1, 1 - cur)

    r = route_ref[...]
    _wait_row_gather(y_hbm, y0_buf.at[cur], sem.at[0, cur], CTM)
    _wait_row_gather(y_hbm, y1_buf.at[cur], sem.at[1, cur], CTM)
    y0 = _load_rows(y0_buf.at[cur], CTM).astype(jnp.float32)
    y1 = _load_rows(y1_buf.at[cur], CTM).astype(jnp.float32)
    xo = x1_ref[...] + (y0 * r[:, 0:1] + y1 * r[:, 1:2])
    o_ref[...] = _rms(xo, g_ref[...])


def _combine(dest_rows, x1, route, g, ybuf):
    t = x1.shape[0]
    tile = lambda w: pl.BlockSpec((CTM, w), lambda i, *_: (i, 0))
    return pl.pallas_call(
        _combine_kernel,
        grid_spec=pltpu.PrefetchScalarGridSpec(
            num_scalar_prefetch=1,
            grid=(t // CTM,),
            in_specs=[tile(D_MODEL), tile(LANES),
                      pl.BlockSpec((1, D_MODEL), lambda i, *_: (0, 0)),
                      pl.BlockSpec(memory_space=pl.ANY)],
            out_specs=tile(D_MODEL),
            scratch_shapes=[pltpu.VMEM((2, CTM * ROW_SUB, LANES), jnp.bfloat16),
                            pltpu.VMEM((2, CTM * ROW_SUB, LANES), jnp.bfloat16),
                            pltpu.SemaphoreType.DMA((2, 2))],
        ),
        out_shape=jax.ShapeDtypeStruct((t, D_MODEL), jnp.float32),
        compiler_params=pltpu.CompilerParams(
            dimension_semantics=("arbitrary",), vmem_limit_bytes=VMEM_LIMIT),
        name="combine",
    )(dest_rows, x1, route, g, ybuf)


def _dispatch_plan(e_flat):
    a = e_flat.shape[0]
    i32 = jnp.int32
    experts = jnp.arange(N_EXPERTS, dtype=i32)
    onehot = (e_flat[:, None] == experts[None, :]).astype(i32)
    rank = jnp.cumsum(onehot, axis=0) - onehot
    counts = jnp.sum(onehot, axis=0)
    padded = ((counts + ROW_BLK - 1) // ROW_BLK) * ROW_BLK
    pends = jnp.cumsum(padded)
    pstarts = pends - padded
    dest = jnp.sum((rank + pstarts[None, :]) * onehot, axis=1).astype(i32)
    n_blocks = (a + N_EXPERTS * (ROW_BLK - 1) + ROW_BLK - 1) // ROW_BLK
    blk_start = jnp.arange(n_blocks, dtype=i32) * ROW_BLK
    blk_e = jnp.clip(jnp.sum(pends[None, :] <= blk_start[:, None], axis=1), 0, N_EXPERTS - 1).astype(i32)
    first = jnp.concatenate([jnp.ones((1,), i32), (blk_e[1:] != blk_e[:-1]).astype(i32)])
    wslot = (jnp.cumsum(first) - 1) % 2
    later_used = (experts[None, :] > experts[:, None]) & (counts[None, :] > 0)
    nxt = jnp.min(jnp.where(later_used, experts[None, :], N_EXPERTS), axis=1)
    nxt = jnp.where(nxt == N_EXPERTS, experts, nxt)
    return {
        "dest": dest,
        "blk_e": blk_e,
        "first": first,
        "wslot": wslot.astype(i32),
        "nxt_e": nxt[blk_e].astype(i32),
        "n_used": (pends[-1] // ROW_BLK).astype(i32).reshape(1),
        "n_valid": jnp.clip(counts[blk_e] - (blk_start - pstarts[blk_e]), 0, ROW_BLK).astype(i32),
    }


def kernel(x, norm_mix_g, w_in, w_pool, pool_scale, conv_w, w_out, norm_ffn_g, w_router_group, b_router_group, w_router_expert, b_router_expert, w_gate, w_up, w_down, norm_final_g):
    bt, s, d = x.shape
    t = bt * s
    bf = jnp.bfloat16
    xt = x.reshape(t, d)
    assert norm_mix_g.shape[0] == 1, "the final norm is fused into the last layer's combine"
    for l in range(norm_mix_g.shape[0]):
        w_r = jnp.concatenate([w_router_group[l], w_router_expert[l]], axis=1)
        w_r = jnp.pad(w_r, ((0, 0), (0, LANES - w_r.shape[1]))).astype(bf)
        b_r = jnp.concatenate([b_router_group[l], b_router_expert[l]])
        b_r = jnp.pad(b_r, (0, LANES - b_r.shape[0])).reshape(1, LANES)
        x1, h2, route = _mixer(
            xt, norm_mix_g[l].reshape(1, d), w_in[l], w_pool[l].astype(bf),
            pool_scale[l].reshape(1, MIX_A), conv_w[l].T, w_out[l],
            norm_ffn_g[l].reshape(1, d), w_r, b_r)
        e_flat = jnp.concatenate([route[:, 2 + k] for k in range(TOP_K)]).astype(jnp.int32)
        plan = _dispatch_plan(e_flat)
        hid = _gate_up(plan, h2, w_gate[l], w_up[l])
        ybuf = _down(plan, hid, w_down[l])
        xt = _combine(plan["dest"] * ROW_SUB, x1, route, norm_final_g.reshape(1, d), ybuf)
    return xt.reshape(bt, s, d)
```

```python
import jax
import jax.numpy as jnp
from jax import lax
from jax.experimental import pallas as pl
from jax.experimental.pallas import tpu as pltpu

D_MODEL = 2048
MIX_A = 1024
MIX_B = 1024
POOL_WINDOWS = (2, 4, 8, 16)
POOL_CH = MIX_A // len(POOL_WINDOWS)
CONV_W = 3
N_GROUPS = 4
E_PER_GROUP = 8
N_EXPERTS = N_GROUPS * E_PER_GROUP
TOP_K = 2
D_EXPERT = D_MODEL // 2
EPS = 1e-6

LANES = 128
ROW_SUB = D_MODEL // LANES
HIST = 16
TM = 256
CTM = 256
ROW_BLK = 256
VMEM_LIMIT = 56 * 1024 * 1024
WEIGHT_DMA_PRIORITY = 1
WEIGHT_DMA_CHUNKS = 8
WEIGHT_SLAB = 128
DMA_QUEUES = 2
GATHER_SHIFT = 4
GATHER_UNROLL = 1 << GATHER_SHIFT
SCALAR_UNROLL = 16

_NEG = -1e30


def _rms(x, g):
    return x * lax.rsqrt(jnp.mean(x * x, axis=-1, keepdims=True) + EPS) * g


def _bdot(a, b):
    return jnp.dot(a, b, preferred_element_type=jnp.float32)


def _store_rows(ref, val):
    ref[...] = val.astype(jnp.bfloat16).reshape(val.shape[0] * ROW_SUB, LANES)


def _load_rows(ref, n):
    return ref[...].reshape(n, D_MODEL)


def _load_cast_weight(w_hbm, w_bf, land, sem):
    rows, cols = w_hbm.shape
    n_slabs = rows // WEIGHT_SLAB

    def slab_copy(c):
        return pltpu.make_async_copy(w_hbm.at[pl.ds(c * WEIGHT_SLAB, WEIGHT_SLAB)],
                                     land.at[c % 2, :, pl.ds(0, cols)], sem.at[c % 2])

    slab_copy(0).start()
    for c in range(n_slabs):
        if c + 1 < n_slabs:
            slab_copy(c + 1).start()
        slab_copy(c).wait()
        w_bf[pl.ds(c * WEIGHT_SLAB, WEIGHT_SLAB), :] = land[c % 2, :, 0:cols].astype(jnp.bfloat16)


def _mixer_kernel(x_ref, x_next_ref, h0_ref, g1_ref, w_in_hbm, w_pool_ref, pscale_ref, convw_ref, w_out_hbm,
                  g2_ref, w_r_ref, b_r_ref,
                  x1_ref, h2_ref, route_ref,
                  ext_u, ext_z, h_buf, w_in_ref, w_out_ref, land, wsem):
    i = pl.program_id(0)

    @pl.when(i == 0)
    def _():
        ext_u[0:HIST, :] = jnp.zeros((HIST, MIX_A), jnp.float32)
        ext_z[0:HIST, :] = jnp.zeros((HIST, MIX_B), jnp.float32)
        h_buf[...] = h0_ref[...]
        _load_cast_weight(w_in_hbm, w_in_ref, land, wsem)
        _load_cast_weight(w_out_hbm, w_out_ref, land, wsem)

    x = x_ref[...]
    h = h_buf[...]

    ext_u[HIST:HIST + TM, :] = _bdot(h, w_in_ref[:, 0:MIX_A])
    c_gate = _bdot(h, w_in_ref[:, MIX_A + MIX_B:MIX_A + 2 * MIX_B])
    v = _bdot(h, w_in_ref[:, MIX_A + 2 * MIX_B:MIX_A + 3 * MIX_B])
    z = c_gate * v
    ext_z[HIST:HIST + TM, :] = z
    b_gate = _bdot(h, w_in_ref[:, MIX_A:MIX_A + MIX_B])

    y = ext_z[HIST - 2:HIST - 2 + TM, :] * convw_ref[0:1, :]
    y = y + ext_z[HIST - 1:HIST - 1 + TM, :] * convw_ref[1:2, :]
    y = y + z * convw_ref[2:3, :]
    y_b = b_gate * y
    ext_z[0:HIST, :] = ext_z[TM:TM + HIST, :]
    out_b = _bdot(y_b.astype(jnp.bfloat16), w_out_ref[MIX_A:MIX_A + MIX_B, :])

    row = lax.broadcasted_iota(jnp.int32, (TM, 1), 0) + i * TM + 1
    y_a = []
    for gi, w in enumerate(POOL_WINDOWS):
        c0 = gi * POOL_CH
        u = ext_u[HIST:HIST + TM, c0:c0 + POOL_CH]
        acc = u
        for s in range(1, w):
            acc = acc + ext_u[HIST - s:HIST - s + TM, c0:c0 + POOL_CH]
        cnt = jnp.minimum(row, w).astype(jnp.float32)
        pooled = (acc / cnt - u).astype(jnp.bfloat16)
        y_a.append(_bdot(pooled, w_pool_ref[gi]))
    y_a = jnp.concatenate(y_a, axis=-1) * pscale_ref[...]
    ext_u[0:HIST, :] = ext_u[TM:TM + HIST, :]
    out_a = _bdot(y_a.astype(jnp.bfloat16), w_out_ref[0:MIX_A, :])

    x1 = x + (out_a + out_b)
    x1_ref[...] = x1

    h2_f32 = _rms(x1, g2_ref[...])
    _store_rows(h2_ref, h2_f32)
    h2 = h2_f32.astype(jnp.bfloat16)

    logits = _bdot(h2, w_r_ref[...]) + b_r_ref[...]
    lane = lax.broadcasted_iota(jnp.int32, (TM, LANES), 1)
    lane_f = lane.astype(jnp.float32)

    def first_argmax(vals, vmax):
        return jnp.min(jnp.where(vals == vmax, lane_f, float(LANES)), axis=-1, keepdims=True)

    gl = jnp.where(lane < N_GROUPS, logits, _NEG)
    gmax = jnp.max(gl, axis=-1, keepdims=True)
    g_w = 1.0 / jnp.sum(jnp.exp(gl - gmax), axis=-1, keepdims=True)
    grp = first_argmax(gl, gmax).astype(jnp.int32)
    lo = N_GROUPS + E_PER_GROUP * grp
    el = jnp.where((lane >= lo) & (lane < lo + E_PER_GROUP), logits, _NEG)
    emax = jnp.max(el, axis=-1, keepdims=True)
    idx1 = first_argmax(el, emax)
    esum = jnp.sum(jnp.exp(el - emax), axis=-1, keepdims=True)
    el2 = jnp.where(lane_f == idx1, _NEG, el)
    e2max = jnp.max(el2, axis=-1, keepdims=True)
    idx2 = first_argmax(el2, e2max)
    p1 = 1.0 / esum
    p2 = jnp.exp(e2max - emax) / esum
    tot = p1 + p2
    w1 = g_w * (p1 / tot)
    w2 = g_w * (p2 / tot)
    route = jnp.where(lane == 0, w1, 0.0)
    route = jnp.where(lane == 1, w2, route)
    route = jnp.where(lane == 2, idx1 - N_GROUPS, route)
    route = jnp.where(lane == 3, idx2 - N_GROUPS, route)
    route_ref[...] = route

    h_buf[...] = _rms(x_next_ref[...], g1_ref[...]).astype(jnp.bfloat16)


def _first_tile_norm_kernel(x_ref, g_ref, o_ref):
    o_ref[...] = _rms(x_ref[...], g_ref[...]).astype(jnp.bfloat16)


def _first_tile_norm(x, g):
    return pl.pallas_call(
        _first_tile_norm_kernel,
        grid=(1,),
        in_specs=[pl.BlockSpec((TM, D_MODEL), lambda i: (0, 0)),
                  pl.BlockSpec((1, D_MODEL), lambda i: (0, 0))],
        out_specs=pl.BlockSpec((TM, D_MODEL), lambda i: (0, 0)),
        out_shape=jax.ShapeDtypeStruct((TM, D_MODEL), jnp.bfloat16),
        name="first_tile_norm",
    )(x, g)


def _mixer(x, g1, w_in, w_pool, pscale, convw, w_out, g2, w_r, b_r):
    t = x.shape[0]
    h0 = _first_tile_norm(x, g1)
    const = lambda shape: pl.BlockSpec(shape, lambda i: (0,) * len(shape),
                                       pipeline_mode=pl.Buffered(1))
    return pl.pallas_call(
        _mixer_kernel,
        grid=(t // TM,),
        in_specs=[
            pl.BlockSpec((TM, D_MODEL), lambda i: (i, 0)),
            pl.BlockSpec((TM, D_MODEL), lambda i: (jnp.minimum(i + 1, t // TM - 1), 0)),
            const((TM, D_MODEL)),
            const((1, D_MODEL)),
            pl.BlockSpec(memory_space=pl.ANY),
            const(w_pool.shape),
            const((1, MIX_A)),
            const((CONV_W, MIX_B)),
            pl.BlockSpec(memory_space=pl.ANY),
            const((1, D_MODEL)),
            const(w_r.shape),
            const((1, LANES)),
        ],
        out_specs=[
            pl.BlockSpec((TM, D_MODEL), lambda i: (i, 0)),
            pl.BlockSpec((TM * ROW_SUB, LANES), lambda i: (i, 0)),
            pl.BlockSpec((TM, LANES), lambda i: (i, 0)),
        ],
        out_shape=[
            jax.ShapeDtypeStruct((t, D_MODEL), jnp.float32),
            jax.ShapeDtypeStruct((t * ROW_SUB, LANES), jnp.bfloat16),
            jax.ShapeDtypeStruct((t, LANES), jnp.float32),
        ],
        scratch_shapes=[
            pltpu.VMEM((TM + HIST, MIX_A), jnp.float32),
            pltpu.VMEM((TM + HIST, MIX_B), jnp.float32),
            pltpu.VMEM((TM, D_MODEL), jnp.bfloat16),
            pltpu.VMEM(w_in.shape, jnp.bfloat16),
            pltpu.VMEM(w_out.shape, jnp.bfloat16),
            pltpu.VMEM((2, WEIGHT_SLAB, w_in.shape[1]), jnp.float32),
            pltpu.SemaphoreType.DMA((2,)),
        ],
        compiler_params=pltpu.CompilerParams(
            dimension_semantics=("arbitrary",), vmem_limit_bytes=VMEM_LIMIT),
        name="mixer",
    )(x, x, h0, g1, w_in, w_pool, pscale, convw, w_out, g2, w_r, b_r)


def _row_copy(src_hbm, row_start, dst_ref, r, sem):
    dst_start = r * ROW_SUB if isinstance(r, int) else pl.multiple_of(r * ROW_SUB, ROW_SUB)
    return pltpu.make_async_copy(src_hbm.at[pl.ds(pl.multiple_of(row_start, ROW_SUB), ROW_SUB)],
                                 dst_ref.at[pl.ds(dst_start, ROW_SUB)], sem)


def _gather_rows_issued(n_rows):
    if isinstance(n_rows, int):
        return n_rows
    return ((n_rows + GATHER_UNROLL - 1) >> GATHER_SHIFT) << GATHER_SHIFT


def _start_row_gather(src_hbm, idx_ref, base, n_rows, dst_ref, sem):
    if isinstance(n_rows, int):
        for r in range(n_rows):
            _row_copy(src_hbm, idx_ref[base + r], dst_ref, r, sem).start(priority=r % DMA_QUEUES)
        return

    def body(i, carry):
        for u in range(GATHER_UNROLL):
            r = i * GATHER_UNROLL + u
            src_row = idx_ref[base + jnp.minimum(r, n_rows - 1)]
            _row_copy(src_hbm, src_row, dst_ref, r, sem).start(priority=u % DMA_QUEUES)
        return carry

    lax.fori_loop(0, _gather_rows_issued(n_rows) >> GATHER_SHIFT, body, 0)


def _wait_row_gather(src_hbm, dst_ref, sem, n_rows):
    n = _gather_rows_issued(n_rows) * ROW_SUB
    pltpu.make_async_copy(src_hbm.at[pl.ds(0, n)], dst_ref.at[pl.ds(0, n)], sem).wait()


def _weight_copies(w_hbm, e, buf, slot, k, sem):
    rows = w_hbm.shape[1] // WEIGHT_DMA_CHUNKS
    return [pltpu.make_async_copy(w_hbm.at[e, pl.ds(c * rows, rows)],
                                  buf.at[slot, k, pl.ds(c * rows, rows)], sem.at[slot])
            for c in range(WEIGHT_DMA_CHUNKS)]


def _start_weight(*args):
    for cp in _weight_copies(*args):
        cp.start(priority=WEIGHT_DMA_PRIORITY)


def _wait_weight(*args):
    for cp in _weight_copies(*args):
        cp.wait()


def _build_row_sources(dest_ref, row_ref):
    n_tokens = dest_ref.shape[0] // TOP_K

    def scatter(i, carry):
        for u in range(SCALAR_UNROLL):
            tok = i * SCALAR_UNROLL + u
            for k in range(TOP_K):
                row_ref[dest_ref[k * n_tokens + tok]] = tok * ROW_SUB
        return carry

    lax.fori_loop(0, n_tokens // SCALAR_UNROLL, scatter, 0)


def _gate_up_kernel(blk_e_ref, first_ref, wslot_ref, nxt_e_ref, n_used_ref, n_valid_ref, dest_ref,
                    h2_hbm, wg_hbm, wu_hbm, hid_ref,
                    xg, wbuf, row_ref, gsem, wsem):
    b = pl.program_id(0)
    n_used = n_used_ref[0]
    e = blk_e_ref[b]
    slot = wslot_ref[b]

    @pl.when(b == 0)
    def _():
        _start_weight(wg_hbm, e, wbuf, slot, 0, wsem)
        _start_weight(wu_hbm, e, wbuf, slot, 1, wsem)
        xg[...] = jnp.zeros_like(xg)
        _build_row_sources(dest_ref, row_ref)
        _start_row_gather(h2_hbm, row_ref, 0, n_valid_ref[0], xg.at[0], gsem.at[0])

    @pl.when(b < n_used)
    def _():
        cur = b % 2

        @pl.when(b + 1 < n_used)
        def _():
            _start_row_gather(h2_hbm, row_ref, (b + 1) * ROW_BLK, n_valid_ref[b + 1],
                              xg.at[1 - cur], gsem.at[1 - cur])

        @pl.when(first_ref[b] == 1)
        def _():
            _wait_weight(wg_hbm, e, wbuf, slot, 0, wsem)
            _wait_weight(wu_hbm, e, wbuf, slot, 1, wsem)
            nxt = nxt_e_ref[b]

            @pl.when(nxt != e)
            def _():
                _start_weight(wg_hbm, nxt, wbuf, 1 - slot, 0, wsem)
                _start_weight(wu_hbm, nxt, wbuf, 1 - slot, 1, wsem)


        _wait_row_gather(h2_hbm, xg.at[cur], gsem.at[cur], n_valid_ref[b])
        xb = _load_rows(xg.at[cur], ROW_BLK)
        gate = _bdot(xb, wbuf[slot, 0].astype(jnp.bfloat16))
        up = _bdot(xb, wbuf[slot, 1].astype(jnp.bfloat16))
        hid_ref[...] = (gate * jax.nn.sigmoid(gate) * up).astype(jnp.bfloat16)

    @pl.when(b >= n_used)
    def _():
        hid_ref[...] = jnp.zeros_like(hid_ref)


def _gate_up(plan, h2, wg, wu):
    n_blocks = plan["blk_e"].shape[0]
    any_spec = pl.BlockSpec(memory_space=pl.ANY)
    return pl.pallas_call(
        _gate_up_kernel,
        grid_spec=pltpu.PrefetchScalarGridSpec(
            num_scalar_prefetch=7,
            grid=(n_blocks,),
            in_specs=[any_spec, any_spec, any_spec],
            out_specs=pl.BlockSpec((ROW_BLK, D_EXPERT), lambda b, *_: (b, 0)),
            scratch_shapes=[
                pltpu.VMEM((2, ROW_BLK * ROW_SUB, LANES), jnp.bfloat16),
                pltpu.VMEM((2, 2, D_MODEL, D_EXPERT), jnp.float32),
                pltpu.SMEM((n_blocks * ROW_BLK,), jnp.int32),
                pltpu.SemaphoreType.DMA((2,)),
                pltpu.SemaphoreType.DMA((2,)),
            ],
        ),
        out_shape=jax.ShapeDtypeStruct((n_blocks * ROW_BLK, D_EXPERT), jnp.bfloat16),
        compiler_params=pltpu.CompilerParams(
            dimension_semantics=("arbitrary",), vmem_limit_bytes=VMEM_LIMIT),
        name="gate_up",
    )(plan["blk_e"], plan["first"], plan["wslot"], plan["nxt_e"], plan["n_used"], plan["n_valid"],
      plan["dest"], h2, wg, wu)


def _down_kernel(blk_e_ref, first_ref, wslot_ref, nxt_e_ref, n_used_ref,
                 hid_ref, wd_hbm, y_ref, wbuf, wsem):
    b = pl.program_id(0)
    n_used = n_used_ref[0]
    e = blk_e_ref[b]
    slot = wslot_ref[b]

    @pl.when(b == 0)
    def _():
        _start_weight(wd_hbm, e, wbuf, slot, 0, wsem)

    @pl.when(b < n_used)
    def _():
        @pl.when(first_ref[b] == 1)
        def _():
            _wait_weight(wd_hbm, e, wbuf, slot, 0, wsem)
            nxt = nxt_e_ref[b]

            @pl.when(nxt != e)
            def _():
                _start_weight(wd_hbm, nxt, wbuf, 1 - slot, 0, wsem)


        _store_rows(y_ref, _bdot(hid_ref[...], wbuf[slot, 0].astype(jnp.bfloat16)))

    @pl.when(b >= n_used)
    def _():
        y_ref[...] = jnp.zeros_like(y_ref)


def _down(plan, hid, wd):
    n_blocks = plan["blk_e"].shape[0]
    return pl.pallas_call(
        _down_kernel,
        grid_spec=pltpu.PrefetchScalarGridSpec(
            num_scalar_prefetch=5,
            grid=(n_blocks,),
            in_specs=[pl.BlockSpec((ROW_BLK, D_EXPERT), lambda b, *_: (b, 0)),
                      pl.BlockSpec(memory_space=pl.ANY)],
            out_specs=pl.BlockSpec((ROW_BLK * ROW_SUB, LANES), lambda b, *_: (b, 0)),
            scratch_shapes=[
                pltpu.VMEM((2, 1, D_EXPERT, D_MODEL), jnp.float32),
                pltpu.SemaphoreType.DMA((2,)),
            ],
        ),
        out_shape=jax.ShapeDtypeStruct((n_blocks * ROW_BLK * ROW_SUB, LANES), jnp.bfloat16),
        compiler_params=pltpu.CompilerParams(
            dimension_semantics=("arbitrary",), vmem_limit_bytes=VMEM_LIMIT),
        name="down",
    )(plan["blk_e"], plan["first"], plan["wslot"], plan["nxt_e"], plan["n_used"], hid, wd)


def _combine_kernel(dest_ref, x1_ref, route_ref, g_ref, y_hbm, o_ref, y0_buf, y1_buf, sem):
    i = pl.program_id(0)
    n = pl.num_programs(0)
    cur = i % 2
    n_tokens = n * CTM

    def start(step, slot):
        _start_row_gather(y_hbm, dest_ref, step * CTM, CTM, y0_buf.at[slot], sem.at[0, slot])
        _start_row_gather(y_hbm, dest_ref, n_tokens + step * CTM, CTM, y1_buf.at[slot], sem.at[1, slot])

    @pl.when(i == 0)
    def _():
        start(0, 0)

    @pl.when(i + 1 < n)
    def _():
        start(i + 1, 1 - cur)

    r = route_ref[...]
    _wait_row_gather(y_hbm, y0_buf.at[cur], sem.at[0, cur], CTM)
    _wait_row_gather(y_hbm, y1_buf.at[cur], sem.at[1, cur], CTM)
    y0 = _load_rows(y0_buf.at[cur], CTM).astype(jnp.float32)
    y1 = _load_rows(y1_buf.at[cur], CTM).astype(jnp.float32)
    xo = x1_ref[...] + (y0 * r[:, 0:1] + y1 * r[:, 1:2])
    o_ref[...] = _rms(xo, g_ref[...])


def _combine(dest_rows, x1, route, g, ybuf):
    t = x1.shape[0]
    tile = lambda w: pl.BlockSpec((CTM, w), lambda i, *_: (i, 0))
    return pl.pallas_call(
        _combine_kernel,
        grid_spec=pltpu.PrefetchScalarGridSpec(
            num_scalar_prefetch=1,
            grid=(t // CTM,),
            in_specs=[tile(D_MODEL), tile(LANES),
                      pl.BlockSpec((1, D_MODEL), lambda i, *_: (0, 0)),
                      pl.BlockSpec(memory_space=pl.ANY)],
            out_specs=tile(D_MODEL),
            scratch_shapes=[pltpu.VMEM((2, CTM * ROW_SUB, LANES), jnp.bfloat16),
                            pltpu.VMEM((2, CTM * ROW_SUB, LANES), jnp.bfloat16),
                            pltpu.SemaphoreType.DMA((2, 2))],
        ),
        out_shape=jax.ShapeDtypeStruct((t, D_MODEL), jnp.float32),
        compiler_params=pltpu.CompilerParams(
            dimension_semantics=("arbitrary",), vmem_limit_bytes=VMEM_LIMIT),
        name="combine",
    )(dest_rows, x1, route, g, ybuf)


def _dispatch_plan(e_flat):
    a = e_flat.shape[0]
    i32 = jnp.int32
    experts = jnp.arange(N_EXPERTS, dtype=i32)
    onehot = (e_flat[:, None] == experts[None, :]).astype(i32)
    rank = jnp.cumsum(onehot, axis=0) - onehot
    counts = jnp.sum(onehot, axis=0)
    padded = ((counts + ROW_BLK - 1) // ROW_BLK) * ROW_BLK
    pends = jnp.cumsum(padded)
    pstarts = pends - padded
    dest = jnp.sum((rank + pstarts[None, :]) * onehot, axis=1).astype(i32)
    n_blocks = (a + N_EXPERTS * (ROW_BLK - 1) + ROW_BLK - 1) // ROW_BLK
    blk_start = jnp.arange(n_blocks, dtype=i32) * ROW_BLK
    blk_e = jnp.clip(jnp.sum(pends[None, :] <= blk_start[:, None], axis=1), 0, N_EXPERTS - 1).astype(i32)
    first = jnp.concatenate([jnp.ones((1,), i32), (blk_e[1:] != blk_e[:-1]).astype(i32)])
    wslot = (jnp.cumsum(first) - 1) % 2
    later_used = (experts[None, :] > experts[:, None]) & (counts[None, :] > 0)
    nxt = jnp.min(jnp.where(later_used, experts[None, :], N_EXPERTS), axis=1)
    nxt = jnp.where(nxt == N_EXPERTS, experts, nxt)
    return {
        "dest": dest,
        "blk_e": blk_e,
        "first": first,
        "wslot": wslot.astype(i32),
        "nxt_e": nxt[blk_e].astype(i32),
        "n_used": (pends[-1] // ROW_BLK).astype(i32).reshape(1),
        "n_valid": jnp.clip(counts[blk_e] - (blk_start - pstarts[blk_e]), 0, ROW_BLK).astype(i32),
    }


def kernel(x, norm_mix_g, w_in, w_pool, pool_scale, conv_w, w_out, norm_ffn_g, w_router_group, b_router_group, w_router_expert, b_router_expert, w_gate, w_up, w_down, norm_final_g):
    bt, s, d = x.shape
    t = bt * s
    bf = jnp.bfloat16
    xt = x.reshape(t, d)
    assert norm_mix_g.shape[0] == 1, "the final norm is fused into the last layer's combine"
    for l in range(norm_mix_g.shape[0]):
        w_r = jnp.concatenate([w_router_group[l], w_router_expert[l]], axis=1)
        w_r = jnp.pad(w_r, ((0, 0), (0, LANES - w_r.shape[1]))).astype(bf)
        b_r = jnp.concatenate([b_router_group[l], b_router_expert[l]])
        b_r = jnp.pad(b_r, (0, LANES - b_r.shape[0])).reshape(1, LANES)
        x1, h2, route = _mixer(
            xt, norm_mix_g[l].reshape(1, d), w_in[l], w_pool[l].astype(bf),
            pool_scale[l].reshape(1, MIX_A), conv_w[l].T, w_out[l],
            norm_ffn_g[l].reshape(1, d), w_r, b_r)
        e_flat = jnp.concatenate([route[:, 2 + k] for k in range(TOP_K)]).astype(jnp.int32)
        plan = _dispatch_plan(e_flat)
        hid = _gate_up(plan, h2, w_gate[l], w_up[l])
        ybuf = _down(plan, hid, w_down[l])
        xt = _combine(plan["dest"] * ROW_SUB, x1, route, norm_final_g.reshape(1, d), ybuf)
    return xt.reshape(bt, s, d)
```

```python
import jax
import jax.numpy as jnp
from jax import lax
from jax.experimental import pallas as pl
from jax.experimental.pallas import tpu as pltpu

D_MODEL = 2048
MIX_A = 1024
MIX_B = 1024
POOL_WINDOWS = (2, 4, 8, 16)
POOL_CH = MIX_A // len(POOL_WINDOWS)
CONV_W = 3
N_GROUPS = 4
E_PER_GROUP = 8
N_EXPERTS = N_GROUPS * E_PER_GROUP
TOP_K = 2
D_EXPERT = D_MODEL // 2
EPS = 1e-6

LANES = 128
ROW_SUB = D_MODEL // LANES
HIST = 16
TM = 256
CTM = 256
ROW_BLK = 256
VMEM_LIMIT = 56 * 1024 * 1024
WEIGHT_DMA_PRIORITY = 1
WEIGHT_DMA_CHUNKS = 8
WEIGHT_SLAB = 64
WEIGHT_LAND_SLOTS = 4
DMA_QUEUES = 2
GATHER_SHIFT = 4
GATHER_UNROLL = 1 << GATHER_SHIFT
SCALAR_UNROLL = 16

_NEG = -1e30


def _rms(x, g):
    return x * lax.rsqrt(jnp.mean(x * x, axis=-1, keepdims=True) + EPS) * g


def _bdot(a, b):
    return jnp.dot(a, b, preferred_element_type=jnp.float32)


def _store_rows(ref, val):
    ref[...] = val.astype(jnp.bfloat16).reshape(val.shape[0] * ROW_SUB, LANES)


def _load_rows(ref, n):
    return ref[...].reshape(n, D_MODEL)


def _load_cast_weight(w_hbm, w_bf, land, sem):
    rows, cols = w_hbm.shape
    n_slabs = rows // WEIGHT_SLAB
    ahead = WEIGHT_LAND_SLOTS - 1

    def slab_copy(c):
        k = c % WEIGHT_LAND_SLOTS
        return pltpu.make_async_copy(w_hbm.at[pl.ds(c * WEIGHT_SLAB, WEIGHT_SLAB)],
                                     land.at[k, :, pl.ds(0, cols)], sem.at[k])

    for c in range(min(ahead, n_slabs)):
        slab_copy(c).start()
    for c in range(n_slabs):
        if c + ahead < n_slabs:
            slab_copy(c + ahead).start()
        slab_copy(c).wait()
        w_bf[pl.ds(c * WEIGHT_SLAB, WEIGHT_SLAB), :] = (
            land[c % WEIGHT_LAND_SLOTS, :, 0:cols].astype(jnp.bfloat16))


def _mixer_kernel(x_ref, x_next_ref, h0_ref, g1_ref, w_in_hbm, w_pool_ref, pscale_ref, convw_ref, w_out_hbm,
                  g2_ref, w_r_ref, b_r_ref,
                  x1_ref, h2_ref, route_ref,
                  ext_u, ext_z, h_buf, w_in_ref, w_out_ref, land, wsem):
    i = pl.program_id(0)

    @pl.when(i == 0)
    def _():
        ext_u[0:HIST, :] = jnp.zeros((HIST, MIX_A), jnp.float32)
        ext_z[0:HIST, :] = jnp.zeros((HIST, MIX_B), jnp.float32)
        h_buf[...] = h0_ref[...]
        _load_cast_weight(w_in_hbm, w_in_ref, land, wsem)
        _load_cast_weight(w_out_hbm, w_out_ref, land, wsem)

    x = x_ref[...]
    h = h_buf[...]

    ext_u[HIST:HIST + TM, :] = _bdot(h, w_in_ref[:, 0:MIX_A])
    c_gate = _bdot(h, w_in_ref[:, MIX_A + MIX_B:MIX_A + 2 * MIX_B])
    v = _bdot(h, w_in_ref[:, MIX_A + 2 * MIX_B:MIX_A + 3 * MIX_B])
    z = c_gate * v
    ext_z[HIST:HIST + TM, :] = z
    b_gate = _bdot(h, w_in_ref[:, MIX_A:MIX_A + MIX_B])

    y = ext_z[HIST - 2:HIST - 2 + TM, :] * convw_ref[0:1, :]
    y = y + ext_z[HIST - 1:HIST - 1 + TM, :] * convw_ref[1:2, :]
    y = y + z * convw_ref[2:3, :]
    y_b = b_gate * y
    ext_z[0:HIST, :] = ext_z[TM:TM + HIST, :]
    out_b = _bdot(y_b.astype(jnp.bfloat16), w_out_ref[MIX_A:MIX_A + MIX_B, :])

    row = lax.broadcasted_iota(jnp.int32, (TM, 1), 0) + i * TM + 1
    y_a = []
    for gi, w in enumerate(POOL_WINDOWS):
        c0 = gi * POOL_CH
        u = ext_u[HIST:HIST + TM, c0:c0 + POOL_CH]
        acc = u
        for s in range(1, w):
            acc = acc + ext_u[HIST - s:HIST - s + TM, c0:c0 + POOL_CH]
        cnt = jnp.minimum(row, w).astype(jnp.float32)
        pooled = (acc / cnt - u).astype(jnp.bfloat16)
        y_a.append(_bdot(pooled, w_pool_ref[gi]))
    y_a = jnp.concatenate(y_a, axis=-1) * pscale_ref[...]
    ext_u[0:HIST, :] = ext_u[TM:TM + HIST, :]
    out_a = _bdot(y_a.astype(jnp.bfloat16), w_out_ref[0:MIX_A, :])

    x1 = x + (out_a + out_b)
    x1_ref[...] = x1

    h2_f32 = _rms(x1, g2_ref[...])
    _store_rows(h2_ref, h2_f32)
    h2 = h2_f32.astype(jnp.bfloat16)

    logits = _bdot(h2, w_r_ref[...]) + b_r_ref[...]
    lane = lax.broadcasted_iota(jnp.int32, (TM, LANES), 1)
    lane_f = lane.astype(jnp.float32)

    def first_argmax(vals, vmax):
        return jnp.min(jnp.where(vals == vmax, lane_f, float(LANES)), axis=-1, keepdims=True)

    gl = jnp.where(lane < N_GROUPS, logits, _NEG)
    gmax = jnp.max(gl, axis=-1, keepdims=True)
    g_w = 1.0 / jnp.sum(jnp.exp(gl - gmax), axis=-1, keepdims=True)
    grp = first_argmax(gl, gmax).astype(jnp.int32)
    lo = N_GROUPS + E_PER_GROUP * grp
    el = jnp.where((lane >= lo) & (lane < lo + E_PER_GROUP), logits, _NEG)
    emax = jnp.max(el, axis=-1, keepdims=True)
    idx1 = first_argmax(el, emax)
    esum = jnp.sum(jnp.exp(el - emax), axis=-1, keepdims=True)
    el2 = jnp.where(lane_f == idx1, _NEG, el)
    e2max = jnp.max(el2, axis=-1, keepdims=True)
    idx2 = first_argmax(el2, e2max)
    p1 = 1.0 / esum
    p2 = jnp.exp(e2max - emax) / esum
    tot = p1 + p2
    w1 = g_w * (p1 / tot)
    w2 = g_w * (p2 / tot)
    route = jnp.where(lane == 0, w1, 0.0)
    route = jnp.where(lane == 1, w2, route)
    route = jnp.where(lane == 2, idx1 - N_GROUPS, route)
    route = jnp.where(lane == 3, idx2 - N_GROUPS, route)
    route_ref[...] = route

    h_buf[...] = _rms(x_next_ref[...], g1_ref[...]).astype(jnp.bfloat16)


def _first_tile_norm_kernel(x_ref, g_ref, o_ref):
    o_ref[...] = _rms(x_ref[...], g_ref[...]).astype(jnp.bfloat16)


def _first_tile_norm(x, g):
    return pl.pallas_call(
        _first_tile_norm_kernel,
        grid=(1,),
        in_specs=[pl.BlockSpec((TM, D_MODEL), lambda i: (0, 0)),
                  pl.BlockSpec((1, D_MODEL), lambda i: (0, 0))],
        out_specs=pl.BlockSpec((TM, D_MODEL), lambda i: (0, 0)),
        out_shape=jax.ShapeDtypeStruct((TM, D_MODEL), jnp.bfloat16),
        name="first_tile_norm",
    )(x, g)


def _mixer(x, g1, w_in, w_pool, pscale, convw, w_out, g2, w_r, b_r):
    t = x.shape[0]
    h0 = _first_tile_norm(x, g1)
    const = lambda shape: pl.BlockSpec(shape, lambda i: (0,) * len(shape),
                                       pipeline_mode=pl.Buffered(1))
    return pl.pallas_call(
        _mixer_kernel,
        grid=(t // TM,),
        in_specs=[
            pl.BlockSpec((TM, D_MODEL), lambda i: (i, 0)),
            pl.BlockSpec((TM, D_MODEL), lambda i: (jnp.minimum(i + 1, t // TM - 1), 0)),
            const((TM, D_MODEL)),
            const((1, D_MODEL)),
            pl.BlockSpec(memory_space=pl.ANY),
            const(w_pool.shape),
            const((1, MIX_A)),
            const((CONV_W, MIX_B)),
            pl.BlockSpec(memory_space=pl.ANY),
            const((1, D_MODEL)),
            const(w_r.shape),
            const((1, LANES)),
        ],
        out_specs=[
            pl.BlockSpec((TM, D_MODEL), lambda i: (i, 0)),
            pl.BlockSpec((TM * ROW_SUB, LANES), lambda i: (i, 0)),
            pl.BlockSpec((TM, LANES), lambda i: (i, 0)),
        ],
        out_shape=[
            jax.ShapeDtypeStruct((t, D_MODEL), jnp.float32),
            jax.ShapeDtypeStruct((t * ROW_SUB, LANES), jnp.bfloat16),
            jax.ShapeDtypeStruct((t, LANES), jnp.float32),
        ],
        scratch_shapes=[
            pltpu.VMEM((TM + HIST, MIX_A), jnp.float32),
            pltpu.VMEM((TM + HIST, MIX_B), jnp.float32),
            pltpu.VMEM((TM, D_MODEL), jnp.bfloat16),
            pltpu.VMEM(w_in.shape, jnp.bfloat16),
            pltpu.VMEM(w_out.shape, jnp.bfloat16),
            pltpu.VMEM((WEIGHT_LAND_SLOTS, WEIGHT_SLAB, w_in.shape[1]), jnp.float32),
            pltpu.SemaphoreType.DMA((WEIGHT_LAND_SLOTS,)),
        ],
        compiler_params=pltpu.CompilerParams(
            dimension_semantics=("arbitrary",), vmem_limit_bytes=VMEM_LIMIT),
        name="mixer",
    )(x, x, h0, g1, w_in, w_pool, pscale, convw, w_out, g2, w_r, b_r)


def _row_copy(src_hbm, row_start, dst_ref, r, sem):
    dst_start = r * ROW_SUB if isinstance(r, int) else pl.multiple_of(r * ROW_SUB, ROW_SUB)
    return pltpu.make_async_copy(src_hbm.at[pl.ds(pl.multiple_of(row_start, ROW_SUB), ROW_SUB)],
                                 dst_ref.at[pl.ds(dst_start, ROW_SUB)], sem)


def _gather_rows_issued(n_rows):
    if isinstance(n_rows, int):
        return n_rows
    return ((n_rows + GATHER_UNROLL - 1) >> GATHER_SHIFT) << GATHER_SHIFT


def _start_row_gather(src_hbm, idx_ref, base, n_rows, dst_ref, sem):
    if isinstance(n_rows, int):
        for r in range(n_rows):
            _row_copy(src_hbm, idx_ref[base + r], dst_ref, r, sem).start(priority=r % DMA_QUEUES)
        return

    def body(i, carry):
        for u in range(GATHER_UNROLL):
            r = i * GATHER_UNROLL + u
            src_row = idx_ref[base + jnp.minimum(r, n_rows - 1)]
            _row_copy(src_hbm, src_row, dst_ref, r, sem).start(priority=u % DMA_QUEUES)
        return carry

    lax.fori_loop(0, _gather_rows_issued(n_rows) >> GATHER_SHIFT, body, 0)


def _wait_row_gather(src_hbm, dst_ref, sem, n_rows):
    n = _gather_rows_issued(n_rows) * ROW_SUB
    pltpu.make_async_copy(src_hbm.at[pl.ds(0, n)], dst_ref.at[pl.ds(0, n)], sem).wait()


def _weight_copies(w_hbm, e, buf, slot, k, sem):
    rows = w_hbm.shape[1] // WEIGHT_DMA_CHUNKS
    return [pltpu.make_async_copy(w_hbm.at[e, pl.ds(c * rows, rows)],
                                  buf.at[slot, k, pl.ds(c * rows, rows)], sem.at[slot])
            for c in range(WEIGHT_DMA_CHUNKS)]


def _start_weight(*args):
    for cp in _weight_copies(*args):
        cp.start(priority=WEIGHT_DMA_PRIORITY)


def _wait_weight(*args):
    for cp in _weight_copies(*args):
        cp.wait()


def _build_row_sources(dest_ref, row_ref):
    n_tokens = dest_ref.shape[0] // TOP_K

    def scatter(i, carry):
        for u in range(SCALAR_UNROLL):
            tok = i * SCALAR_UNROLL + u
            for k in range(TOP_K):
                row_ref[dest_ref[k * n_tokens + tok]] = tok * ROW_SUB
        return carry

    lax.fori_loop(0, n_tokens // SCALAR_UNROLL, scatter, 0)


def _gate_up_kernel(blk_e_ref, first_ref, wslot_ref, nxt_e_ref, n_used_ref, n_valid_ref, dest_ref,
                    h2_hbm, wg_hbm, wu_hbm, hid_ref,
                    xg, wbuf, row_ref, gsem, wsem):
    b = pl.program_id(0)
    n_used = n_used_ref[0]
    e = blk_e_ref[b]
    slot = wslot_ref[b]

    @pl.when(b == 0)
    def _():
        _start_weight(wg_hbm, e, wbuf, slot, 0, wsem)
        _start_weight(wu_hbm, e, wbuf, slot, 1, wsem)
        xg[...] = jnp.zeros_like(xg)
        _build_row_sources(dest_ref, row_ref)
        _start_row_gather(h2_hbm, row_ref, 0, n_valid_ref[0], xg.at[0], gsem.at[0])

    @pl.when(b < n_used)
    def _():
        cur = b % 2

        @pl.when(b + 1 < n_used)
        def _():
            _start_row_gather(h2_hbm, row_ref, (b + 1) * ROW_BLK, n_valid_ref[b + 1],
                              xg.at[1 - cur], gsem.at[1 - cur])

        @pl.when(first_ref[b] == 1)
        def _():
            _wait_weight(wg_hbm, e, wbuf, slot, 0, wsem)
            _wait_weight(wu_hbm, e, wbuf, slot, 1, wsem)
            nxt = nxt_e_ref[b]

            @pl.when(nxt != e)
            def _():
                _start_weight(wg_hbm, nxt, wbuf, 1 - slot, 0, wsem)
                _start_weight(wu_hbm, nxt, wbuf, 1 - slot, 1, wsem)


        _wait_row_gather(h2_hbm, xg.at[cur], gsem.at[cur], n_valid_ref[b])
        xb = _load_rows(xg.at[cur], ROW_BLK)
        gate = _bdot(xb, wbuf[slot, 0].astype(jnp.bfloat16))
        up = _bdot(xb, wbuf[slot, 1].astype(jnp.bfloat16))
        hid_ref[...] = (gate * jax.nn.sigmoid(gate) * up).astype(jnp.bfloat16)

    @pl.when(b >= n_used)
    def _():
        hid_ref[...] = jnp.zeros_like(hid_ref)


def _gate_up(plan, h2, wg, wu):
    n_blocks = plan["blk_e"].shape[0]
    any_spec = pl.BlockSpec(memory_space=pl.ANY)
    return pl.pallas_call(
        _gate_up_kernel,
        grid_spec=pltpu.PrefetchScalarGridSpec(
            num_scalar_prefetch=7,
            grid=(n_blocks,),
            in_specs=[any_spec, any_spec, any_spec],
            out_specs=pl.BlockSpec((ROW_BLK, D_EXPERT), lambda b, *_: (b, 0)),
            scratch_shapes=[
                pltpu.VMEM((2, ROW_BLK * ROW_SUB, LANES), jnp.bfloat16),
                pltpu.VMEM((2, 2, D_MODEL, D_EXPERT), jnp.float32),
                pltpu.SMEM((n_blocks * ROW_BLK,), jnp.int32),
                pltpu.SemaphoreType.DMA((2,)),
                pltpu.SemaphoreType.DMA((2,)),
            ],
        ),
        out_shape=jax.ShapeDtypeStruct((n_blocks * ROW_BLK, D_EXPERT), jnp.bfloat16),
        compiler_params=pltpu.CompilerParams(
            dimension_semantics=("arbitrary",), vmem_limit_bytes=VMEM_LIMIT),
        name="gate_up",
    )(plan["blk_e"], plan["first"], plan["wslot"], plan["nxt_e"], plan["n_used"], plan["n_valid"],
      plan["dest"], h2, wg, wu)


def _down_kernel(blk_e_ref, first_ref, wslot_ref, nxt_e_ref, n_used_ref,
                 hid_ref, wd_hbm, y_ref, wbuf, wsem):
    b = pl.program_id(0)
    n_used = n_used_ref[0]
    e = blk_e_ref[b]
    slot = wslot_ref[b]

    @pl.when(b == 0)
    def _():
        _start_weight(wd_hbm, e, wbuf, slot, 0, wsem)

    @pl.when(b < n_used)
    def _():
        @pl.when(first_ref[b] == 1)
        def _():
            _wait_weight(wd_hbm, e, wbuf, slot, 0, wsem)
            nxt = nxt_e_ref[b]

            @pl.when(nxt != e)
            def _():
                _start_weight(wd_hbm, nxt, wbuf, 1 - slot, 0, wsem)


        _store_rows(y_ref, _bdot(hid_ref[...], wbuf[slot, 0].astype(jnp.bfloat16)))

    @pl.when(b >= n_used)
    def _():
        y_ref[...] = jnp.zeros_like(y_ref)


def _down(plan, hid, wd):
    n_blocks = plan["blk_e"].shape[0]
    return pl.pallas_call(
        _down_kernel,
        grid_spec=pltpu.PrefetchScalarGridSpec(
            num_scalar_prefetch=5,
            grid=(n_blocks,),
            in_specs=[pl.BlockSpec((ROW_BLK, D_EXPERT), lambda b, *_: (b, 0)),
                      pl.BlockSpec(memory_space=pl.ANY)],
            out_specs=pl.BlockSpec((ROW_BLK * ROW_SUB, LANES), lambda b, *_: (b, 0)),
            scratch_shapes=[
                pltpu.VMEM((2, 1, D_EXPERT, D_MODEL), jnp.float32),
                pltpu.SemaphoreType.DMA((2,)),
            ],
        ),
        out_shape=jax.ShapeDtypeStruct((n_blocks * ROW_BLK * ROW_SUB, LANES), jnp.bfloat16),
        compiler_params=pltpu.CompilerParams(
            dimension_semantics=("arbitrary",), vmem_limit_bytes=VMEM_LIMIT),
        name="down",
    )(plan["blk_e"], plan["first"], plan["wslot"], plan["nxt_e"], plan["n_used"], hid, wd)


def _combine_kernel(dest_ref, x1_ref, route_ref, g_ref, y_hbm, o_ref, y0_buf, y1_buf, sem):
    i = pl.program_id(0)
    n = pl.num_programs(0)
    cur = i % 2
    n_tokens = n * CTM

    def start(step, slot):
        _start_row_gather(y_hbm, dest_ref, step * CTM, CTM, y0_buf.at[slot], sem.at[0, slot])
        _start_row_gather(y_hbm, dest_ref, n_tokens + step * CTM, CTM, y1_buf.at[slot], sem.at[1, slot])

    @pl.when(i == 0)
    def _():
        start(0, 0)

    @pl.when(i + 1 < n)
    def _():
        start(i + 1, 1 - cur)

    r = route_ref[...]
    _wait_row_gather(y_hbm, y0_buf.at[cur], sem.at[0, cur], CTM)
    _wait_row_gather(y_hbm, y1_buf.at[cur], sem.at[1, cur], CTM)
    y0 = _load_rows(y0_buf.at[cur], CTM).astype(jnp.float32)
    y1 = _load_rows(y1_buf.at[cur], CTM).astype(jnp.float32)
    xo = x1_ref[...] + (y0 * r[:, 0:1] + y1 * r[:, 1:2])
    o_ref[...] = _rms(xo, g_ref[...])


def _combine(dest_rows, x1, route, g, ybuf):
    t = x1.shape[0]
    tile = lambda w: pl.BlockSpec((CTM, w), lambda i, *_: (i, 0))
    return pl.pallas_call(
        _combine_kernel,
        grid_spec=pltpu.PrefetchScalarGridSpec(
            num_scalar_prefetch=1,
            grid=(t // CTM,),
            in_specs=[tile(D_MODEL), tile(LANES),
                      pl.BlockSpec((1, D_MODEL), lambda i, *_: (0, 0)),
                      pl.BlockSpec(memory_space=pl.ANY)],
            out_specs=tile(D_MODEL),
            scratch_shapes=[pltpu.VMEM((2, CTM * ROW_SUB, LANES), jnp.bfloat16),
                            pltpu.VMEM((2, CTM * ROW_SUB, LANES), jnp.bfloat16),
                            pltpu.SemaphoreType.DMA((2, 2))],
        ),
        out_shape=jax.ShapeDtypeStruct((t, D_MODEL), jnp.float32),
        compiler_params=pltpu.CompilerParams(
            dimension_semantics=("arbitrary",), vmem_limit_bytes=VMEM_LIMIT),
        name="combine",
    )(dest_rows, x1, route, g, ybuf)


def _dispatch_plan(e_flat):
    a = e_flat.shape[0]
    i32 = jnp.int32
    experts = jnp.arange(N_EXPERTS, dtype=i32)
    onehot = (e_flat[:, None] == experts[None, :]).astype(i32)
    rank = jnp.cumsum(onehot, axis=0) - onehot
    counts = jnp.sum(onehot, axis=0)
    padded = ((counts + ROW_BLK - 1) // ROW_BLK) * ROW_BLK
    pends = jnp.cumsum(padded)
    pstarts = pends - padded
    dest = jnp.sum((rank + pstarts[None, :]) * onehot, axis=1).astype(i32)
    n_blocks = (a + N_EXPERTS * (ROW_BLK - 1) + ROW_BLK - 1) // ROW_BLK
    blk_start = jnp.arange(n_blocks, dtype=i32) * ROW_BLK
    blk_e = jnp.clip(jnp.sum(pends[None, :] <= blk_start[:, None], axis=1), 0, N_EXPERTS - 1).astype(i32)
    first = jnp.concatenate([jnp.ones((1,), i32), (blk_e[1:] != blk_e[:-1]).astype(i32)])
    wslot = (jnp.cumsum(first) - 1) % 2
    later_used = (experts[None, :] > experts[:, None]) & (counts[None, :] > 0)
    nxt = jnp.min(jnp.where(later_used, experts[None, :], N_EXPERTS), axis=1)
    nxt = jnp.where(nxt == N_EXPERTS, experts, nxt)
    return {
        "dest": dest,
        "blk_e": blk_e,
        "first": first,
        "wslot": wslot.astype(i32),
        "nxt_e": nxt[blk_e].astype(i32),
        "n_used": (pends[-1] // ROW_BLK).astype(i32).reshape(1),
        "n_valid": jnp.clip(counts[blk_e] - (blk_start - pstarts[blk_e]), 0, ROW_BLK).astype(i32),
    }


def kernel(x, norm_mix_g, w_in, w_pool, pool_scale, conv_w, w_out, norm_ffn_g, w_router_group, b_router_group, w_router_expert, b_router_expert, w_gate, w_up, w_down, norm_final_g):
    bt, s, d = x.shape
    t = bt * s
    bf = jnp.bfloat16
    xt = x.reshape(t, d)
    assert norm_mix_g.shape[0] == 1, "the final norm is fused into the last layer's combine"
    for l in range(norm_mix_g.shape[0]):
        w_r = jnp.concatenate([w_router_group[l], w_router_expert[l]], axis=1)
        w_r = jnp.pad(w_r, ((0, 0), (0, LANES - w_r.shape[1]))).astype(bf)
        b_r = jnp.concatenate([b_router_group[l], b_router_expert[l]])
        b_r = jnp.pad(b_r, (0, LANES - b_r.shape[0])).reshape(1, LANES)
        x1, h2, route = _mixer(
            xt, norm_mix_g[l].reshape(1, d), w_in[l], w_pool[l].astype(bf),
            pool_scale[l].reshape(1, MIX_A), conv_w[l].T, w_out[l],
            norm_ffn_g[l].reshape(1, d), w_r, b_r)
        e_flat = jnp.concatenate([route[:, 2 + k] for k in range(TOP_K)]).astype(jnp.int32)
        plan = _dispatch_plan(e_flat)
        hid = _gate_up(plan, h2, w_gate[l], w_up[l])
        ybuf = _down(plan, hid, w_down[l])
        xt = _combine(plan["dest"] * ROW_SUB, x1, route, norm_final_g.reshape(1, d), ybuf)
    return xt.reshape(bt, s, d)
```

```python
import jax
import jax.numpy as jnp
from jax import lax
from jax.experimental import pallas as pl
from jax.experimental.pallas import tpu as pltpu

D_MODEL = 2048
MIX_A = 1024
MIX_B = 1024
POOL_WINDOWS = (2, 4, 8, 16)
POOL_CH = MIX_A // len(POOL_WINDOWS)
CONV_W = 3
N_GROUPS = 4
E_PER_GROUP = 8
N_EXPERTS = N_GROUPS * E_PER_GROUP
TOP_K = 2
D_EXPERT = D_MODEL // 2
EPS = 1e-6

LANES = 128
ROW_SUB = D_MODEL // LANES
HIST = 16
TM = 256
CTM = 256
ROW_BLK = 256
VMEM_LIMIT = 56 * 1024 * 1024
WEIGHT_DMA_PRIORITY = 1
WEIGHT_DMA_CHUNKS = 8
GATE_UP_SLOTS = 3
WEIGHT_SLAB = 64
WEIGHT_LAND_SLOTS = 4
DMA_QUEUES = 2
GATHER_SHIFT = 4
GATHER_UNROLL = 1 << GATHER_SHIFT
SCALAR_UNROLL = 16

_NEG = -1e30


def _rms(x, g):
    return x * lax.rsqrt(jnp.mean(x * x, axis=-1, keepdims=True) + EPS) * g


def _bdot(a, b):
    return jnp.dot(a, b, preferred_element_type=jnp.float32)


def _store_rows(ref, val):
    ref[...] = val.astype(jnp.bfloat16).reshape(val.shape[0] * ROW_SUB, LANES)


def _load_rows(ref, n):
    return ref[...].reshape(n, D_MODEL)


def _load_cast_weight(w_hbm, w_bf, land, sem):
    rows, cols = w_hbm.shape
    n_slabs = rows // WEIGHT_SLAB
    ahead = WEIGHT_LAND_SLOTS - 1

    def slab_copy(c):
        k = c % WEIGHT_LAND_SLOTS
        return pltpu.make_async_copy(w_hbm.at[pl.ds(c * WEIGHT_SLAB, WEIGHT_SLAB)],
                                     land.at[k, :, pl.ds(0, cols)], sem.at[k])

    for c in range(min(ahead, n_slabs)):
        slab_copy(c).start()
    for c in range(n_slabs):
        if c + ahead < n_slabs:
            slab_copy(c + ahead).start()
        slab_copy(c).wait()
        w_bf[pl.ds(c * WEIGHT_SLAB, WEIGHT_SLAB), :] = (
            land[c % WEIGHT_LAND_SLOTS, :, 0:cols].astype(jnp.bfloat16))


def _mixer_kernel(x_ref, x_next_ref, h0_ref, g1_ref, w_in_hbm, w_pool_ref, pscale_ref, convw_ref, w_out_hbm,
                  g2_ref, w_r_ref, b_r_ref,
                  x1_ref, h2_ref, route_ref,
                  ext_u, ext_z, h_buf, w_in_ref, w_out_ref, land, wsem):
    i = pl.program_id(0)

    @pl.when(i == 0)
    def _():
        ext_u[0:HIST, :] = jnp.zeros((HIST, MIX_A), jnp.float32)
        ext_z[0:HIST, :] = jnp.zeros((HIST, MIX_B), jnp.float32)
        h_buf[...] = h0_ref[...]
        _load_cast_weight(w_in_hbm, w_in_ref, land, wsem)
        _load_cast_weight(w_out_hbm, w_out_ref, land, wsem)

    x = x_ref[...]
    h = h_buf[...]

    ext_u[HIST:HIST + TM, :] = _bdot(h, w_in_ref[:, 0:MIX_A])
    c_gate = _bdot(h, w_in_ref[:, MIX_A + MIX_B:MIX_A + 2 * MIX_B])
    v = _bdot(h, w_in_ref[:, MIX_A + 2 * MIX_B:MIX_A + 3 * MIX_B])
    z = c_gate * v
    ext_z[HIST:HIST + TM, :] = z
    b_gate = _bdot(h, w_in_ref[:, MIX_A:MIX_A + MIX_B])

    y = ext_z[HIST - 2:HIST - 2 + TM, :] * convw_ref[0:1, :]
    y = y + ext_z[HIST - 1:HIST - 1 + TM, :] * convw_ref[1:2, :]
    y = y + z * convw_ref[2:3, :]
    y_b = b_gate * y
    ext_z[0:HIST, :] = ext_z[TM:TM + HIST, :]
    out_b = _bdot(y_b.astype(jnp.bfloat16), w_out_ref[MIX_A:MIX_A + MIX_B, :])

    row = lax.broadcasted_iota(jnp.int32, (TM, 1), 0) + i * TM + 1
    y_a = []
    for gi, w in enumerate(POOL_WINDOWS):
        c0 = gi * POOL_CH
        u = ext_u[HIST:HIST + TM, c0:c0 + POOL_CH]
        acc = u
        for s in range(1, w):
            acc = acc + ext_u[HIST - s:HIST - s + TM, c0:c0 + POOL_CH]
        cnt = jnp.minimum(row, w).astype(jnp.float32)
        pooled = (acc / cnt - u).astype(jnp.bfloat16)
        y_a.append(_bdot(pooled, w_pool_ref[gi]))
    y_a = jnp.concatenate(y_a, axis=-1) * pscale_ref[...]
    ext_u[0:HIST, :] = ext_u[TM:TM + HIST, :]
    out_a = _bdot(y_a.astype(jnp.bfloat16), w_out_ref[0:MIX_A, :])

    x1 = x + (out_a + out_b)
    x1_ref[...] = x1

    h2_f32 = _rms(x1, g2_ref[...])
    _store_rows(h2_ref, h2_f32)
    h2 = h2_f32.astype(jnp.bfloat16)

    logits = _bdot(h2, w_r_ref[...]) + b_r_ref[...]
    lane = lax.broadcasted_iota(jnp.int32, (TM, LANES), 1)
    lane_f = lane.astype(jnp.float32)

    def first_argmax(vals, vmax):
        return jnp.min(jnp.where(vals == vmax, lane_f, float(LANES)), axis=-1, keepdims=True)

    gl = jnp.where(lane < N_GROUPS, logits, _NEG)
    gmax = jnp.max(gl, axis=-1, keepdims=True)
    g_w = 1.0 / jnp.sum(jnp.exp(gl - gmax), axis=-1, keepdims=True)
    grp = first_argmax(gl, gmax).astype(jnp.int32)
    lo = N_GROUPS + E_PER_GROUP * grp
    el = jnp.where((lane >= lo) & (lane < lo + E_PER_GROUP), logits, _NEG)
    emax = jnp.max(el, axis=-1, keepdims=True)
    idx1 = first_argmax(el, emax)
    esum = jnp.sum(jnp.exp(el - emax), axis=-1, keepdims=True)
    el2 = jnp.where(lane_f == idx1, _NEG, el)
    e2max = jnp.max(el2, axis=-1, keepdims=True)
    idx2 = first_argmax(el2, e2max)
    p1 = 1.0 / esum
    p2 = jnp.exp(e2max - emax) / esum
    tot = p1 + p2
    w1 = g_w * (p1 / tot)
    w2 = g_w * (p2 / tot)
    route = jnp.where(lane == 0, w1, 0.0)
    route = jnp.where(lane == 1, w2, route)
    route = jnp.where(lane == 2, idx1 - N_GROUPS, route)
    route = jnp.where(lane == 3, idx2 - N_GROUPS, route)
    route_ref[...] = route

    h_buf[...] = _rms(x_next_ref[...], g1_ref[...]).astype(jnp.bfloat16)


def _first_tile_norm_kernel(x_ref, g_ref, o_ref):
    o_ref[...] = _rms(x_ref[...], g_ref[...]).astype(jnp.bfloat16)


def _first_tile_norm(x, g):
    return pl.pallas_call(
        _first_tile_norm_kernel,
        grid=(1,),
        in_specs=[pl.BlockSpec((TM, D_MODEL), lambda i: (0, 0)),
                  pl.BlockSpec((1, D_MODEL), lambda i: (0, 0))],
        out_specs=pl.BlockSpec((TM, D_MODEL), lambda i: (0, 0)),
        out_shape=jax.ShapeDtypeStruct((TM, D_MODEL), jnp.bfloat16),
        name="first_tile_norm",
    )(x, g)


def _mixer(x, g1, w_in, w_pool, pscale, convw, w_out, g2, w_r, b_r):
    t = x.shape[0]
    h0 = _first_tile_norm(x, g1)
    const = lambda shape: pl.BlockSpec(shape, lambda i: (0,) * len(shape),
                                       pipeline_mode=pl.Buffered(1))
    return pl.pallas_call(
        _mixer_kernel,
        grid=(t // TM,),
        in_specs=[
            pl.BlockSpec((TM, D_MODEL), lambda i: (i, 0)),
            pl.BlockSpec((TM, D_MODEL), lambda i: (jnp.minimum(i + 1, t // TM - 1), 0)),
            const((TM, D_MODEL)),
            const((1, D_MODEL)),
            pl.BlockSpec(memory_space=pl.ANY),
            const(w_pool.shape),
            const((1, MIX_A)),
            const((CONV_W, MIX_B)),
            pl.BlockSpec(memory_space=pl.ANY),
            const((1, D_MODEL)),
            const(w_r.shape),
            const((1, LANES)),
        ],
        out_specs=[
            pl.BlockSpec((TM, D_MODEL), lambda i: (i, 0)),
            pl.BlockSpec((TM * ROW_SUB, LANES), lambda i: (i, 0)),
            pl.BlockSpec((TM, LANES), lambda i: (i, 0)),
        ],
        out_shape=[
            jax.ShapeDtypeStruct((t, D_MODEL), jnp.float32),
            jax.ShapeDtypeStruct((t * ROW_SUB, LANES), jnp.bfloat16),
            jax.ShapeDtypeStruct((t, LANES), jnp.float32),
        ],
        scratch_shapes=[
            pltpu.VMEM((TM + HIST, MIX_A), jnp.float32),
            pltpu.VMEM((TM + HIST, MIX_B), jnp.float32),
            pltpu.VMEM((TM, D_MODEL), jnp.bfloat16),
            pltpu.VMEM(w_in.shape, jnp.bfloat16),
            pltpu.VMEM(w_out.shape, jnp.bfloat16),
            pltpu.VMEM((WEIGHT_LAND_SLOTS, WEIGHT_SLAB, w_in.shape[1]), jnp.float32),
            pltpu.SemaphoreType.DMA((WEIGHT_LAND_SLOTS,)),
        ],
        compiler_params=pltpu.CompilerParams(
            dimension_semantics=("arbitrary",), vmem_limit_bytes=VMEM_LIMIT),
        name="mixer",
    )(x, x, h0, g1, w_in, w_pool, pscale, convw, w_out, g2, w_r, b_r)


def _row_copy(src_hbm, row_start, dst_ref, r, sem):
    dst_start = r * ROW_SUB if isinstance(r, int) else pl.multiple_of(r * ROW_SUB, ROW_SUB)
    return pltpu.make_async_copy(src_hbm.at[pl.ds(pl.multiple_of(row_start, ROW_SUB), ROW_SUB)],
                                 dst_ref.at[pl.ds(dst_start, ROW_SUB)], sem)


def _gather_rows_issued(n_rows):
    if isinstance(n_rows, int):
        return n_rows
    return ((n_rows + GATHER_UNROLL - 1) >> GATHER_SHIFT) << GATHER_SHIFT


def _start_row_gather(src_hbm, idx_ref, base, n_rows, dst_ref, sem):
    if isinstance(n_rows, int):
        for r in range(n_rows):
            _row_copy(src_hbm, idx_ref[base + r], dst_ref, r, sem).start(priority=r % DMA_QUEUES)
        return

    def body(i, carry):
        for u in range(GATHER_UNROLL):
            r = i * GATHER_UNROLL + u
            src_row = idx_ref[base + jnp.minimum(r, n_rows - 1)]
            _row_copy(src_hbm, src_row, dst_ref, r, sem).start(priority=u % DMA_QUEUES)
        return carry

    lax.fori_loop(0, _gather_rows_issued(n_rows) >> GATHER_SHIFT, body, 0)


def _wait_row_gather(src_hbm, dst_ref, sem, n_rows):
    n = _gather_rows_issued(n_rows) * ROW_SUB
    pltpu.make_async_copy(src_hbm.at[pl.ds(0, n)], dst_ref.at[pl.ds(0, n)], sem).wait()


def _weight_copies(w_hbm, e, buf, slot, k, sem):
    rows = w_hbm.shape[1] // WEIGHT_DMA_CHUNKS
    return [pltpu.make_async_copy(w_hbm.at[e, pl.ds(c * rows, rows)],
                                  buf.at[slot, k, pl.ds(c * rows, rows)], sem.at[slot])
            for c in range(WEIGHT_DMA_CHUNKS)]


def _start_weight(*args):
    for cp in _weight_copies(*args):
        cp.start(priority=WEIGHT_DMA_PRIORITY)


def _wait_weight(*args):
    for cp in _weight_copies(*args):
        cp.wait()


def _build_row_sources(dest_ref, row_ref):
    n_tokens = dest_ref.shape[0] // TOP_K

    def scatter(i, carry):
        for u in range(SCALAR_UNROLL):
            tok = i * SCALAR_UNROLL + u
            for k in range(TOP_K):
                row_ref[dest_ref[k * n_tokens + tok]] = tok * ROW_SUB
        return carry

    lax.fori_loop(0, n_tokens // SCALAR_UNROLL, scatter, 0)


def _gate_up_kernel(blk_e_ref, first_ref, wslot_ref, nxt_e_ref, nxt2_e_ref, n_used_ref, n_valid_ref, dest_ref,
                    h2_hbm, wg_hbm, wu_hbm, hid_ref,
                    xg, wbuf, row_ref, gsem, wsem):
    b = pl.program_id(0)
    n_used = n_used_ref[0]
    e = blk_e_ref[b]
    slot = wslot_ref[b]

    def start_expert(expert, s):
        _start_weight(wg_hbm, expert, wbuf, s, 0, wsem)
        _start_weight(wu_hbm, expert, wbuf, s, 1, wsem)

    @pl.when(b == 0)
    def _():
        start_expert(e, slot)

        @pl.when(nxt_e_ref[0] != e)
        def _():
            start_expert(nxt_e_ref[0], slot + 1)

        xg[...] = jnp.zeros_like(xg)
        _build_row_sources(dest_ref, row_ref)
        _start_row_gather(h2_hbm, row_ref, 0, n_valid_ref[0], xg.at[0], gsem.at[0])

    @pl.when(b < n_used)
    def _():
        cur = b % 2

        @pl.when(b + 1 < n_used)
        def _():
            _start_row_gather(h2_hbm, row_ref, (b + 1) * ROW_BLK, n_valid_ref[b + 1],
                              xg.at[1 - cur], gsem.at[1 - cur])

        @pl.when(first_ref[b] == 1)
        def _():
            _wait_weight(wg_hbm, e, wbuf, slot, 0, wsem)
            _wait_weight(wu_hbm, e, wbuf, slot, 1, wsem)
            nxt = nxt_e_ref[b]
            nxt2 = nxt2_e_ref[b]

            @pl.when((nxt != e) & (nxt2 != nxt))
            def _():
                start_expert(nxt2, jnp.where(slot == 0, GATE_UP_SLOTS - 1, slot - 1))

        _wait_row_gather(h2_hbm, xg.at[cur], gsem.at[cur], n_valid_ref[b])
        xb = _load_rows(xg.at[cur], ROW_BLK)
        gate = _bdot(xb, wbuf[slot, 0].astype(jnp.bfloat16))
        up = _bdot(xb, wbuf[slot, 1].astype(jnp.bfloat16))
        hid_ref[...] = (gate * jax.nn.sigmoid(gate) * up).astype(jnp.bfloat16)

    @pl.when(b >= n_used)
    def _():
        hid_ref[...] = jnp.zeros_like(hid_ref)


def _gate_up(plan, h2, wg, wu):
    n_blocks = plan["blk_e"].shape[0]
    any_spec = pl.BlockSpec(memory_space=pl.ANY)
    return pl.pallas_call(
        _gate_up_kernel,
        grid_spec=pltpu.PrefetchScalarGridSpec(
            num_scalar_prefetch=8,
            grid=(n_blocks,),
            in_specs=[any_spec, any_spec, any_spec],
            out_specs=pl.BlockSpec((ROW_BLK, D_EXPERT), lambda b, *_: (b, 0)),
            scratch_shapes=[
                pltpu.VMEM((2, ROW_BLK * ROW_SUB, LANES), jnp.bfloat16),
                pltpu.VMEM((GATE_UP_SLOTS, 2, D_MODEL, D_EXPERT), jnp.float32),
                pltpu.SMEM((n_blocks * ROW_BLK,), jnp.int32),
                pltpu.SemaphoreType.DMA((2,)),
                pltpu.SemaphoreType.DMA((GATE_UP_SLOTS,)),
            ],
        ),
        out_shape=jax.ShapeDtypeStruct((n_blocks * ROW_BLK, D_EXPERT), jnp.bfloat16),
        compiler_params=pltpu.CompilerParams(
            dimension_semantics=("arbitrary",), vmem_limit_bytes=VMEM_LIMIT),
        name="gate_up",
    )(plan["blk_e"], plan["first"], plan["wslot3"], plan["nxt_e"], plan["nxt2_e"], plan["n_used"],
      plan["n_valid"], plan["dest"], h2, wg, wu)


def _down_kernel(blk_e_ref, first_ref, wslot_ref, nxt_e_ref, n_used_ref,
                 hid_ref, wd_hbm, y_ref, wbuf, wsem):
    b = pl.program_id(0)
    n_used = n_used_ref[0]
    e = blk_e_ref[b]
    slot = wslot_ref[b]

    @pl.when(b == 0)
    def _():
        _start_weight(wd_hbm, e, wbuf, slot, 0, wsem)

    @pl.when(b < n_used)
    def _():
        @pl.when(first_ref[b] == 1)
        def _():
            _wait_weight(wd_hbm, e, wbuf, slot, 0, wsem)
            nxt = nxt_e_ref[b]

            @pl.when(nxt != e)
            def _():
                _start_weight(wd_hbm, nxt, wbuf, 1 - slot, 0, wsem)


        _store_rows(y_ref, _bdot(hid_ref[...], wbuf[slot, 0].astype(jnp.bfloat16)))

    @pl.when(b >= n_used)
    def _():
        y_ref[...] = jnp.zeros_like(y_ref)


def _down(plan, hid, wd):
    n_blocks = plan["blk_e"].shape[0]
    return pl.pallas_call(
        _down_kernel,
        grid_spec=pltpu.PrefetchScalarGridSpec(
            num_scalar_prefetch=5,
            grid=(n_blocks,),
            in_specs=[pl.BlockSpec((ROW_BLK, D_EXPERT), lambda b, *_: (b, 0)),
                      pl.BlockSpec(memory_space=pl.ANY)],
            out_specs=pl.BlockSpec((ROW_BLK * ROW_SUB, LANES), lambda b, *_: (b, 0)),
            scratch_shapes=[
                pltpu.VMEM((2, 1, D_EXPERT, D_MODEL), jnp.float32),
                pltpu.SemaphoreType.DMA((2,)),
            ],
        ),
        out_shape=jax.ShapeDtypeStruct((n_blocks * ROW_BLK * ROW_SUB, LANES), jnp.bfloat16),
        compiler_params=pltpu.CompilerParams(
            dimension_semantics=("arbitrary",), vmem_limit_bytes=VMEM_LIMIT),
        name="down",
    )(plan["blk_e"], plan["first"], plan["wslot"], plan["nxt_e"], plan["n_used"], hid, wd)


def _combine_kernel(dest_ref, x1_ref, route_ref, g_ref, y_hbm, o_ref, y0_buf, y1_buf, sem):
    i = pl.program_id(0)
    n = pl.num_programs(0)
    cur = i % 2
    n_tokens = n * CTM

    def start(step, slot):
        _start_row_gather(y_hbm, dest_ref, step * CTM, CTM, y0_buf.at[slot], sem.at[0, slot])
        _start_row_gather(y_hbm, dest_ref, n_tokens + step * CTM, CTM, y1_buf.at[slot], sem.at[1, slot])

    @pl.when(i == 0)
    def _():
        start(0, 0)

    @pl.when(i + 1 < n)
    def _():
        start(i + 1, 1 - cur)

    r = route_ref[...]
    _wait_row_gather(y_hbm, y0_buf.at[cur], sem.at[0, cur], CTM)
    _wait_row_gather(y_hbm, y1_buf.at[cur], sem.at[1, cur], CTM)
    y0 = _load_rows(y0_buf.at[cur], CTM).astype(jnp.float32)
    y1 = _load_rows(y1_buf.at[cur], CTM).astype(jnp.float32)
    xo = x1_ref[...] + (y0 * r[:, 0:1] + y1 * r[:, 1:2])
    o_ref[...] = _rms(xo, g_ref[...])


def _combine(dest_rows, x1, route, g, ybuf):
    t = x1.shape[0]
    tile = lambda w: pl.BlockSpec((CTM, w), lambda i, *_: (i, 0))
    return pl.pallas_call(
        _combine_kernel,
        grid_spec=pltpu.PrefetchScalarGridSpec(
            num_scalar_prefetch=1,
            grid=(t // CTM,),
            in_specs=[tile(D_MODEL), tile(LANES),
                      pl.BlockSpec((1, D_MODEL), lambda i, *_: (0, 0)),
                      pl.BlockSpec(memory_space=pl.ANY)],
            out_specs=tile(D_MODEL),
            scratch_shapes=[pltpu.VMEM((2, CTM * ROW_SUB, LANES), jnp.bfloat16),
                            pltpu.VMEM((2, CTM * ROW_SUB, LANES), jnp.bfloat16),
                            pltpu.SemaphoreType.DMA((2, 2))],
        ),
        out_shape=jax.ShapeDtypeStruct((t, D_MODEL), jnp.float32),
        compiler_params=pltpu.CompilerParams(
            dimension_semantics=("arbitrary",), vmem_limit_bytes=VMEM_LIMIT),
        name="combine",
    )(dest_rows, x1, route, g, ybuf)


def _dispatch_plan(e_flat):
    a = e_flat.shape[0]
    i32 = jnp.int32
    experts = jnp.arange(N_EXPERTS, dtype=i32)
    onehot = (e_flat[:, None] == experts[None, :]).astype(i32)
    rank = jnp.cumsum(onehot, axis=0) - onehot
    counts = jnp.sum(onehot, axis=0)
    padded = ((counts + ROW_BLK - 1) // ROW_BLK) * ROW_BLK
    pends = jnp.cumsum(padded)
    pstarts = pends - padded
    dest = jnp.sum((rank + pstarts[None, :]) * onehot, axis=1).astype(i32)
    n_blocks = (a + N_EXPERTS * (ROW_BLK - 1) + ROW_BLK - 1) // ROW_BLK
    blk_start = jnp.arange(n_blocks, dtype=i32) * ROW_BLK
    blk_e = jnp.clip(jnp.sum(pends[None, :] <= blk_start[:, None], axis=1), 0, N_EXPERTS - 1).astype(i32)
    first = jnp.concatenate([jnp.ones((1,), i32), (blk_e[1:] != blk_e[:-1]).astype(i32)])
    ordinal = jnp.cumsum(first) - 1
    later_used = (experts[None, :] > experts[:, None]) & (counts[None, :] > 0)
    nxt = jnp.min(jnp.where(later_used, experts[None, :], N_EXPERTS), axis=1)
    nxt = jnp.where(nxt == N_EXPERTS, experts, nxt)
    return {
        "dest": dest,
        "blk_e": blk_e,
        "first": first,
        "wslot": (ordinal % 2).astype(i32),
        "wslot3": (ordinal % GATE_UP_SLOTS).astype(i32),
        "nxt_e": nxt[blk_e].astype(i32),
        "nxt2_e": nxt[nxt[blk_e]].astype(i32),
        "n_used": (pends[-1] // ROW_BLK).astype(i32).reshape(1),
        "n_valid": jnp.clip(counts[blk_e] - (blk_start - pstarts[blk_e]), 0, ROW_BLK).astype(i32),
    }


def kernel(x, norm_mix_g, w_in, w_pool, pool_scale, conv_w, w_out, norm_ffn_g, w_router_group, b_router_group, w_router_expert, b_router_expert, w_gate, w_up, w_down, norm_final_g):
    bt, s, d = x.shape
    t = bt * s
    bf = jnp.bfloat16
    xt = x.reshape(t, d)
    assert norm_mix_g.shape[0] == 1, "the final norm is fused into the last layer's combine"
    for l in range(norm_mix_g.shape[0]):
        w_r = jnp.concatenate([w_router_group[l], w_router_expert[l]], axis=1)
        w_r = jnp.pad(w_r, ((0, 0), (0, LANES - w_r.shape[1]))).astype(bf)
        b_r = jnp.concatenate([b_router_group[l], b_router_expert[l]])
        b_r = jnp.pad(b_r, (0, LANES - b_r.shape[0])).reshape(1, LANES)
        x1, h2, route = _mixer(
            xt, norm_mix_g[l].reshape(1, d), w_in[l], w_pool[l].astype(bf),
            pool_scale[l].reshape(1, MIX_A), conv_w[l].T, w_out[l],
            norm_ffn_g[l].reshape(1, d), w_r, b_r)
        e_flat = jnp.concatenate([route[:, 2 + k] for k in range(TOP_K)]).astype(jnp.int32)
        plan = _dispatch_plan(e_flat)
        hid = _gate_up(plan, h2, w_gate[l], w_up[l])
        ybuf = _down(plan, hid, w_down[l])
        xt = _combine(plan["dest"] * ROW_SUB, x1, route, norm_final_g.reshape(1, d), ybuf)
    return xt.reshape(bt, s, d)
```

```python
import jax
import jax.numpy as jnp
from jax import lax
from jax.experimental import pallas as pl
from jax.experimental.pallas import tpu as pltpu

D_MODEL = 2048
MIX_A = 1024
MIX_B = 1024
POOL_WINDOWS = (2, 4, 8, 16)
POOL_CH = MIX_A // len(POOL_WINDOWS)
CONV_W = 3
N_GROUPS = 4
E_PER_GROUP = 8
N_EXPERTS = N_GROUPS * E_PER_GROUP
TOP_K = 2
D_EXPERT = D_MODEL // 2
EPS = 1e-6

LANES = 128
ROW_SUB = D_MODEL // LANES
HIST = 16
TM = 256
CTM = 256
ROW_BLK = 256
VMEM_LIMIT = 56 * 1024 * 1024
WEIGHT_DMA_PRIORITY = 1
WEIGHT_DMA_CHUNKS = 8
WEIGHT_SLAB = 64
WEIGHT_LAND_SLOTS = 4
DMA_QUEUES = 2
GATHER_SHIFT = 5
GATHER_UNROLL = 1 << GATHER_SHIFT
SCALAR_UNROLL = 16

_NEG = -1e30


def _rms(x, g):
    return x * lax.rsqrt(jnp.mean(x * x, axis=-1, keepdims=True) + EPS) * g


def _bdot(a, b):
    return jnp.dot(a, b, preferred_element_type=jnp.float32)


def _store_rows(ref, val):
    ref[...] = val.astype(jnp.bfloat16).reshape(val.shape[0] * ROW_SUB, LANES)


def _load_rows(ref, n):
    return ref[...].reshape(n, D_MODEL)


def _load_cast_weight(w_hbm, w_bf, land, sem):
    rows, cols = w_hbm.shape
    n_slabs = rows // WEIGHT_SLAB
    ahead = WEIGHT_LAND_SLOTS - 1

    def slab_copy(c):
        k = c % WEIGHT_LAND_SLOTS
        return pltpu.make_async_copy(w_hbm.at[pl.ds(c * WEIGHT_SLAB, WEIGHT_SLAB)],
                                     land.at[k, :, pl.ds(0, cols)], sem.at[k])

    for c in range(min(ahead, n_slabs)):
        slab_copy(c).start()
    for c in range(n_slabs):
        if c + ahead < n_slabs:
            slab_copy(c + ahead).start()
        slab_copy(c).wait()
        w_bf[pl.ds(c * WEIGHT_SLAB, WEIGHT_SLAB), :] = (
            land[c % WEIGHT_LAND_SLOTS, :, 0:cols].astype(jnp.bfloat16))


def _mixer_kernel(x_ref, x_next_ref, g1_ref, w_in_hbm, w_pool_ref, pscale_ref, convw_ref, w_out_hbm,
                  g2_ref, w_r_ref, b_r_ref,
                  x1_ref, h2_ref, route_ref,
                  ext_u, ext_z, h_buf, w_in_ref, w_out_ref, land, wsem):
    i = pl.program_id(0)

    @pl.when(i == 0)
    def _():
        ext_u[0:HIST, :] = jnp.zeros((HIST, MIX_A), jnp.float32)
        ext_z[0:HIST, :] = jnp.zeros((HIST, MIX_B), jnp.float32)
        h_buf[...] = _rms(x_ref[...], g1_ref[...]).astype(jnp.bfloat16)
        _load_cast_weight(w_in_hbm, w_in_ref, land, wsem)
        _load_cast_weight(w_out_hbm, w_out_ref, land, wsem)

    x = x_ref[...]
    h = h_buf[...]

    ext_u[HIST:HIST + TM, :] = _bdot(h, w_in_ref[:, 0:MIX_A])
    c_gate = _bdot(h, w_in_ref[:, MIX_A + MIX_B:MIX_A + 2 * MIX_B])
    v = _bdot(h, w_in_ref[:, MIX_A + 2 * MIX_B:MIX_A + 3 * MIX_B])
    z = c_gate * v
    ext_z[HIST:HIST + TM, :] = z
    b_gate = _bdot(h, w_in_ref[:, MIX_A:MIX_A + MIX_B])

    y = ext_z[HIST - 2:HIST - 2 + TM, :] * convw_ref[0:1, :]
    y = y + ext_z[HIST - 1:HIST - 1 + TM, :] * convw_ref[1:2, :]
    y = y + z * convw_ref[2:3, :]
    y_b = b_gate * y
    ext_z[0:HIST, :] = ext_z[TM:TM + HIST, :]
    out_b = _bdot(y_b.astype(jnp.bfloat16), w_out_ref[MIX_A:MIX_A + MIX_B, :])

    row = lax.broadcasted_iota(jnp.int32, (TM, 1), 0) + i * TM + 1
    y_a = []
    for gi, w in enumerate(POOL_WINDOWS):
        c0 = gi * POOL_CH
        u = ext_u[HIST:HIST + TM, c0:c0 + POOL_CH]
        acc = u
        for s in range(1, w):
            acc = acc + ext_u[HIST - s:HIST - s + TM, c0:c0 + POOL_CH]
        cnt = jnp.minimum(row, w).astype(jnp.float32)
        pooled = (acc / cnt - u).astype(jnp.bfloat16)
        y_a.append(_bdot(pooled, w_pool_ref[gi]))
    y_a = jnp.concatenate(y_a, axis=-1) * pscale_ref[...]
    ext_u[0:HIST, :] = ext_u[TM:TM + HIST, :]
    out_a = _bdot(y_a.astype(jnp.bfloat16), w_out_ref[0:MIX_A, :])

    x1 = x + (out_a + out_b)
    x1_ref[...] = x1

    h2_f32 = _rms(x1, g2_ref[...])
    _store_rows(h2_ref, h2_f32)
    h2 = h2_f32.astype(jnp.bfloat16)

    logits = _bdot(h2, w_r_ref[...]) + b_r_ref[...]
    lane = lax.broadcasted_iota(jnp.int32, (TM, LANES), 1)
    lane_f = lane.astype(jnp.float32)

    def first_argmax(vals, vmax):
        return jnp.min(jnp.where(vals == vmax, lane_f, float(LANES)), axis=-1, keepdims=True)

    gl = jnp.where(lane < N_GROUPS, logits, _NEG)
    gmax = jnp.max(gl, axis=-1, keepdims=True)
    g_w = 1.0 / jnp.sum(jnp.exp(gl - gmax), axis=-1, keepdims=True)
    grp = first_argmax(gl, gmax).astype(jnp.int32)
    lo = N_GROUPS + E_PER_GROUP * grp
    el = jnp.where((lane >= lo) & (lane < lo + E_PER_GROUP), logits, _NEG)
    emax = jnp.max(el, axis=-1, keepdims=True)
    idx1 = first_argmax(el, emax)
    esum = jnp.sum(jnp.exp(el - emax), axis=-1, keepdims=True)
    el2 = jnp.where(lane_f == idx1, _NEG, el)
    e2max = jnp.max(el2, axis=-1, keepdims=True)
    idx2 = first_argmax(el2, e2max)
    p1 = 1.0 / esum
    p2 = jnp.exp(e2max - emax) / esum
    tot = p1 + p2
    w1 = g_w * (p1 / tot)
    w2 = g_w * (p2 / tot)
    route = jnp.where(lane == 0, w1, 0.0)
    route = jnp.where(lane == 1, w2, route)
    route = jnp.where(lane == 2, idx1 - N_GROUPS, route)
    route = jnp.where(lane == 3, idx2 - N_GROUPS, route)
    route_ref[...] = route

    h_buf[...] = _rms(x_next_ref[...], g1_ref[...]).astype(jnp.bfloat16)


def _mixer(x, g1, w_in, w_pool, pscale, convw, w_out, g2, w_r, b_r):
    t = x.shape[0]
    const = lambda shape: pl.BlockSpec(shape, lambda i: (0,) * len(shape),
                                       pipeline_mode=pl.Buffered(1))
    return pl.pallas_call(
        _mixer_kernel,
        grid=(t // TM,),
        in_specs=[
            pl.BlockSpec((TM, D_MODEL), lambda i: (i, 0)),
            pl.BlockSpec((TM, D_MODEL), lambda i: (jnp.minimum(i + 1, t // TM - 1), 0)),
            const((1, D_MODEL)),
            pl.BlockSpec(memory_space=pl.ANY),
            const(w_pool.shape),
            const((1, MIX_A)),
            const((CONV_W, MIX_B)),
            pl.BlockSpec(memory_space=pl.ANY),
            const((1, D_MODEL)),
            const(w_r.shape),
            const((1, LANES)),
        ],
        out_specs=[
            pl.BlockSpec((TM, D_MODEL), lambda i: (i, 0)),
            pl.BlockSpec((TM * ROW_SUB, LANES), lambda i: (i, 0)),
            pl.BlockSpec((TM, LANES), lambda i: (i, 0)),
        ],
        out_shape=[
            jax.ShapeDtypeStruct((t, D_MODEL), jnp.float32),
            jax.ShapeDtypeStruct((t * ROW_SUB, LANES), jnp.bfloat16),
            jax.ShapeDtypeStruct((t, LANES), jnp.float32),
        ],
        scratch_shapes=[
            pltpu.VMEM((TM + HIST, MIX_A), jnp.float32),
            pltpu.VMEM((TM + HIST, MIX_B), jnp.float32),
            pltpu.VMEM((TM, D_MODEL), jnp.bfloat16),
            pltpu.VMEM(w_in.shape, jnp.bfloat16),
            pltpu.VMEM(w_out.shape, jnp.bfloat16),
            pltpu.VMEM((WEIGHT_LAND_SLOTS, WEIGHT_SLAB, w_in.shape[1]), jnp.float32),
            pltpu.SemaphoreType.DMA((WEIGHT_LAND_SLOTS,)),
        ],
        compiler_params=pltpu.CompilerParams(
            dimension_semantics=("arbitrary",), vmem_limit_bytes=VMEM_LIMIT),
        name="mixer",
    )(x, x, g1, w_in, w_pool, pscale, convw, w_out, g2, w_r, b_r)


def _row_copy(src_hbm, row_start, dst_ref, r, sem):
    dst_start = r * ROW_SUB if isinstance(r, int) else pl.multiple_of(r * ROW_SUB, ROW_SUB)
    return pltpu.make_async_copy(src_hbm.at[pl.ds(pl.multiple_of(row_start, ROW_SUB), ROW_SUB)],
                                 dst_ref.at[pl.ds(dst_start, ROW_SUB)], sem)


def _gather_rows_issued(n_rows):
    if isinstance(n_rows, int):
        return n_rows
    return ((n_rows + GATHER_UNROLL - 1) >> GATHER_SHIFT) << GATHER_SHIFT


def _start_row_gather(src_hbm, idx_ref, base, n_rows, dst_ref, sem):
    if isinstance(n_rows, int):
        for r in range(n_rows):
            _row_copy(src_hbm, idx_ref[base + r], dst_ref, r, sem).start(priority=r % DMA_QUEUES)
        return

    def body(i, carry):
        for u in range(GATHER_UNROLL):
            r = i * GATHER_UNROLL + u
            src_row = idx_ref[base + jnp.minimum(r, n_rows - 1)]
            _row_copy(src_hbm, src_row, dst_ref, r, sem).start(priority=u % DMA_QUEUES)
        return carry

    lax.fori_loop(0, _gather_rows_issued(n_rows) >> GATHER_SHIFT, body, 0)


def _wait_row_gather(src_hbm, dst_ref, sem, n_rows):
    n = _gather_rows_issued(n_rows) * ROW_SUB
    pltpu.make_async_copy(src_hbm.at[pl.ds(0, n)], dst_ref.at[pl.ds(0, n)], sem).wait()


def _weight_copies(w_hbm, e, buf, slot, k, sem):
    rows = w_hbm.shape[1] // WEIGHT_DMA_CHUNKS
    return [pltpu.make_async_copy(w_hbm.at[e, pl.ds(c * rows, rows)],
                                  buf.at[slot, k, pl.ds(c * rows, rows)], sem.at[slot])
            for c in range(WEIGHT_DMA_CHUNKS)]


def _start_weight(*args):
    for cp in _weight_copies(*args):
        cp.start(priority=WEIGHT_DMA_PRIORITY)


def _wait_weight(*args):
    for cp in _weight_copies(*args):
        cp.wait()


def _build_row_sources(dest_ref, row_ref):
    n_tokens = dest_ref.shape[0] // TOP_K

    def scatter(i, carry):
        for u in range(SCALAR_UNROLL):
            tok = i * SCALAR_UNROLL + u
            for k in range(TOP_K):
                row_ref[dest_ref[k * n_tokens + tok]] = tok * ROW_SUB
        return carry

    lax.fori_loop(0, n_tokens // SCALAR_UNROLL, scatter, 0)


def _gate_up_kernel(blk_e_ref, first_ref, wslot_ref, nxt_e_ref, n_used_ref, n_valid_ref, dest_ref,
                    h2_hbm, wg_hbm, wu_hbm, hid_ref,
                    xg, wbuf, row_ref, gsem, wsem):
    b = pl.program_id(0)
    n_used = n_used_ref[0]
    e = blk_e_ref[b]
    slot = wslot_ref[b]

    @pl.when(b == 0)
    def _():
        _start_weight(wg_hbm, e, wbuf, slot, 0, wsem)
        _start_weight(wu_hbm, e, wbuf, slot, 1, wsem)
        xg[...] = jnp.zeros_like(xg)
        _build_row_sources(dest_ref, row_ref)
        _start_row_gather(h2_hbm, row_ref, 0, n_valid_ref[0], xg.at[0], gsem.at[0])

    @pl.when(b < n_used)
    def _():
        cur = b % 2

        @pl.when(b + 1 < n_used)
        def _():
            _start_row_gather(h2_hbm, row_ref, (b + 1) * ROW_BLK, n_valid_ref[b + 1],
                              xg.at[1 - cur], gsem.at[1 - cur])

        @pl.when(first_ref[b] == 1)
        def _():
            _wait_weight(wg_hbm, e, wbuf, slot, 0, wsem)
            _wait_weight(wu_hbm, e, wbuf, slot, 1, wsem)
            nxt = nxt_e_ref[b]

            @pl.when(nxt != e)
            def _():
                _start_weight(wg_hbm, nxt, wbuf, 1 - slot, 0, wsem)
                _start_weight(wu_hbm, nxt, wbuf, 1 - slot, 1, wsem)


        _wait_row_gather(h2_hbm, xg.at[cur], gsem.at[cur], n_valid_ref[b])
        xb = _load_rows(xg.at[cur], ROW_BLK)
        gate = _bdot(xb, wbuf[slot, 0].astype(jnp.bfloat16))
        up = _bdot(xb, wbuf[slot, 1].astype(jnp.bfloat16))
        hid_ref[...] = (gate * jax.nn.sigmoid(gate) * up).astype(jnp.bfloat16)

    @pl.when(b >= n_used)
    def _():
        hid_ref[...] = jnp.zeros_like(hid_ref)


def _gate_up(plan, h2, wg, wu):
    n_blocks = plan["blk_e"].shape[0]
    any_spec = pl.BlockSpec(memory_space=pl.ANY)
    return pl.pallas_call(
        _gate_up_kernel,
        grid_spec=pltpu.PrefetchScalarGridSpec(
            num_scalar_prefetch=7,
            grid=(n_blocks,),
            in_specs=[any_spec, any_spec, any_spec],
            out_specs=pl.BlockSpec((ROW_BLK, D_EXPERT), lambda b, *_: (b, 0)),
            scratch_shapes=[
                pltpu.VMEM((2, ROW_BLK * ROW_SUB, LANES), jnp.bfloat16),
                pltpu.VMEM((2, 2, D_MODEL, D_EXPERT), jnp.float32),
                pltpu.SMEM((n_blocks * ROW_BLK,), jnp.int32),
                pltpu.SemaphoreType.DMA((2,)),
                pltpu.SemaphoreType.DMA((2,)),
            ],
        ),
        out_shape=jax.ShapeDtypeStruct((n_blocks * ROW_BLK, D_EXPERT), jnp.bfloat16),
        compiler_params=pltpu.CompilerParams(
            dimension_semantics=("arbitrary",), vmem_limit_bytes=VMEM_LIMIT),
        name="gate_up",
    )(plan["blk_e"], plan["first"], plan["wslot"], plan["nxt_e"], plan["n_used"], plan["n_valid"],
      plan["dest"], h2, wg, wu)


def _down_kernel(blk_e_ref, first_ref, wslot_ref, nxt_e_ref, n_used_ref,
                 hid_ref, wd_hbm, y_ref, wbuf, wsem):
    b = pl.program_id(0)
    n_used = n_used_ref[0]
    e = blk_e_ref[b]
    slot = wslot_ref[b]

    @pl.when(b == 0)
    def _():
        _start_weight(wd_hbm, e, wbuf, slot, 0, wsem)

    @pl.when(b < n_used)
    def _():
        @pl.when(first_ref[b] == 1)
        def _():
            _wait_weight(wd_hbm, e, wbuf, slot, 0, wsem)
            nxt = nxt_e_ref[b]

            @pl.when(nxt != e)
            def _():
                _start_weight(wd_hbm, nxt, wbuf, 1 - slot, 0, wsem)


        _store_rows(y_ref, _bdot(hid_ref[...], wbuf[slot, 0].astype(jnp.bfloat16)))

    @pl.when(b >= n_used)
    def _():
        y_ref[...] = jnp.zeros_like(y_ref)


def _down(plan, hid, wd):
    n_blocks = plan["blk_e"].shape[0]
    return pl.pallas_call(
        _down_kernel,
        grid_spec=pltpu.PrefetchScalarGridSpec(
            num_scalar_prefetch=5,
            grid=(n_blocks,),
            in_specs=[pl.BlockSpec((ROW_BLK, D_EXPERT), lambda b, *_: (b, 0)),
                      pl.BlockSpec(memory_space=pl.ANY)],
            out_specs=pl.BlockSpec((ROW_BLK * ROW_SUB, LANES), lambda b, *_: (b, 0)),
            scratch_shapes=[
                pltpu.VMEM((2, 1, D_EXPERT, D_MODEL), jnp.float32),
                pltpu.SemaphoreType.DMA((2,)),
            ],
        ),
        out_shape=jax.ShapeDtypeStruct((n_blocks * ROW_BLK * ROW_SUB, LANES), jnp.bfloat16),
        compiler_params=pltpu.CompilerParams(
            dimension_semantics=("arbitrary",), vmem_limit_bytes=VMEM_LIMIT),
        name="down",
    )(plan["blk_e"], plan["first"], plan["wslot"], plan["nxt_e"], plan["n_used"], hid, wd)


def _combine_kernel(dest_ref, x1_ref, route_ref, g_ref, y_hbm, o_ref, y0_buf, y1_buf, sem):
    i = pl.program_id(0)
    n = pl.num_programs(0)
    cur = i % 2
    n_tokens = n * CTM

    def start(step, slot):
        _start_row_gather(y_hbm, dest_ref, step * CTM, CTM, y0_buf.at[slot], sem.at[0, slot])
        _start_row_gather(y_hbm, dest_ref, n_tokens + step * CTM, CTM, y1_buf.at[slot], sem.at[1, slot])

    @pl.when(i == 0)
    def _():
        start(0, 0)

    @pl.when(i + 1 < n)
    def _():
        start(i + 1, 1 - cur)

    r = route_ref[...]
    _wait_row_gather(y_hbm, y0_buf.at[cur], sem.at[0, cur], CTM)
    _wait_row_gather(y_hbm, y1_buf.at[cur], sem.at[1, cur], CTM)
    y0 = _load_rows(y0_buf.at[cur], CTM).astype(jnp.float32)
    y1 = _load_rows(y1_buf.at[cur], CTM).astype(jnp.float32)
    xo = x1_ref[...] + (y0 * r[:, 0:1] + y1 * r[:, 1:2])
    o_ref[...] = _rms(xo, g_ref[...])


def _combine(dest_rows, x1, route, g, ybuf):
    t = x1.shape[0]
    tile = lambda w: pl.BlockSpec((CTM, w), lambda i, *_: (i, 0))
    return pl.pallas_call(
        _combine_kernel,
        grid_spec=pltpu.PrefetchScalarGridSpec(
            num_scalar_prefetch=1,
            grid=(t // CTM,),
            in_specs=[tile(D_MODEL), tile(LANES),
                      pl.BlockSpec((1, D_MODEL), lambda i, *_: (0, 0)),
                      pl.BlockSpec(memory_space=pl.ANY)],
            out_specs=tile(D_MODEL),
            scratch_shapes=[pltpu.VMEM((2, CTM * ROW_SUB, LANES), jnp.bfloat16),
                            pltpu.VMEM((2, CTM * ROW_SUB, LANES), jnp.bfloat16),
                            pltpu.SemaphoreType.DMA((2, 2))],
        ),
        out_shape=jax.ShapeDtypeStruct((t, D_MODEL), jnp.float32),
        compiler_params=pltpu.CompilerParams(
            dimension_semantics=("arbitrary",), vmem_limit_bytes=VMEM_LIMIT),
        name="combine",
    )(dest_rows, x1, route, g, ybuf)


def _dispatch_plan(e_flat):
    a = e_flat.shape[0]
    i32 = jnp.int32
    experts = jnp.arange(N_EXPERTS, dtype=i32)
    onehot = (e_flat[:, None] == experts[None, :]).astype(i32)
    rank = jnp.cumsum(onehot, axis=0) - onehot
    counts = jnp.sum(onehot, axis=0)
    padded = ((counts + ROW_BLK - 1) // ROW_BLK) * ROW_BLK
    pends = jnp.cumsum(padded)
    pstarts = pends - padded
    dest = jnp.sum((rank + pstarts[None, :]) * onehot, axis=1).astype(i32)
    n_blocks = (a + N_EXPERTS * (ROW_BLK - 1) + ROW_BLK - 1) // ROW_BLK
    blk_start = jnp.arange(n_blocks, dtype=i32) * ROW_BLK
    blk_e = jnp.clip(jnp.sum(pends[None, :] <= blk_start[:, None], axis=1), 0, N_EXPERTS - 1).astype(i32)
    first = jnp.concatenate([jnp.ones((1,), i32), (blk_e[1:] != blk_e[:-1]).astype(i32)])
    wslot = (jnp.cumsum(first) - 1) % 2
    later_used = (experts[None, :] > experts[:, None]) & (counts[None, :] > 0)
    nxt = jnp.min(jnp.where(later_used, experts[None, :], N_EXPERTS), axis=1)
    nxt = jnp.where(nxt == N_EXPERTS, experts, nxt)
    return {
        "dest": dest,
        "blk_e": blk_e,
        "first": first,
        "wslot": wslot.astype(i32),
        "nxt_e": nxt[blk_e].astype(i32),
        "n_used": (pends[-1] // ROW_BLK).astype(i32).reshape(1),
        "n_valid": jnp.clip(counts[blk_e] - (blk_start - pstarts[blk_e]), 0, ROW_BLK).astype(i32),
    }


def kernel(x, norm_mix_g, w_in, w_pool, pool_scale, conv_w, w_out, norm_ffn_g, w_router_group, b_router_group, w_router_expert, b_router_expert, w_gate, w_up, w_down, norm_final_g):
    bt, s, d = x.shape
    t = bt * s
    bf = jnp.bfloat16
    xt = x.reshape(t, d)
    assert norm_mix_g.shape[0] == 1, "the final norm is fused into the last layer's combine"
    for l in range(norm_mix_g.shape[0]):
        w_r = jnp.concatenate([w_router_group[l], w_router_expert[l]], axis=1)
        w_r = jnp.pad(w_r, ((0, 0), (0, LANES - w_r.shape[1]))).astype(bf)
        b_r = jnp.concatenate([b_router_group[l], b_router_expert[l]])
        b_r = jnp.pad(b_r, (0, LANES - b_r.shape[0])).reshape(1, LANES)
        x1, h2, route = _mixer(
            xt, norm_mix_g[l].reshape(1, d), w_in[l], w_pool[l].astype(bf),
            pool_scale[l].reshape(1, MIX_A), conv_w[l].T, w_out[l],
            norm_ffn_g[l].reshape(1, d), w_r, b_r)
        e_flat = jnp.concatenate([route[:, 2 + k] for k in range(TOP_K)]).astype(jnp.int32)
        plan = _dispatch_plan(e_flat)
        hid = _gate_up(plan, h2, w_gate[l], w_up[l])
        ybuf = _down(plan, hid, w_down[l])
        xt = _combine(plan["dest"] * ROW_SUB, x1, route, norm_final_g.reshape(1, d), ybuf)
    return xt.reshape(bt, s, d)
```

```python
import jax
import jax.numpy as jnp
from jax import lax
from jax.experimental import pallas as pl
from jax.experimental.pallas import tpu as pltpu

D_MODEL = 2048
MIX_A = 1024
MIX_B = 1024
POOL_WINDOWS = (2, 4, 8, 16)
POOL_CH = MIX_A // len(POOL_WINDOWS)
CONV_W = 3
N_GROUPS = 4
E_PER_GROUP = 8
N_EXPERTS = N_GROUPS * E_PER_GROUP
TOP_K = 2
D_EXPERT = D_MODEL // 2
EPS = 1e-6

LANES = 128
ROW_SUB = D_MODEL // LANES
HIST = 16
TM = 256
CTM = 256
ROW_BLK = 256
VMEM_LIMIT = 56 * 1024 * 1024
WEIGHT_DMA_PRIORITY = 1
WEIGHT_DMA_CHUNKS = 8
WEIGHT_SLAB = 64
WEIGHT_LAND_SLOTS = 4
DMA_QUEUES = 2
GATHER_SHIFT = 4
GATHER_UNROLL = 1 << GATHER_SHIFT
SCALAR_UNROLL = 16

_NEG = -1e30


def _rms(x, g):
    return x * lax.rsqrt(jnp.mean(x * x, axis=-1, keepdims=True) + EPS) * g


def _bdot(a, b):
    return jnp.dot(a, b, preferred_element_type=jnp.float32)


def _store_rows(ref, val):
    ref[...] = val.astype(jnp.bfloat16).reshape(val.shape[0] * ROW_SUB, LANES)


def _load_rows(ref, n):
    return ref[...].reshape(n, D_MODEL)


def _load_cast_weight(w_hbm, w_bf, land, sem):
    rows, cols = w_hbm.shape
    n_slabs = rows // WEIGHT_SLAB
    ahead = WEIGHT_LAND_SLOTS - 1

    def slab_copy(c):
        k = c % WEIGHT_LAND_SLOTS
        return pltpu.make_async_copy(w_hbm.at[pl.ds(c * WEIGHT_SLAB, WEIGHT_SLAB)],
                                     land.at[k, :, pl.ds(0, cols)], sem.at[k])

    for c in range(min(ahead, n_slabs)):
        slab_copy(c).start()
    for c in range(n_slabs):
        if c + ahead < n_slabs:
            slab_copy(c + ahead).start()
        slab_copy(c).wait()
        w_bf[pl.ds(c * WEIGHT_SLAB, WEIGHT_SLAB), :] = (
            land[c % WEIGHT_LAND_SLOTS, :, 0:cols].astype(jnp.bfloat16))


def _mixer_kernel(x_ref, x_next_ref, g1_ref, w_in_hbm, w_pool_ref, pscale_ref, convw_ref, w_out_hbm,
                  g2_ref, w_r_ref, b_r_ref,
                  x1_ref, h2_ref, route_ref,
                  ext_u, ext_z, h_buf, w_in_ref, w_out_ref, land, wsem):
    i = pl.program_id(0)

    @pl.when(i == 0)
    def _():
        ext_u[0:HIST, :] = jnp.zeros((HIST, MIX_A), jnp.float32)
        ext_z[0:HIST, :] = jnp.zeros((HIST, MIX_B), jnp.float32)
        h_buf[...] = _rms(x_ref[...], g1_ref[...]).astype(jnp.bfloat16)
        _load_cast_weight(w_in_hbm, w_in_ref, land, wsem)
        _load_cast_weight(w_out_hbm, w_out_ref, land, wsem)

    x = x_ref[...]
    h = h_buf[...]

    ext_u[HIST:HIST + TM, :] = _bdot(h, w_in_ref[:, 0:MIX_A])
    c_gate = _bdot(h, w_in_ref[:, MIX_A + MIX_B:MIX_A + 2 * MIX_B])
    v = _bdot(h, w_in_ref[:, MIX_A + 2 * MIX_B:MIX_A + 3 * MIX_B])
    z = c_gate * v
    ext_z[HIST:HIST + TM, :] = z
    b_gate = _bdot(h, w_in_ref[:, MIX_A:MIX_A + MIX_B])

    y = ext_z[HIST - 2:HIST - 2 + TM, :] * convw_ref[0:1, :]
    y = y + ext_z[HIST - 1:HIST - 1 + TM, :] * convw_ref[1:2, :]
    y = y + z * convw_ref[2:3, :]
    y_b = b_gate * y
    ext_z[0:HIST, :] = ext_z[TM:TM + HIST, :]
    out_b = _bdot(y_b.astype(jnp.bfloat16), w_out_ref[MIX_A:MIX_A + MIX_B, :])

    row = lax.broadcasted_iota(jnp.int32, (TM, 1), 0) + i * TM + 1
    y_a = []
    for gi, w in enumerate(POOL_WINDOWS):
        c0 = gi * POOL_CH
        u = ext_u[HIST:HIST + TM, c0:c0 + POOL_CH]
        acc = u
        for s in range(1, w):
            acc = acc + ext_u[HIST - s:HIST - s + TM, c0:c0 + POOL_CH]
        cnt = jnp.minimum(row, w).astype(jnp.float32)
        pooled = (acc / cnt - u).astype(jnp.bfloat16)
        y_a.append(_bdot(pooled, w_pool_ref[gi]))
    y_a = jnp.concatenate(y_a, axis=-1) * pscale_ref[...]
    ext_u[0:HIST, :] = ext_u[TM:TM + HIST, :]
    out_a = _bdot(y_a.astype(jnp.bfloat16), w_out_ref[0:MIX_A, :])

    x1 = x + (out_a + out_b)
    x1_ref[...] = x1

    h2_f32 = _rms(x1, g2_ref[...])
    _store_rows(h2_ref, h2_f32)
    h2 = h2_f32.astype(jnp.bfloat16)

    logits = _bdot(h2, w_r_ref[...]) + b_r_ref[...]
    lane = lax.broadcasted_iota(jnp.int32, (TM, LANES), 1)
    lane_f = lane.astype(jnp.float32)

    def first_argmax(vals, vmax):
        return jnp.min(jnp.where(vals == vmax, lane_f, float(LANES)), axis=-1, keepdims=True)

    gl = jnp.where(lane < N_GROUPS, logits, _NEG)
    gmax = jnp.max(gl, axis=-1, keepdims=True)
    g_w = 1.0 / jnp.sum(jnp.exp(gl - gmax), axis=-1, keepdims=True)
    grp = first_argmax(gl, gmax).astype(jnp.int32)
    lo = N_GROUPS + E_PER_GROUP * grp
    el = jnp.where((lane >= lo) & (lane < lo + E_PER_GROUP), logits, _NEG)
    emax = jnp.max(el, axis=-1, keepdims=True)
    idx1 = first_argmax(el, emax)
    esum = jnp.sum(jnp.exp(el - emax), axis=-1, keepdims=True)
    el2 = jnp.where(lane_f == idx1, _NEG, el)
    e2max = jnp.max(el2, axis=-1, keepdims=True)
    idx2 = first_argmax(el2, e2max)
    p1 = 1.0 / esum
    p2 = jnp.exp(e2max - emax) / esum
    tot = p1 + p2
    w1 = g_w * (p1 / tot)
    w2 = g_w * (p2 / tot)
    route = jnp.where(lane == 0, w1, 0.0)
    route = jnp.where(lane == 1, w2, route)
    route = jnp.where(lane == 2, idx1 - N_GROUPS, route)
    route = jnp.where(lane == 3, idx2 - N_GROUPS, route)
    route_ref[...] = route

    h_buf[...] = _rms(x_next_ref[...], g1_ref[...]).astype(jnp.bfloat16)


def _mixer(x, g1, w_in, w_pool, pscale, convw, w_out, g2, w_r, b_r):
    t = x.shape[0]
    const = lambda shape: pl.BlockSpec(shape, lambda i: (0,) * len(shape),
                                       pipeline_mode=pl.Buffered(1))
    return pl.pallas_call(
        _mixer_kernel,
        grid=(t // TM,),
        in_specs=[
            pl.BlockSpec((TM, D_MODEL), lambda i: (i, 0)),
            pl.BlockSpec((TM, D_MODEL), lambda i: (jnp.minimum(i + 1, t // TM - 1), 0)),
            const((1, D_MODEL)),
            pl.BlockSpec(memory_space=pl.ANY),
            const(w_pool.shape),
            const((1, MIX_A)),
            const((CONV_W, MIX_B)),
            pl.BlockSpec(memory_space=pl.ANY),
            const((1, D_MODEL)),
            const(w_r.shape),
            const((1, LANES)),
        ],
        out_specs=[
            pl.BlockSpec((TM, D_MODEL), lambda i: (i, 0)),
            pl.BlockSpec((TM * ROW_SUB, LANES), lambda i: (i, 0)),
            pl.BlockSpec((TM, LANES), lambda i: (i, 0)),
        ],
        out_shape=[
            jax.ShapeDtypeStruct((t, D_MODEL), jnp.float32),
            jax.ShapeDtypeStruct((t * ROW_SUB, LANES), jnp.bfloat16),
            jax.ShapeDtypeStruct((t, LANES), jnp.float32),
        ],
        scratch_shapes=[
            pltpu.VMEM((TM + HIST, MIX_A), jnp.float32),
            pltpu.VMEM((TM + HIST, MIX_B), jnp.float32),
            pltpu.VMEM((TM, D_MODEL), jnp.bfloat16),
            pltpu.VMEM(w_in.shape, jnp.bfloat16),
            pltpu.VMEM(w_out.shape, jnp.bfloat16),
            pltpu.VMEM((WEIGHT_LAND_SLOTS, WEIGHT_SLAB, w_in.shape[1]), jnp.float32),
            pltpu.SemaphoreType.DMA((WEIGHT_LAND_SLOTS,)),
        ],
        compiler_params=pltpu.CompilerParams(
            dimension_semantics=("arbitrary",), vmem_limit_bytes=VMEM_LIMIT),
        name="mixer",
    )(x, x, g1, w_in, w_pool, pscale, convw, w_out, g2, w_r, b_r)


def _row_copy(src_hbm, row_start, dst_ref, r, sem):
    dst_start = r * ROW_SUB if isinstance(r, int) else pl.multiple_of(r * ROW_SUB, ROW_SUB)
    return pltpu.make_async_copy(src_hbm.at[pl.ds(pl.multiple_of(row_start, ROW_SUB), ROW_SUB)],
                                 dst_ref.at[pl.ds(dst_start, ROW_SUB)], sem)


def _gather_rows_issued(n_rows):
    if isinstance(n_rows, int):
        return n_rows
    return ((n_rows + GATHER_UNROLL - 1) >> GATHER_SHIFT) << GATHER_SHIFT


def _start_row_gather(src_hbm, idx_ref, base, n_rows, dst_ref, sem):
    if isinstance(n_rows, int):
        for r in range(n_rows):
            _row_copy(src_hbm, idx_ref[base + r], dst_ref, r, sem).start(priority=r % DMA_QUEUES)
        return

    def body(i, carry):
        for u in range(GATHER_UNROLL):
            r = i * GATHER_UNROLL + u
            src_row = idx_ref[base + jnp.minimum(r, n_rows - 1)]
            _row_copy(src_hbm, src_row, dst_ref, r, sem).start(priority=u % DMA_QUEUES)
        return carry

    lax.fori_loop(0, _gather_rows_issued(n_rows) >> GATHER_SHIFT, body, 0)


def _wait_row_gather(src_hbm, dst_ref, sem, n_rows):
    n = _gather_rows_issued(n_rows) * ROW_SUB
    pltpu.make_async_copy(src_hbm.at[pl.ds(0, n)], dst_ref.at[pl.ds(0, n)], sem).wait()


def _weight_copies(w_hbm, e, buf, slot, k, sem):
    rows = w_hbm.shape[1] // WEIGHT_DMA_CHUNKS
    return [pltpu.make_async_copy(w_hbm.at[e, pl.ds(c * rows, rows)],
                                  buf.at[slot, k, pl.ds(c * rows, rows)], sem.at[slot])
            for c in range(WEIGHT_DMA_CHUNKS)]


def _start_weight(*args):
    for cp in _weight_copies(*args):
        cp.start(priority=WEIGHT_DMA_PRIORITY)


def _wait_weight(*args):
    for cp in _weight_copies(*args):
        cp.wait()


def _build_row_sources(dest_ref, row_ref):
    n_tokens = dest_ref.shape[0] // TOP_K

    def scatter(i, carry):
        for u in range(SCALAR_UNROLL):
            tok = i * SCALAR_UNROLL + u
            for k in range(TOP_K):
                row_ref[dest_ref[k * n_tokens + tok]] = tok * ROW_SUB
        return carry

    lax.fori_loop(0, n_tokens // SCALAR_UNROLL, scatter, 0)


def _gate_up_kernel(blk_e_ref, first_ref, wslot_ref, nxt_e_ref, n_used_ref, n_valid_ref, dest_ref,
                    h2_hbm, wg_hbm, wu_hbm, hid_ref,
                    xg, wbuf, row_ref, gsem, wsem):
    b = pl.program_id(0)
    n_used = n_used_ref[0]
    e = blk_e_ref[b]
    slot = wslot_ref[b]

    @pl.when(b == 0)
    def _():
        _start_weight(wg_hbm, e, wbuf, slot, 0, wsem)
        _start_weight(wu_hbm, e, wbuf, slot, 1, wsem)
        xg[...] = jnp.zeros_like(xg)
        _build_row_sources(dest_ref, row_ref)
        _start_row_gather(h2_hbm, row_ref, 0, n_valid_ref[0], xg.at[0], gsem.at[0])

    @pl.when(b < n_used)
    def _():
        cur = b % 2

        @pl.when(b + 1 < n_used)
        def _():
            _start_row_gather(h2_hbm, row_ref, (b + 1) * ROW_BLK, n_valid_ref[b + 1],
                              xg.at[1 - cur], gsem.at[1 - cur])

        @pl.when(first_ref[b] == 1)
        def _():
            _wait_weight(wg_hbm, e, wbuf, slot, 0, wsem)
            _wait_weight(wu_hbm, e, wbuf, slot, 1, wsem)
            nxt = nxt_e_ref[b]

            @pl.when(nxt != e)
            def _():
                _start_weight(wg_hbm, nxt, wbuf, 1 - slot, 0, wsem)
                _start_weight(wu_hbm, nxt, wbuf, 1 - slot, 1, wsem)


        _wait_row_gather(h2_hbm, xg.at[cur], gsem.at[cur], n_valid_ref[b])
        xb = _load_rows(xg.at[cur], ROW_BLK)
        gate = _bdot(xb, wbuf[slot, 0].astype(jnp.bfloat16))
        up = _bdot(xb, wbuf[slot, 1].astype(jnp.bfloat16))
        hid_ref[...] = (gate * jax.nn.sigmoid(gate) * up).astype(jnp.bfloat16)

    @pl.when(b >= n_used)
    def _():
        hid_ref[...] = jnp.zeros_like(hid_ref)


def _gate_up(plan, h2, wg, wu):
    n_blocks = plan["blk_e"].shape[0]
    any_spec = pl.BlockSpec(memory_space=pl.ANY)
    return pl.pallas_call(
        _gate_up_kernel,
        grid_spec=pltpu.PrefetchScalarGridSpec(
            num_scalar_prefetch=7,
            grid=(n_blocks,),
            in_specs=[any_spec, any_spec, any_spec],
            out_specs=pl.BlockSpec((ROW_BLK, D_EXPERT), lambda b, *_: (b, 0)),
            scratch_shapes=[
                pltpu.VMEM((2, ROW_BLK * ROW_SUB, LANES), jnp.bfloat16),
                pltpu.VMEM((2, 2, D_MODEL, D_EXPERT), jnp.float32),
                pltpu.SMEM((n_blocks * ROW_BLK,), jnp.int32),
                pltpu.SemaphoreType.DMA((2,)),
                pltpu.SemaphoreType.DMA((2,)),
            ],
        ),
        out_shape=jax.ShapeDtypeStruct((n_blocks * ROW_BLK, D_EXPERT), jnp.bfloat16),
        compiler_params=pltpu.CompilerParams(
            dimension_semantics=("arbitrary",), vmem_limit_bytes=VMEM_LIMIT),
        name="gate_up",
    )(plan["blk_e"], plan["first"], plan["wslot"], plan["nxt_e"], plan["n_used"], plan["n_valid"],
      plan["dest"], h2, wg, wu)


def _down_kernel(blk_e_ref, first_ref, wslot_ref, nxt_e_ref, n_used_ref,
                 hid_ref, wd_hbm, y_ref, wbuf, wsem):
    b = pl.program_id(0)
    n_used = n_used_ref[0]
    e = blk_e_ref[b]
    slot = wslot_ref[b]

    @pl.when(b == 0)
    def _():
        _start_weight(wd_hbm, e, wbuf, slot, 0, wsem)

    @pl.when(b < n_used)
    def _():
        @pl.when(first_ref[b] == 1)
        def _():
            _wait_weight(wd_hbm, e, wbuf, slot, 0, wsem)
            nxt = nxt_e_ref[b]

            @pl.when(nxt != e)
            def _():
                _start_weight(wd_hbm, nxt, wbuf, 1 - slot, 0, wsem)


        _store_rows(y_ref, _bdot(hid_ref[...], wbuf[slot, 0].astype(jnp.bfloat16)))

    @pl.when(b >= n_used)
    def _():
        y_ref[...] = jnp.zeros_like(y_ref)


def _down(plan, hid, wd):
    n_blocks = plan["blk_e"].shape[0]
    return pl.pallas_call(
        _down_kernel,
        grid_spec=pltpu.PrefetchScalarGridSpec(
            num_scalar_prefetch=5,
            grid=(n_blocks,),
            in_specs=[pl.BlockSpec((ROW_BLK, D_EXPERT), lambda b, *_: (b, 0)),
                      pl.BlockSpec(memory_space=pl.ANY)],
            out_specs=pl.BlockSpec((ROW_BLK * ROW_SUB, LANES), lambda b, *_: (b, 0)),
            scratch_shapes=[
                pltpu.VMEM((2, 1, D_EXPERT, D_MODEL), jnp.float32),
                pltpu.SemaphoreType.DMA((2,)),
            ],
        ),
        out_shape=jax.ShapeDtypeStruct((n_blocks * ROW_BLK * ROW_SUB, LANES), jnp.bfloat16),
        compiler_params=pltpu.CompilerParams(
            dimension_semantics=("arbitrary",), vmem_limit_bytes=VMEM_LIMIT),
        name="down",
    )(plan["blk_e"], plan["first"], plan["wslot"], plan["nxt_e"], plan["n_used"], hid, wd)


def _combine_kernel(dest_ref, x1_ref, route_ref, g_ref, y_hbm, o_ref, y0_buf, y1_buf, sem):
    i = pl.program_id(0)
    n = pl.num_programs(0)
    cur = i % 2
    n_tokens = n * CTM

    def start(step, slot):
        _start_row_gather(y_hbm, dest_ref, step * CTM, CTM, y0_buf.at[slot], sem.at[0, slot])
        _start_row_gather(y_hbm, dest_ref, n_tokens + step * CTM, CTM, y1_buf.at[slot], sem.at[1, slot])

    @pl.when(i == 0)
    def _():
        start(0, 0)

    @pl.when(i + 1 < n)
    def _():
        start(i + 1, 1 - cur)

    r = route_ref[...]
    _wait_row_gather(y_hbm, y0_buf.at[cur], sem.at[0, cur], CTM)
    _wait_row_gather(y_hbm, y1_buf.at[cur], sem.at[1, cur], CTM)
    y0 = _load_rows(y0_buf.at[cur], CTM).astype(jnp.float32)
    y1 = _load_rows(y1_buf.at[cur], CTM).astype(jnp.float32)
    xo = x1_ref[...] + (y0 * r[:, 0:1] + y1 * r[:, 1:2])
    o_ref[...] = _rms(xo, g_ref[...])


def _combine(dest_rows, x1, route, g, ybuf):
    t = x1.shape[0]
    tile = lambda w: pl.BlockSpec((CTM, w), lambda i, *_: (i, 0))
    return pl.pallas_call(
        _combine_kernel,
        grid_spec=pltpu.PrefetchScalarGridSpec(
            num_scalar_prefetch=1,
            grid=(t // CTM,),
            in_specs=[tile(D_MODEL), tile(LANES),
                      pl.BlockSpec((1, D_MODEL), lambda i, *_: (0, 0)),
                      pl.BlockSpec(memory_space=pl.ANY)],
            out_specs=tile(D_MODEL),
            scratch_shapes=[pltpu.VMEM((2, CTM * ROW_SUB, LANES), jnp.bfloat16),
                            pltpu.VMEM((2, CTM * ROW_SUB, LANES), jnp.bfloat16),
                            pltpu.SemaphoreType.DMA((2, 2))],
        ),
        out_shape=jax.ShapeDtypeStruct((t, D_MODEL), jnp.float32),
        compiler_params=pltpu.CompilerParams(
            dimension_semantics=("arbitrary",), vmem_limit_bytes=VMEM_LIMIT),
        name="combine",
    )(dest_rows, x1, route, g, ybuf)


def _dispatch_plan(e_flat):
    a = e_flat.shape[0]
    i32 = jnp.int32
    experts = jnp.arange(N_EXPERTS, dtype=i32)
    onehot = (e_flat[:, None] == experts[None, :]).astype(i32)
    rank = jnp.cumsum(onehot, axis=0) - onehot
    counts = jnp.sum(onehot, axis=0)
    padded = ((counts + ROW_BLK - 1) // ROW_BLK) * ROW_BLK
    pends = jnp.cumsum(padded)
    pstarts = pends - padded
    dest = jnp.sum((rank + pstarts[None, :]) * onehot, axis=1).astype(i32)
    n_blocks = (a + N_EXPERTS * (ROW_BLK - 1) + ROW_BLK - 1) // ROW_BLK
    blk_start = jnp.arange(n_blocks, dtype=i32) * ROW_BLK
    blk_e = jnp.clip(jnp.sum(pends[None, :] <= blk_start[:, None], axis=1), 0, N_EXPERTS - 1).astype(i32)
    first = jnp.concatenate([jnp.ones((1,), i32), (blk_e[1:] != blk_e[:-1]).astype(i32)])
    wslot = (jnp.cumsum(first) - 1) % 2
    later_used = (experts[None, :] > experts[:, None]) & (counts[None, :] > 0)
    nxt = jnp.min(jnp.where(later_used, experts[None, :], N_EXPERTS), axis=1)
    nxt = jnp.where(nxt == N_EXPERTS, experts, nxt)
    return {
        "dest": dest,
        "blk_e": blk_e,
        "first": first,
        "wslot": wslot.astype(i32),
        "nxt_e": nxt[blk_e].astype(i32),
        "n_used": (pends[-1] // ROW_BLK).astype(i32).reshape(1),
        "n_valid": jnp.clip(counts[blk_e] - (blk_start - pstarts[blk_e]), 0, ROW_BLK).astype(i32),
    }


def kernel(x, norm_mix_g, w_in, w_pool, pool_scale, conv_w, w_out, norm_ffn_g, w_router_group, b_router_group, w_router_expert, b_router_expert, w_gate, w_up, w_down, norm_final_g):
    bt, s, d = x.shape
    t = bt * s
    bf = jnp.bfloat16
    xt = x.reshape(t, d)
    assert norm_mix_g.shape[0] == 1, "the final norm is fused into the last layer's combine"
    for l in range(norm_mix_g.shape[0]):
        w_r = jnp.concatenate([w_router_group[l], w_router_expert[l]], axis=1)
        w_r = jnp.pad(w_r, ((0, 0), (0, LANES - w_r.shape[1]))).astype(bf)
        b_r = jnp.concatenate([b_router_group[l], b_router_expert[l]])
        b_r = jnp.pad(b_r, (0, LANES - b_r.shape[0])).reshape(1, LANES)
        x1, h2, route = _mixer(
            xt, norm_mix_g[l].reshape(1, d), w_in[l], w_pool[l].astype(bf),
            pool_scale[l].reshape(1, MIX_A), conv_w[l].T, w_out[l],
            norm_ffn_g[l].reshape(1, d), w_r, b_r)
        e_flat = jnp.concatenate([route[:, 2 + k] for k in range(TOP_K)]).astype(jnp.int32)
        plan = _dispatch_plan(e_flat)
        hid = _gate_up(plan, h2, w_gate[l], w_up[l])
        ybuf = _down(plan, hid, w_down[l])
        xt = _combine(plan["dest"] * ROW_SUB, x1, route, norm_final_g.reshape(1, d), ybuf)
    return xt.reshape(bt, s, d)
```

```python
import jax
import jax.numpy as jnp
from jax import lax
from jax.experimental import pallas as pl
from jax.experimental.pallas import tpu as pltpu

D_MODEL = 2048
MIX_A = 1024
MIX_B = 1024
POOL_WINDOWS = (2, 4, 8, 16)
POOL_CH = MIX_A // len(POOL_WINDOWS)
CONV_W = 3
N_GROUPS = 4
E_PER_GROUP = 8
N_EXPERTS = N_GROUPS * E_PER_GROUP
TOP_K = 2
D_EXPERT = D_MODEL // 2
EPS = 1e-6

LANES = 128
ROW_SUB = D_MODEL // LANES
HIST = 16
TM = 256
CTM = 256
ROW_BLK = 256
VMEM_LIMIT = 56 * 1024 * 1024
WEIGHT_DMA_PRIORITY = 1
WEIGHT_DMA_CHUNKS = 8
WEIGHT_SLAB = 64
WEIGHT_LAND_SLOTS = 4
DMA_QUEUES = 2
GATHER_SHIFT = 4
GATHER_UNROLL = 1 << GATHER_SHIFT
SCALAR_UNROLL = 16

_NEG = -1e30


def _rms(x, g):
    return x * lax.rsqrt(jnp.mean(x * x, axis=-1, keepdims=True) + EPS) * g


def _bdot(a, b):
    return jnp.dot(a, b, preferred_element_type=jnp.float32)


def _store_rows(ref, val):
    ref[...] = val.astype(jnp.bfloat16).reshape(val.shape[0] * ROW_SUB, LANES)


def _load_rows(ref, n):
    return ref[...].reshape(n, D_MODEL)


def _load_cast_weight(w_hbm, w_bf, land, sem):
    rows, cols = w_hbm.shape
    n_slabs = rows // WEIGHT_SLAB
    ahead = WEIGHT_LAND_SLOTS - 1

    def slab_copy(c):
        k = c % WEIGHT_LAND_SLOTS
        return pltpu.make_async_copy(w_hbm.at[pl.ds(c * WEIGHT_SLAB, WEIGHT_SLAB)],
                                     land.at[k, :, pl.ds(0, cols)], sem.at[k])

    for c in range(min(ahead, n_slabs)):
        slab_copy(c).start()
    for c in range(n_slabs):
        if c + ahead < n_slabs:
            slab_copy(c + ahead).start()
        slab_copy(c).wait()
        w_bf[pl.ds(c * WEIGHT_SLAB, WEIGHT_SLAB), :] = (
            land[c % WEIGHT_LAND_SLOTS, :, 0:cols].astype(jnp.bfloat16))


def _mixer_kernel(x_ref, x_next_ref, g1_ref, w_in_hbm, w_pool_ref, pscale_ref, convw_ref, w_out_hbm,
                  g2_ref, w_r_ref, b_r_ref,
                  x1_ref, h2_ref, route_ref,
                  ext_u, ext_z, h_buf, w_in_ref, w_out_ref, land, wsem):
    i = pl.program_id(0)

    @pl.when(i == 0)
    def _():
        ext_u[0:HIST, :] = jnp.zeros((HIST, MIX_A), jnp.float32)
        ext_z[0:HIST, :] = jnp.zeros((HIST, MIX_B), jnp.float32)
        h_buf[...] = _rms(x_ref[...], g1_ref[...]).astype(jnp.bfloat16)
        _load_cast_weight(w_in_hbm, w_in_ref, land, wsem)
        _load_cast_weight(w_out_hbm, w_out_ref, land, wsem)

    x = x_ref[...]
    h = h_buf[...]

    ext_u[HIST:HIST + TM, :] = _bdot(h, w_in_ref[:, 0:MIX_A])
    c_gate = _bdot(h, w_in_ref[:, MIX_A + MIX_B:MIX_A + 2 * MIX_B])
    v = _bdot(h, w_in_ref[:, MIX_A + 2 * MIX_B:MIX_A + 3 * MIX_B])
    z = c_gate * v
    ext_z[HIST:HIST + TM, :] = z
    b_gate = _bdot(h, w_in_ref[:, MIX_A:MIX_A + MIX_B])

    y = ext_z[HIST - 2:HIST - 2 + TM, :] * convw_ref[0:1, :]
    y = y + ext_z[HIST - 1:HIST - 1 + TM, :] * convw_ref[1:2, :]
    y = y + z * convw_ref[2:3, :]
    y_b = b_gate * y
    ext_z[0:HIST, :] = ext_z[TM:TM + HIST, :]
    out_b = _bdot(y_b.astype(jnp.bfloat16), w_out_ref[MIX_A:MIX_A + MIX_B, :])

    row = lax.broadcasted_iota(jnp.int32, (TM, 1), 0) + i * TM + 1
    y_a = []
    for gi, w in enumerate(POOL_WINDOWS):
        c0 = gi * POOL_CH
        u = ext_u[HIST:HIST + TM, c0:c0 + POOL_CH]
        acc = u
        for s in range(1, w):
            acc = acc + ext_u[HIST - s:HIST - s + TM, c0:c0 + POOL_CH]
        cnt = jnp.minimum(row, w).astype(jnp.float32)
        pooled = (acc / cnt - u).astype(jnp.bfloat16)
        y_a.append(_bdot(pooled, w_pool_ref[gi]))
    y_a = jnp.concatenate(y_a, axis=-1) * pscale_ref[...]
    ext_u[0:HIST, :] = ext_u[TM:TM + HIST, :]
    out_a = _bdot(y_a.astype(jnp.bfloat16), w_out_ref[0:MIX_A, :])

    x1 = x + (out_a + out_b)
    x1_ref[...] = x1

    h2_f32 = _rms(x1, g2_ref[...])
    _store_rows(h2_ref, h2_f32)
    h2 = h2_f32.astype(jnp.bfloat16)

    logits = _bdot(h2, w_r_ref[...]) + b_r_ref[...]
    lane = lax.broadcasted_iota(jnp.int32, (TM, LANES), 1)
    lane_f = lane.astype(jnp.float32)

    def first_argmax(vals, vmax):
        return jnp.min(jnp.where(vals == vmax, lane_f, float(LANES)), axis=-1, keepdims=True)

    gl = jnp.where(lane < N_GROUPS, logits, _NEG)
    gmax = jnp.max(gl, axis=-1, keepdims=True)
    g_w = 1.0 / jnp.sum(jnp.exp(gl - gmax), axis=-1, keepdims=True)
    grp = first_argmax(gl, gmax).astype(jnp.int32)
    lo = N_GROUPS + E_PER_GROUP * grp
    el = jnp.where((lane >= lo) & (lane < lo + E_PER_GROUP), logits, _NEG)
    emax = jnp.max(el, axis=-1, keepdims=True)
    idx1 = first_argmax(el, emax)
    esum = jnp.sum(jnp.exp(el - emax), axis=-1, keepdims=True)
    el2 = jnp.where(lane_f == idx1, _NEG, el)
    e2max = jnp.max(el2, axis=-1, keepdims=True)
    idx2 = first_argmax(el2, e2max)
    p1 = 1.0 / esum
    p2 = jnp.exp(e2max - emax) / esum
    tot = p1 + p2
    w1 = g_w * (p1 / tot)
    w2 = g_w * (p2 / tot)
    route = jnp.where(lane == 0, w1, 0.0)
    route = jnp.where(lane == 1, w2, route)
    route = jnp.where(lane == 2, idx1 - N_GROUPS, route)
    route = jnp.where(lane == 3, idx2 - N_GROUPS, route)
    route_ref[...] = route

    h_buf[...] = _rms(x_next_ref[...], g1_ref[...]).astype(jnp.bfloat16)


def _mixer(x, g1, w_in, w_pool, pscale, convw, w_out, g2, w_r, b_r):
    t = x.shape[0]
    const = lambda shape: pl.BlockSpec(shape, lambda i: (0,) * len(shape),
                                       pipeline_mode=pl.Buffered(1))
    return pl.pallas_call(
        _mixer_kernel,
        grid=(t // TM,),
        in_specs=[
            pl.BlockSpec((TM, D_MODEL), lambda i: (i, 0)),
            pl.BlockSpec((TM, D_MODEL), lambda i: (jnp.minimum(i + 1, t // TM - 1), 0)),
            const((1, D_MODEL)),
            pl.BlockSpec(memory_space=pl.ANY),
            const(w_pool.shape),
            const((1, MIX_A)),
            const((CONV_W, MIX_B)),
            pl.BlockSpec(memory_space=pl.ANY),
            const((1, D_MODEL)),
            const(w_r.shape),
            const((1, LANES)),
        ],
        out_specs=[
            pl.BlockSpec((TM, D_MODEL), lambda i: (i, 0)),
            pl.BlockSpec((TM * ROW_SUB, LANES), lambda i: (i, 0)),
            pl.BlockSpec((TM, LANES), lambda i: (i, 0)),
        ],
        out_shape=[
            jax.ShapeDtypeStruct((t, D_MODEL), jnp.float32),
            jax.ShapeDtypeStruct((t * ROW_SUB, LANES), jnp.bfloat16),
            jax.ShapeDtypeStruct((t, LANES), jnp.float32),
        ],
        scratch_shapes=[
            pltpu.VMEM((TM + HIST, MIX_A), jnp.float32),
            pltpu.VMEM((TM + HIST, MIX_B), jnp.float32),
            pltpu.VMEM((TM, D_MODEL), jnp.bfloat16),
            pltpu.VMEM(w_in.shape, jnp.bfloat16),
            pltpu.VMEM(w_out.shape, jnp.bfloat16),
            pltpu.VMEM((WEIGHT_LAND_SLOTS, WEIGHT_SLAB, w_in.shape[1]), jnp.float32),
            pltpu.SemaphoreType.DMA((WEIGHT_LAND_SLOTS,)),
        ],
        compiler_params=pltpu.CompilerParams(
            dimension_semantics=("arbitrary",), vmem_limit_bytes=VMEM_LIMIT),
        name="mixer",
    )(x, x, g1, w_in, w_pool, pscale, convw, w_out, g2, w_r, b_r)


def _row_copy(src_hbm, row_start, dst_ref, r, sem):
    dst_start = r * ROW_SUB if isinstance(r, int) else pl.multiple_of(r * ROW_SUB, ROW_SUB)
    return pltpu.make_async_copy(src_hbm.at[pl.ds(pl.multiple_of(row_start, ROW_SUB), ROW_SUB)],
                                 dst_ref.at[pl.ds(dst_start, ROW_SUB)], sem)


def _gather_rows_issued(n_rows):
    if isinstance(n_rows, int):
        return n_rows
    return ((n_rows + GATHER_UNROLL - 1) >> GATHER_SHIFT) << GATHER_SHIFT


def _start_row_gather(src_hbm, idx_ref, base, n_rows, dst_ref, sem):
    if isinstance(n_rows, int):
        for r in range(n_rows):
            _row_copy(src_hbm, idx_ref[base + r], dst_ref, r, sem).start(priority=r % DMA_QUEUES)
        return

    def body(i, carry):
        for u in range(GATHER_UNROLL):
            r = i * GATHER_UNROLL + u
            src_row = idx_ref[base + jnp.minimum(r, n_rows - 1)]
            _row_copy(src_hbm, src_row, dst_ref, r, sem).start(priority=0)
        return carry

    lax.fori_loop(0, _gather_rows_issued(n_rows) >> GATHER_SHIFT, body, 0)


def _wait_row_gather(src_hbm, dst_ref, sem, n_rows):
    n = _gather_rows_issued(n_rows) * ROW_SUB
    pltpu.make_async_copy(src_hbm.at[pl.ds(0, n)], dst_ref.at[pl.ds(0, n)], sem).wait()


def _weight_copies(w_hbm, e, buf, slot, k, sem):
    rows = w_hbm.shape[1] // WEIGHT_DMA_CHUNKS
    return [pltpu.make_async_copy(w_hbm.at[e, pl.ds(c * rows, rows)],
                                  buf.at[slot, k, pl.ds(c * rows, rows)], sem.at[slot])
            for c in range(WEIGHT_DMA_CHUNKS)]


def _start_weight(*args):
    for cp in _weight_copies(*args):
        cp.start(priority=WEIGHT_DMA_PRIORITY)


def _wait_weight(*args):
    for cp in _weight_copies(*args):
        cp.wait()


def _build_row_sources(dest_ref, row_ref):
    n_tokens = dest_ref.shape[0] // TOP_K

    def scatter(i, carry):
        for u in range(SCALAR_UNROLL):
            tok = i * SCALAR_UNROLL + u
            for k in range(TOP_K):
                row_ref[dest_ref[k * n_tokens + tok]] = tok * ROW_SUB
        return carry

    lax.fori_loop(0, n_tokens // SCALAR_UNROLL, scatter, 0)


def _gate_up_kernel(blk_e_ref, first_ref, wslot_ref, nxt_e_ref, n_used_ref, n_valid_ref, dest_ref,
                    h2_hbm, wg_hbm, wu_hbm, hid_ref,
                    xg, wbuf, row_ref, gsem, wsem):
    b = pl.program_id(0)
    n_used = n_used_ref[0]
    e = blk_e_ref[b]
    slot = wslot_ref[b]

    @pl.when(b == 0)
    def _():
        _start_weight(wg_hbm, e, wbuf, slot, 0, wsem)
        _start_weight(wu_hbm, e, wbuf, slot, 1, wsem)
        xg[...] = jnp.zeros_like(xg)
        _build_row_sources(dest_ref, row_ref)
        _start_row_gather(h2_hbm, row_ref, 0, n_valid_ref[0], xg.at[0], gsem.at[0])

    @pl.when(b < n_used)
    def _():
        cur = b % 2

        @pl.when(b + 1 < n_used)
        def _():
            _start_row_gather(h2_hbm, row_ref, (b + 1) * ROW_BLK, n_valid_ref[b + 1],
                              xg.at[1 - cur], gsem.at[1 - cur])

        @pl.when(first_ref[b] == 1)
        def _():
            _wait_weight(wg_hbm, e, wbuf, slot, 0, wsem)
            _wait_weight(wu_hbm, e, wbuf, slot, 1, wsem)
            nxt = nxt_e_ref[b]

            @pl.when(nxt != e)
            def _():
                _start_weight(wg_hbm, nxt, wbuf, 1 - slot, 0, wsem)
                _start_weight(wu_hbm, nxt, wbuf, 1 - slot, 1, wsem)


        _wait_row_gather(h2_hbm, xg.at[cur], gsem.at[cur], n_valid_ref[b])
        xb = _load_rows(xg.at[cur], ROW_BLK)
        gate = _bdot(xb, wbuf[slot, 0].astype(jnp.bfloat16))
        up = _bdot(xb, wbuf[slot, 1].astype(jnp.bfloat16))
        hid_ref[...] = (gate * jax.nn.sigmoid(gate) * up).astype(jnp.bfloat16)

    @pl.when(b >= n_used)
    def _():
        hid_ref[...] = jnp.zeros_like(hid_ref)


def _gate_up(plan, h2, wg, wu):
    n_blocks = plan["blk_e"].shape[0]
    any_spec = pl.BlockSpec(memory_space=pl.ANY)
    return pl.pallas_call(
        _gate_up_kernel,
        grid_spec=pltpu.PrefetchScalarGridSpec(
            num_scalar_prefetch=7,
            grid=(n_blocks,),
            in_specs=[any_spec, any_spec, any_spec],
            out_specs=pl.BlockSpec((ROW_BLK, D_EXPERT), lambda b, *_: (b, 0)),
            scratch_shapes=[
                pltpu.VMEM((2, ROW_BLK * ROW_SUB, LANES), jnp.bfloat16),
                pltpu.VMEM((2, 2, D_MODEL, D_EXPERT), jnp.float32),
                pltpu.SMEM((n_blocks * ROW_BLK,), jnp.int32),
                pltpu.SemaphoreType.DMA((2,)),
                pltpu.SemaphoreType.DMA((2,)),
            ],
        ),
        out_shape=jax.ShapeDtypeStruct((n_blocks * ROW_BLK, D_EXPERT), jnp.bfloat16),
        compiler_params=pltpu.CompilerParams(
            dimension_semantics=("arbitrary",), vmem_limit_bytes=VMEM_LIMIT),
        name="gate_up",
    )(plan["blk_e"], plan["first"], plan["wslot"], plan["nxt_e"], plan["n_used"], plan["n_valid"],
      plan["dest"], h2, wg, wu)


def _down_kernel(blk_e_ref, first_ref, wslot_ref, nxt_e_ref, n_used_ref,
                 hid_ref, wd_hbm, y_ref, wbuf, wsem):
    b = pl.program_id(0)
    n_used = n_used_ref[0]
    e = blk_e_ref[b]
    slot = wslot_ref[b]

    @pl.when(b == 0)
    def _():
        _start_weight(wd_hbm, e, wbuf, slot, 0, wsem)

    @pl.when(b < n_used)
    def _():
        @pl.when(first_ref[b] == 1)
        def _():
            _wait_weight(wd_hbm, e, wbuf, slot, 0, wsem)
            nxt = nxt_e_ref[b]

            @pl.when(nxt != e)
            def _():
                _start_weight(wd_hbm, nxt, wbuf, 1 - slot, 0, wsem)


        _store_rows(y_ref, _bdot(hid_ref[...], wbuf[slot, 0].astype(jnp.bfloat16)))

    @pl.when(b >= n_used)
    def _():
        y_ref[...] = jnp.zeros_like(y_ref)


def _down(plan, hid, wd):
    n_blocks = plan["blk_e"].shape[0]
    return pl.pallas_call(
        _down_kernel,
        grid_spec=pltpu.PrefetchScalarGridSpec(
            num_scalar_prefetch=5,
            grid=(n_blocks,),
            in_specs=[pl.BlockSpec((ROW_BLK, D_EXPERT), lambda b, *_: (b, 0)),
                      pl.BlockSpec(memory_space=pl.ANY)],
            out_specs=pl.BlockSpec((ROW_BLK * ROW_SUB, LANES), lambda b, *_: (b, 0)),
            scratch_shapes=[
                pltpu.VMEM((2, 1, D_EXPERT, D_MODEL), jnp.float32),
                pltpu.SemaphoreType.DMA((2,)),
            ],
        ),
        out_shape=jax.ShapeDtypeStruct((n_blocks * ROW_BLK * ROW_SUB, LANES), jnp.bfloat16),
        compiler_params=pltpu.CompilerParams(
            dimension_semantics=("arbitrary",), vmem_limit_bytes=VMEM_LIMIT),
        name="down",
    )(plan["blk_e"], plan["first"], plan["wslot"], plan["nxt_e"], plan["n_used"], hid, wd)


def _combine_kernel(dest_ref, x1_ref, route_ref, g_ref, y_hbm, o_ref, y0_buf, y1_buf, sem):
    i = pl.program_id(0)
    n = pl.num_programs(0)
    cur = i % 2
    n_tokens = n * CTM

    def start(step, slot):
        _start_row_gather(y_hbm, dest_ref, step * CTM, CTM, y0_buf.at[slot], sem.at[0, slot])
        _start_row_gather(y_hbm, dest_ref, n_tokens + step * CTM, CTM, y1_buf.at[slot], sem.at[1, slot])

    @pl.when(i == 0)
    def _():
        start(0, 0)

    @pl.when(i + 1 < n)
    def _():
        start(i + 1, 1 - cur)

    r = route_ref[...]
    _wait_row_gather(y_hbm, y0_buf.at[cur], sem.at[0, cur], CTM)
    _wait_row_gather(y_hbm, y1_buf.at[cur], sem.at[1, cur], CTM)
    y0 = _load_rows(y0_buf.at[cur], CTM).astype(jnp.float32)
    y1 = _load_rows(y1_buf.at[cur], CTM).astype(jnp.float32)
    xo = x1_ref[...] + (y0 * r[:, 0:1] + y1 * r[:, 1:2])
    o_ref[...] = _rms(xo, g_ref[...])


def _combine(dest_rows, x1, route, g, ybuf):
    t = x1.shape[0]
    tile = lambda w: pl.BlockSpec((CTM, w), lambda i, *_: (i, 0))
    return pl.pallas_call(
        _combine_kernel,
        grid_spec=pltpu.PrefetchScalarGridSpec(
            num_scalar_prefetch=1,
            grid=(t // CTM,),
            in_specs=[tile(D_MODEL), tile(LANES),
                      pl.BlockSpec((1, D_MODEL), lambda i, *_: (0, 0)),
                      pl.BlockSpec(memory_space=pl.ANY)],
            out_specs=tile(D_MODEL),
            scratch_shapes=[pltpu.VMEM((2, CTM * ROW_SUB, LANES), jnp.bfloat16),
                            pltpu.VMEM((2, CTM * ROW_SUB, LANES), jnp.bfloat16),
                            pltpu.SemaphoreType.DMA((2, 2))],
        ),
        out_shape=jax.ShapeDtypeStruct((t, D_MODEL), jnp.float32),
        compiler_params=pltpu.CompilerParams(
            dimension_semantics=("arbitrary",), vmem_limit_bytes=VMEM_LIMIT),
        name="combine",
    )(dest_rows, x1, route, g, ybuf)


def _dispatch_plan(e_flat):
    a = e_flat.shape[0]
    i32 = jnp.int32
    experts = jnp.arange(N_EXPERTS, dtype=i32)
    onehot = (e_flat[:, None] == experts[None, :]).astype(i32)
    rank = jnp.cumsum(onehot, axis=0) - onehot
    counts = jnp.sum(onehot, axis=0)
    padded = ((counts + ROW_BLK - 1) // ROW_BLK) * ROW_BLK
    pends = jnp.cumsum(padded)
    pstarts = pends - padded
    dest = jnp.sum((rank + pstarts[None, :]) * onehot, axis=1).astype(i32)
    n_blocks = (a + N_EXPERTS * (ROW_BLK - 1) + ROW_BLK - 1) // ROW_BLK
    blk_start = jnp.arange(n_blocks, dtype=i32) * ROW_BLK
    blk_e = jnp.clip(jnp.sum(pends[None, :] <= blk_start[:, None], axis=1), 0, N_EXPERTS - 1).astype(i32)
    first = jnp.concatenate([jnp.ones((1,), i32), (blk_e[1:] != blk_e[:-1]).astype(i32)])
    wslot = (jnp.cumsum(first) - 1) % 2
    later_used = (experts[None, :] > experts[:, None]) & (counts[None, :] > 0)
    nxt = jnp.min(jnp.where(later_used, experts[None, :], N_EXPERTS), axis=1)
    nxt = jnp.where(nxt == N_EXPERTS, experts, nxt)
    return {
        "dest": dest,
        "blk_e": blk_e,
        "first": first,
        "wslot": wslot.astype(i32),
        "nxt_e": nxt[blk_e].astype(i32),
        "n_used": (pends[-1] // ROW_BLK).astype(i32).reshape(1),
        "n_valid": jnp.clip(counts[blk_e] - (blk_start - pstarts[blk_e]), 0, ROW_BLK).astype(i32),
    }


def kernel(x, norm_mix_g, w_in, w_pool, pool_scale, conv_w, w_out, norm_ffn_g, w_router_group, b_router_group, w_router_expert, b_router_expert, w_gate, w_up, w_down, norm_final_g):
    bt, s, d = x.shape
    t = bt * s
    bf = jnp.bfloat16
    xt = x.reshape(t, d)
    assert norm_mix_g.shape[0] == 1, "the final norm is fused into the last layer's combine"
    for l in range(norm_mix_g.shape[0]):
        w_r = jnp.concatenate([w_router_group[l], w_router_expert[l]], axis=1)
        w_r = jnp.pad(w_r, ((0, 0), (0, LANES - w_r.shape[1]))).astype(bf)
        b_r = jnp.concatenate([b_router_group[l], b_router_expert[l]])
        b_r = jnp.pad(b_r, (0, LANES - b_r.shape[0])).reshape(1, LANES)
        x1, h2, route = _mixer(
            xt, norm_mix_g[l].reshape(1, d), w_in[l], w_pool[l].astype(bf),
            pool_scale[l].reshape(1, MIX_A), conv_w[l].T, w_out[l],
            norm_ffn_g[l].reshape(1, d), w_r, b_r)
        e_flat = jnp.concatenate([route[:, 2 + k] for k in range(TOP_K)]).astype(jnp.int32)
        plan = _dispatch_plan(e_flat)
        hid = _gate_up(plan, h2, w_gate[l], w_up[l])
        ybuf = _down(plan, hid, w_down[l])
        xt = _combine(plan["dest"] * ROW_SUB, x1, route, norm_final_g.reshape(1, d), ybuf)
    return xt.reshape(bt, s, d)
```

```python
import jax
import jax.numpy as jnp
from jax import lax
from jax.experimental import pallas as pl
from jax.experimental.pallas import tpu as pltpu

D_MODEL = 2048
MIX_A = 1024
MIX_B = 1024
POOL_WINDOWS = (2, 4, 8, 16)
POOL_CH = MIX_A // len(POOL_WINDOWS)
CONV_W = 3
N_GROUPS = 4
E_PER_GROUP = 8
N_EXPERTS = N_GROUPS * E_PER_GROUP
TOP_K = 2
D_EXPERT = D_MODEL // 2
EPS = 1e-6

LANES = 128
ROW_SUB = D_MODEL // LANES
HIST = 16
TM = 256
CTM = 256
ROW_BLK = 256
VMEM_LIMIT = 56 * 1024 * 1024
WEIGHT_DMA_PRIORITY = 1
WEIGHT_DMA_CHUNKS = 8
WEIGHT_SLAB = 64
WEIGHT_LAND_SLOTS = 8
DMA_QUEUES = 2
GATHER_SHIFT = 4
GATHER_UNROLL = 1 << GATHER_SHIFT
SCALAR_UNROLL = 16

_NEG = -1e30


def _rms(x, g):
    return x * lax.rsqrt(jnp.mean(x * x, axis=-1, keepdims=True) + EPS) * g


def _bdot(a, b):
    return jnp.dot(a, b, preferred_element_type=jnp.float32)


def _store_rows(ref, val):
    ref[...] = val.astype(jnp.bfloat16).reshape(val.shape[0] * ROW_SUB, LANES)


def _load_rows(ref, n):
    return ref[...].reshape(n, D_MODEL)


def _load_cast_weight(w_hbm, w_bf, land, sem):
    rows, cols = w_hbm.shape
    n_slabs = rows // WEIGHT_SLAB
    ahead = WEIGHT_LAND_SLOTS - 1

    def slab_copy(c):
        k = c % WEIGHT_LAND_SLOTS
        return pltpu.make_async_copy(w_hbm.at[pl.ds(c * WEIGHT_SLAB, WEIGHT_SLAB)],
                                     land.at[k, :, pl.ds(0, cols)], sem.at[k])

    for c in range(min(ahead, n_slabs)):
        slab_copy(c).start()
    for c in range(n_slabs):
        if c + ahead < n_slabs:
            slab_copy(c + ahead).start()
        slab_copy(c).wait()
        w_bf[pl.ds(c * WEIGHT_SLAB, WEIGHT_SLAB), :] = (
            land[c % WEIGHT_LAND_SLOTS, :, 0:cols].astype(jnp.bfloat16))


def _mixer_kernel(x_ref, x_next_ref, g1_ref, w_in_hbm, w_pool_ref, pscale_ref, convw_ref, w_out_hbm,
                  g2_ref, w_r_ref, b_r_ref,
                  x1_ref, h2_ref, route_ref,
                  ext_u, ext_z, h_buf, w_in_ref, w_out_ref, land, wsem):
    i = pl.program_id(0)

    @pl.when(i == 0)
    def _():
        ext_u[0:HIST, :] = jnp.zeros((HIST, MIX_A), jnp.float32)
        ext_z[0:HIST, :] = jnp.zeros((HIST, MIX_B), jnp.float32)
        h_buf[...] = _rms(x_ref[...], g1_ref[...]).astype(jnp.bfloat16)
        _load_cast_weight(w_in_hbm, w_in_ref, land, wsem)
        _load_cast_weight(w_out_hbm, w_out_ref, land, wsem)

    x = x_ref[...]
    h = h_buf[...]

    ext_u[HIST:HIST + TM, :] = _bdot(h, w_in_ref[:, 0:MIX_A])
    c_gate = _bdot(h, w_in_ref[:, MIX_A + MIX_B:MIX_A + 2 * MIX_B])
    v = _bdot(h, w_in_ref[:, MIX_A + 2 * MIX_B:MIX_A + 3 * MIX_B])
    z = c_gate * v
    ext_z[HIST:HIST + TM, :] = z
    b_gate = _bdot(h, w_in_ref[:, MIX_A:MIX_A + MIX_B])

    y = ext_z[HIST - 2:HIST - 2 + TM, :] * convw_ref[0:1, :]
    y = y + ext_z[HIST - 1:HIST - 1 + TM, :] * convw_ref[1:2, :]
    y = y + z * convw_ref[2:3, :]
    y_b = b_gate * y
    ext_z[0:HIST, :] = ext_z[TM:TM + HIST, :]
    out_b = _bdot(y_b.astype(jnp.bfloat16), w_out_ref[MIX_A:MIX_A + MIX_B, :])

    row = lax.broadcasted_iota(jnp.int32, (TM, 1), 0) + i * TM + 1
    y_a = []
    for gi, w in enumerate(POOL_WINDOWS):
        c0 = gi * POOL_CH
        u = ext_u[HIST:HIST + TM, c0:c0 + POOL_CH]
        acc = u
        for s in range(1, w):
            acc = acc + ext_u[HIST - s:HIST - s + TM, c0:c0 + POOL_CH]
        cnt = jnp.minimum(row, w).astype(jnp.float32)
        pooled = (acc / cnt - u).astype(jnp.bfloat16)
        y_a.append(_bdot(pooled, w_pool_ref[gi]))
    y_a = jnp.concatenate(y_a, axis=-1) * pscale_ref[...]
    ext_u[0:HIST, :] = ext_u[TM:TM + HIST, :]
    out_a = _bdot(y_a.astype(jnp.bfloat16), w_out_ref[0:MIX_A, :])

    x1 = x + (out_a + out_b)
    x1_ref[...] = x1

    h2_f32 = _rms(x1, g2_ref[...])
    _store_rows(h2_ref, h2_f32)
    h2 = h2_f32.astype(jnp.bfloat16)

    logits = _bdot(h2, w_r_ref[...]) + b_r_ref[...]
    lane = lax.broadcasted_iota(jnp.int32, (TM, LANES), 1)
    lane_f = lane.astype(jnp.float32)

    def first_argmax(vals, vmax):
        return jnp.min(jnp.where(vals == vmax, lane_f, float(LANES)), axis=-1, keepdims=True)

    gl = jnp.where(lane < N_GROUPS, logits, _NEG)
    gmax = jnp.max(gl, axis=-1, keepdims=True)
    g_w = 1.0 / jnp.sum(jnp.exp(gl - gmax), axis=-1, keepdims=True)
    grp = first_argmax(gl, gmax).astype(jnp.int32)
    lo = N_GROUPS + E_PER_GROUP * grp
    el = jnp.where((lane >= lo) & (lane < lo + E_PER_GROUP), logits, _NEG)
    emax = jnp.max(el, axis=-1, keepdims=True)
    idx1 = first_argmax(el, emax)
    esum = jnp.sum(jnp.exp(el - emax), axis=-1, keepdims=True)
    el2 = jnp.where(lane_f == idx1, _NEG, el)
    e2max = jnp.max(el2, axis=-1, keepdims=True)
    idx2 = first_argmax(el2, e2max)
    p1 = 1.0 / esum
    p2 = jnp.exp(e2max - emax) / esum
    tot = p1 + p2
    w1 = g_w * (p1 / tot)
    w2 = g_w * (p2 / tot)
    route = jnp.where(lane == 0, w1, 0.0)
    route = jnp.where(lane == 1, w2, route)
    route = jnp.where(lane == 2, idx1 - N_GROUPS, route)
    route = jnp.where(lane == 3, idx2 - N_GROUPS, route)
    route_ref[...] = route

    h_buf[...] = _rms(x_next_ref[...], g1_ref[...]).astype(jnp.bfloat16)


def _mixer(x, g1, w_in, w_pool, pscale, convw, w_out, g2, w_r, b_r):
    t = x.shape[0]
    const = lambda shape: pl.BlockSpec(shape, lambda i: (0,) * len(shape),
                                       pipeline_mode=pl.Buffered(1))
    return pl.pallas_call(
        _mixer_kernel,
        grid=(t // TM,),
        in_specs=[
            pl.BlockSpec((TM, D_MODEL), lambda i: (i, 0)),
            pl.BlockSpec((TM, D_MODEL), lambda i: (jnp.minimum(i + 1, t // TM - 1), 0)),
            const((1, D_MODEL)),
            pl.BlockSpec(memory_space=pl.ANY),
            const(w_pool.shape),
            const((1, MIX_A)),
            const((CONV_W, MIX_B)),
            pl.BlockSpec(memory_space=pl.ANY),
            const((1, D_MODEL)),
            const(w_r.shape),
            const((1, LANES)),
        ],
        out_specs=[
            pl.BlockSpec((TM, D_MODEL), lambda i: (i, 0)),
            pl.BlockSpec((TM * ROW_SUB, LANES), lambda i: (i, 0)),
            pl.BlockSpec((TM, LANES), lambda i: (i, 0)),
        ],
        out_shape=[
            jax.ShapeDtypeStruct((t, D_MODEL), jnp.float32),
            jax.ShapeDtypeStruct((t * ROW_SUB, LANES), jnp.bfloat16),
            jax.ShapeDtypeStruct((t, LANES), jnp.float32),
        ],
        scratch_shapes=[
            pltpu.VMEM((TM + HIST, MIX_A), jnp.float32),
            pltpu.VMEM((TM + HIST, MIX_B), jnp.float32),
            pltpu.VMEM((TM, D_MODEL), jnp.bfloat16),
            pltpu.VMEM(w_in.shape, jnp.bfloat16),
            pltpu.VMEM(w_out.shape, jnp.bfloat16),
            pltpu.VMEM((WEIGHT_LAND_SLOTS, WEIGHT_SLAB, w_in.shape[1]), jnp.float32),
            pltpu.SemaphoreType.DMA((WEIGHT_LAND_SLOTS,)),
        ],
        compiler_params=pltpu.CompilerParams(
            dimension_semantics=("arbitrary",), vmem_limit_bytes=VMEM_LIMIT),
        name="mixer",
    )(x, x, g1, w_in, w_pool, pscale, convw, w_out, g2, w_r, b_r)


def _row_copy(src_hbm, row_start, dst_ref, r, sem):
    dst_start = r * ROW_SUB if isinstance(r, int) else pl.multiple_of(r * ROW_SUB, ROW_SUB)
    return pltpu.make_async_copy(src_hbm.at[pl.ds(pl.multiple_of(row_start, ROW_SUB), ROW_SUB)],
                                 dst_ref.at[pl.ds(dst_start, ROW_SUB)], sem)


def _gather_rows_issued(n_rows):
    if isinstance(n_rows, int):
        return n_rows
    return ((n_rows + GATHER_UNROLL - 1) >> GATHER_SHIFT) << GATHER_SHIFT


def _start_row_gather(src_hbm, idx_ref, base, n_rows, dst_ref, sem):
    if isinstance(n_rows, int):
        for r in range(n_rows):
            _row_copy(src_hbm, idx_ref[base + r], dst_ref, r, sem).start(priority=r % DMA_QUEUES)
        return

    def body(i, carry):
        for u in range(GATHER_UNROLL):
            r = i * GATHER_UNROLL + u
            src_row = idx_ref[base + jnp.minimum(r, n_rows - 1)]
            _row_copy(src_hbm, src_row, dst_ref, r, sem).start(priority=u % DMA_QUEUES)
        return carry

    lax.fori_loop(0, _gather_rows_issued(n_rows) >> GATHER_SHIFT, body, 0)


def _wait_row_gather(src_hbm, dst_ref, sem, n_rows):
    n = _gather_rows_issued(n_rows) * ROW_SUB
    pltpu.make_async_copy(src_hbm.at[pl.ds(0, n)], dst_ref.at[pl.ds(0, n)], sem).wait()


def _weight_copies(w_hbm, e, buf, slot, k, sem):
    rows = w_hbm.shape[1] // WEIGHT_DMA_CHUNKS
    return [pltpu.make_async_copy(w_hbm.at[e, pl.ds(c * rows, rows)],
                                  buf.at[slot, k, pl.ds(c * rows, rows)], sem.at[slot])
            for c in range(WEIGHT_DMA_CHUNKS)]


def _start_weight(*args):
    for cp in _weight_copies(*args):
        cp.start(priority=WEIGHT_DMA_PRIORITY)


def _wait_weight(*args):
    for cp in _weight_copies(*args):
        cp.wait()


def _build_row_sources(dest_ref, row_ref):
    n_tokens = dest_ref.shape[0] // TOP_K

    def scatter(i, carry):
        for u in range(SCALAR_UNROLL):
            tok = i * SCALAR_UNROLL + u
            for k in range(TOP_K):
                row_ref[dest_ref[k * n_tokens + tok]] = tok * ROW_SUB
        return carry

    lax.fori_loop(0, n_tokens // SCALAR_UNROLL, scatter, 0)


def _gate_up_kernel(blk_e_ref, first_ref, wslot_ref, nxt_e_ref, n_used_ref, n_valid_ref, dest_ref,
                    h2_hbm, wg_hbm, wu_hbm, hid_ref,
                    xg, wbuf, row_ref, gsem, wsem):
    b = pl.program_id(0)
    n_used = n_used_ref[0]
    e = blk_e_ref[b]
    slot = wslot_ref[b]

    @pl.when(b == 0)
    def _():
        _start_weight(wg_hbm, e, wbuf, slot, 0, wsem)
        _start_weight(wu_hbm, e, wbuf, slot, 1, wsem)
        xg[...] = jnp.zeros_like(xg)
        _build_row_sources(dest_ref, row_ref)
        _start_row_gather(h2_hbm, row_ref, 0, n_valid_ref[0], xg.at[0], gsem.at[0])

    @pl.when(b < n_used)
    def _():
        cur = b % 2

        @pl.when(b + 1 < n_used)
        def _():
            _start_row_gather(h2_hbm, row_ref, (b + 1) * ROW_BLK, n_valid_ref[b + 1],
                              xg.at[1 - cur], gsem.at[1 - cur])

        @pl.when(first_ref[b] == 1)
        def _():
            _wait_weight(wg_hbm, e, wbuf, slot, 0, wsem)
            _wait_weight(wu_hbm, e, wbuf, slot, 1, wsem)
            nxt = nxt_e_ref[b]

            @pl.when(nxt != e)
            def _():
                _start_weight(wg_hbm, nxt, wbuf, 1 - slot, 0, wsem)
                _start_weight(wu_hbm, nxt, wbuf, 1 - slot, 1, wsem)


        _wait_row_gather(h2_hbm, xg.at[cur], gsem.at[cur], n_valid_ref[b])
        xb = _load_rows(xg.at[cur], ROW_BLK)
        gate = _bdot(xb, wbuf[slot, 0].astype(jnp.bfloat16))
        up = _bdot(xb, wbuf[slot, 1].astype(jnp.bfloat16))
        hid_ref[...] = (gate * jax.nn.sigmoid(gate) * up).astype(jnp.bfloat16)

    @pl.when(b >= n_used)
    def _():
        hid_ref[...] = jnp.zeros_like(hid_ref)


def _gate_up(plan, h2, wg, wu):
    n_blocks = plan["blk_e"].shape[0]
    any_spec = pl.BlockSpec(memory_space=pl.ANY)
    return pl.pallas_call(
        _gate_up_kernel,
        grid_spec=pltpu.PrefetchScalarGridSpec(
            num_scalar_prefetch=7,
            grid=(n_blocks,),
            in_specs=[any_spec, any_spec, any_spec],
            out_specs=pl.BlockSpec((ROW_BLK, D_EXPERT), lambda b, *_: (b, 0)),
            scratch_shapes=[
                pltpu.VMEM((2, ROW_BLK * ROW_SUB, LANES), jnp.bfloat16),
                pltpu.VMEM((2, 2, D_MODEL, D_EXPERT), jnp.float32),
                pltpu.SMEM((n_blocks * ROW_BLK,), jnp.int32),
                pltpu.SemaphoreType.DMA((2,)),
                pltpu.SemaphoreType.DMA((2,)),
            ],
        ),
        out_shape=jax.ShapeDtypeStruct((n_blocks * ROW_BLK, D_EXPERT), jnp.bfloat16),
        compiler_params=pltpu.CompilerParams(
            dimension_semantics=("arbitrary",), vmem_limit_bytes=VMEM_LIMIT),
        name="gate_up",
    )(plan["blk_e"], plan["first"], plan["wslot"], plan["nxt_e"], plan["n_used"], plan["n_valid"],
      plan["dest"], h2, wg, wu)


def _down_kernel(blk_e_ref, first_ref, wslot_ref, nxt_e_ref, n_used_ref,
                 hid_ref, wd_hbm, y_ref, wbuf, wsem):
    b = pl.program_id(0)
    n_used = n_used_ref[0]
    e = blk_e_ref[b]
    slot = wslot_ref[b]

    @pl.when(b == 0)
    def _():
        _start_weight(wd_hbm, e, wbuf, slot, 0, wsem)

    @pl.when(b < n_used)
    def _():
        @pl.when(first_ref[b] == 1)
        def _():
            _wait_weight(wd_hbm, e, wbuf, slot, 0, wsem)
            nxt = nxt_e_ref[b]

            @pl.when(nxt != e)
            def _():
                _start_weight(wd_hbm, nxt, wbuf, 1 - slot, 0, wsem)


        _store_rows(y_ref, _bdot(hid_ref[...], wbuf[slot, 0].astype(jnp.bfloat16)))

    @pl.when(b >= n_used)
    def _():
        y_ref[...] = jnp.zeros_like(y_ref)


def _down(plan, hid, wd):
    n_blocks = plan["blk_e"].shape[0]
    return pl.pallas_call(
        _down_kernel,
        grid_spec=pltpu.PrefetchScalarGridSpec(
            num_scalar_prefetch=5,
            grid=(n_blocks,),
            in_specs=[pl.BlockSpec((ROW_BLK, D_EXPERT), lambda b, *_: (b, 0)),
                      pl.BlockSpec(memory_space=pl.ANY)],
            out_specs=pl.BlockSpec((ROW_BLK * ROW_SUB, LANES), lambda b, *_: (b, 0)),
            scratch_shapes=[
                pltpu.VMEM((2, 1, D_EXPERT, D_MODEL), jnp.float32),
                pltpu.SemaphoreType.DMA((2,)),
            ],
        ),
        out_shape=jax.ShapeDtypeStruct((n_blocks * ROW_BLK * ROW_SUB, LANES), jnp.bfloat16),
        compiler_params=pltpu.CompilerParams(
            dimension_semantics=("arbitrary",), vmem_limit_bytes=VMEM_LIMIT),
        name="down",
    )(plan["blk_e"], plan["first"], plan["wslot"], plan["nxt_e"], plan["n_used"], hid, wd)


def _combine_kernel(dest_ref, x1_ref, route_ref, g_ref, y_hbm, o_ref, y0_buf, y1_buf, sem):
    i = pl.program_id(0)
    n = pl.num_programs(0)
    cur = i % 2
    n_tokens = n * CTM

    def start(step, slot):
        _start_row_gather(y_hbm, dest_ref, step * CTM, CTM, y0_buf.at[slot], sem.at[0, slot])
        _start_row_gather(y_hbm, dest_ref, n_tokens + step * CTM, CTM, y1_buf.at[slot], sem.at[1, slot])

    @pl.when(i == 0)
    def _():
        start(0, 0)

    @pl.when(i + 1 < n)
    def _():
        start(i + 1, 1 - cur)

    r = route_ref[...]
    _wait_row_gather(y_hbm, y0_buf.at[cur], sem.at[0, cur], CTM)
    _wait_row_gather(y_hbm, y1_buf.at[cur], sem.at[1, cur], CTM)
    y0 = _load_rows(y0_buf.at[cur], CTM).astype(jnp.float32)
    y1 = _load_rows(y1_buf.at[cur], CTM).astype(jnp.float32)
    xo = x1_ref[...] + (y0 * r[:, 0:1] + y1 * r[:, 1:2])
    o_ref[...] = _rms(xo, g_ref[...])


def _combine(dest_rows, x1, route, g, ybuf):
    t = x1.shape[0]
    tile = lambda w: pl.BlockSpec((CTM, w), lambda i, *_: (i, 0))
    return pl.pallas_call(
        _combine_kernel,
        grid_spec=pltpu.PrefetchScalarGridSpec(
            num_scalar_prefetch=1,
            grid=(t // CTM,),
            in_specs=[tile(D_MODEL), tile(LANES),
                      pl.BlockSpec((1, D_MODEL), lambda i, *_: (0, 0)),
                      pl.BlockSpec(memory_space=pl.ANY)],
            out_specs=tile(D_MODEL),
            scratch_shapes=[pltpu.VMEM((2, CTM * ROW_SUB, LANES), jnp.bfloat16),
                            pltpu.VMEM((2, CTM * ROW_SUB, LANES), jnp.bfloat16),
                            pltpu.SemaphoreType.DMA((2, 2))],
        ),
        out_shape=jax.ShapeDtypeStruct((t, D_MODEL), jnp.float32),
        compiler_params=pltpu.CompilerParams(
            dimension_semantics=("arbitrary",), vmem_limit_bytes=VMEM_LIMIT),
        name="combine",
    )(dest_rows, x1, route, g, ybuf)


def _dispatch_plan(e_flat):
    a = e_flat.shape[0]
    i32 = jnp.int32
    experts = jnp.arange(N_EXPERTS, dtype=i32)
    onehot = (e_flat[:, None] == experts[None, :]).astype(i32)
    rank = jnp.cumsum(onehot, axis=0) - onehot
    counts = jnp.sum(onehot, axis=0)
    padded = ((counts + ROW_BLK - 1) // ROW_BLK) * ROW_BLK
    pends = jnp.cumsum(padded)
    pstarts = pends - padded
    dest = jnp.sum((rank + pstarts[None, :]) * onehot, axis=1).astype(i32)
    n_blocks = (a + N_EXPERTS * (ROW_BLK - 1) + ROW_BLK - 1) // ROW_BLK
    blk_start = jnp.arange(n_blocks, dtype=i32) * ROW_BLK
    blk_e = jnp.clip(jnp.sum(pends[None, :] <= blk_start[:, None], axis=1), 0, N_EXPERTS - 1).astype(i32)
    first = jnp.concatenate([jnp.ones((1,), i32), (blk_e[1:] != blk_e[:-1]).astype(i32)])
    wslot = (jnp.cumsum(first) - 1) % 2
    later_used = (experts[None, :] > experts[:, None]) & (counts[None, :] > 0)
    nxt = jnp.min(jnp.where(later_used, experts[None, :], N_EXPERTS), axis=1)
    nxt = jnp.where(nxt == N_EXPERTS, experts, nxt)
    return {
        "dest": dest,
        "blk_e": blk_e,
        "first": first,
        "wslot": wslot.astype(i32),
        "nxt_e": nxt[blk_e].astype(i32),
        "n_used": (pends[-1] // ROW_BLK).astype(i32).reshape(1),
        "n_valid": jnp.clip(counts[blk_e] - (blk_start - pstarts[blk_e]), 0, ROW_BLK).astype(i32),
    }


def kernel(x, norm_mix_g, w_in, w_pool, pool_scale, conv_w, w_out, norm_ffn_g, w_router_group, b_router_group, w_router_expert, b_router_expert, w_gate, w_up, w_down, norm_final_g):
    bt, s, d = x.shape
    t = bt * s
    bf = jnp.bfloat16
    xt = x.reshape(t, d)
    assert norm_mix_g.shape[0] == 1, "the final norm is fused into the last layer's combine"
    for l in range(norm_mix_g.shape[0]):
        w_r = jnp.concatenate([w_router_group[l], w_router_expert[l]], axis=1)
        w_r = jnp.pad(w_r, ((0, 0), (0, LANES - w_r.shape[1]))).astype(bf)
        b_r = jnp.concatenate([b_router_group[l], b_router_expert[l]])
        b_r = jnp.pad(b_r, (0, LANES - b_r.shape[0])).reshape(1, LANES)
        x1, h2, route = _mixer(
            xt, norm_mix_g[l].reshape(1, d), w_in[l], w_pool[l].astype(bf),
            pool_scale[l].reshape(1, MIX_A), conv_w[l].T, w_out[l],
            norm_ffn_g[l].reshape(1, d), w_r, b_r)
        e_flat = jnp.concatenate([route[:, 2 + k] for k in range(TOP_K)]).astype(jnp.int32)
        plan = _dispatch_plan(e_flat)
        hid = _gate_up(plan, h2, w_gate[l], w_up[l])
        ybuf = _down(plan, hid, w_down[l])
        xt = _combine(plan["dest"] * ROW_SUB, x1, route, norm_final_g.reshape(1, d), ybuf)
    return xt.reshape(bt, s, d)
```

```python
import jax
import jax.numpy as jnp
from jax import lax
from jax.experimental import pallas as pl
from jax.experimental.pallas import tpu as pltpu

D_MODEL = 2048
MIX_A = 1024
MIX_B = 1024
POOL_WINDOWS = (2, 4, 8, 16)
POOL_CH = MIX_A // len(POOL_WINDOWS)
CONV_W = 3
N_GROUPS = 4
E_PER_GROUP = 8
N_EXPERTS = N_GROUPS * E_PER_GROUP
TOP_K = 2
D_EXPERT = D_MODEL // 2
EPS = 1e-6

LANES = 128
ROW_SUB = D_MODEL // LANES
HIST = 16
TM = 256
CTM = 256
ROW_BLK = 256
VMEM_LIMIT = 56 * 1024 * 1024
WEIGHT_DMA_PRIORITY = 1
WEIGHT_DMA_CHUNKS = 8
WEIGHT_SLAB = 32
WEIGHT_LAND_SLOTS = 16
DMA_QUEUES = 2
GATHER_SHIFT = 4
GATHER_UNROLL = 1 << GATHER_SHIFT
SCALAR_UNROLL = 16

_NEG = -1e30


def _rms(x, g):
    return x * lax.rsqrt(jnp.mean(x * x, axis=-1, keepdims=True) + EPS) * g


def _bdot(a, b):
    return jnp.dot(a, b, preferred_element_type=jnp.float32)


def _store_rows(ref, val):
    ref[...] = val.astype(jnp.bfloat16).reshape(val.shape[0] * ROW_SUB, LANES)


def _load_rows(ref, n):
    return ref[...].reshape(n, D_MODEL)


def _load_cast_weight(w_hbm, w_bf, land, sem):
    rows, cols = w_hbm.shape
    n_slabs = rows // WEIGHT_SLAB
    ahead = WEIGHT_LAND_SLOTS - 1

    def slab_copy(c):
        k = c % WEIGHT_LAND_SLOTS
        return pltpu.make_async_copy(w_hbm.at[pl.ds(c * WEIGHT_SLAB, WEIGHT_SLAB)],
                                     land.at[k, :, pl.ds(0, cols)], sem.at[k])

    for c in range(min(ahead, n_slabs)):
        slab_copy(c).start()
    for c in range(n_slabs):
        if c + ahead < n_slabs:
            slab_copy(c + ahead).start()
        slab_copy(c).wait()
        w_bf[pl.ds(c * WEIGHT_SLAB, WEIGHT_SLAB), :] = (
            land[c % WEIGHT_LAND_SLOTS, :, 0:cols].astype(jnp.bfloat16))


def _mixer_kernel(x_ref, x_next_ref, g1_ref, w_in_hbm, w_pool_ref, pscale_ref, convw_ref, w_out_hbm,
                  g2_ref, w_r_ref, b_r_ref,
                  x1_ref, h2_ref, route_ref,
                  ext_u, ext_z, h_buf, w_in_ref, w_out_ref, land, wsem):
    i = pl.program_id(0)

    @pl.when(i == 0)
    def _():
        ext_u[0:HIST, :] = jnp.zeros((HIST, MIX_A), jnp.float32)
        ext_z[0:HIST, :] = jnp.zeros((HIST, MIX_B), jnp.float32)
        h_buf[...] = _rms(x_ref[...], g1_ref[...]).astype(jnp.bfloat16)
        _load_cast_weight(w_in_hbm, w_in_ref, land, wsem)
        _load_cast_weight(w_out_hbm, w_out_ref, land, wsem)

    x = x_ref[...]
    h = h_buf[...]

    ext_u[HIST:HIST + TM, :] = _bdot(h, w_in_ref[:, 0:MIX_A])
    c_gate = _bdot(h, w_in_ref[:, MIX_A + MIX_B:MIX_A + 2 * MIX_B])
    v = _bdot(h, w_in_ref[:, MIX_A + 2 * MIX_B:MIX_A + 3 * MIX_B])
    z = c_gate * v
    ext_z[HIST:HIST + TM, :] = z
    b_gate = _bdot(h, w_in_ref[:, MIX_A:MIX_A + MIX_B])

    y = ext_z[HIST - 2:HIST - 2 + TM, :] * convw_ref[0:1, :]
    y = y + ext_z[HIST - 1:HIST - 1 + TM, :] * convw_ref[1:2, :]
    y = y + z * convw_ref[2:3, :]
    y_b = b_gate * y
    ext_z[0:HIST, :] = ext_z[TM:TM + HIST, :]
    out_b = _bdot(y_b.astype(jnp.bfloat16), w_out_ref[MIX_A:MIX_A + MIX_B, :])

    row = lax.broadcasted_iota(jnp.int32, (TM, 1), 0) + i * TM + 1
    y_a = []
    for gi, w in enumerate(POOL_WINDOWS):
        c0 = gi * POOL_CH
        u = ext_u[HIST:HIST + TM, c0:c0 + POOL_CH]
        acc = u
        for s in range(1, w):
            acc = acc + ext_u[HIST - s:HIST - s + TM, c0:c0 + POOL_CH]
        cnt = jnp.minimum(row, w).astype(jnp.float32)
        pooled = (acc / cnt - u).astype(jnp.bfloat16)
        y_a.append(_bdot(pooled, w_pool_ref[gi]))
    y_a = jnp.concatenate(y_a, axis=-1) * pscale_ref[...]
    ext_u[0:HIST, :] = ext_u[TM:TM + HIST, :]
    out_a = _bdot(y_a.astype(jnp.bfloat16), w_out_ref[0:MIX_A, :])

    x1 = x + (out_a + out_b)
    x1_ref[...] = x1

    h2_f32 = _rms(x1, g2_ref[...])
    _store_rows(h2_ref, h2_f32)
    h2 = h2_f32.astype(jnp.bfloat16)

    logits = _bdot(h2, w_r_ref[...]) + b_r_ref[...]
    lane = lax.broadcasted_iota(jnp.int32, (TM, LANES), 1)
    lane_f = lane.astype(jnp.float32)

    def first_argmax(vals, vmax):
        return jnp.min(jnp.where(vals == vmax, lane_f, float(LANES)), axis=-1, keepdims=True)

    gl = jnp.where(lane < N_GROUPS, logits, _NEG)
    gmax = jnp.max(gl, axis=-1, keepdims=True)
    g_w = 1.0 / jnp.sum(jnp.exp(gl - gmax), axis=-1, keepdims=True)
    grp = first_argmax(gl, gmax).astype(jnp.int32)
    lo = N_GROUPS + E_PER_GROUP * grp
    el = jnp.where((lane >= lo) & (lane < lo + E_PER_GROUP), logits, _NEG)
    emax = jnp.max(el, axis=-1, keepdims=True)
    idx1 = first_argmax(el, emax)
    esum = jnp.sum(jnp.exp(el - emax), axis=-1, keepdims=True)
    el2 = jnp.where(lane_f == idx1, _NEG, el)
    e2max = jnp.max(el2, axis=-1, keepdims=True)
    idx2 = first_argmax(el2, e2max)
    p1 = 1.0 / esum
    p2 = jnp.exp(e2max - emax) / esum
    tot = p1 + p2
    w1 = g_w * (p1 / tot)
    w2 = g_w * (p2 / tot)
    route = jnp.where(lane == 0, w1, 0.0)
    route = jnp.where(lane == 1, w2, route)
    route = jnp.where(lane == 2, idx1 - N_GROUPS, route)
    route = jnp.where(lane == 3, idx2 - N_GROUPS, route)
    route_ref[...] = route

    h_buf[...] = _rms(x_next_ref[...], g1_ref[...]).astype(jnp.bfloat16)


def _mixer(x, g1, w_in, w_pool, pscale, convw, w_out, g2, w_r, b_r):
    t = x.shape[0]
    const = lambda shape: pl.BlockSpec(shape, lambda i: (0,) * len(shape),
                                       pipeline_mode=pl.Buffered(1))
    return pl.pallas_call(
        _mixer_kernel,
        grid=(t // TM,),
        in_specs=[
            pl.BlockSpec((TM, D_MODEL), lambda i: (i, 0)),
            pl.BlockSpec((TM, D_MODEL), lambda i: (jnp.minimum(i + 1, t // TM - 1), 0)),
            const((1, D_MODEL)),
            pl.BlockSpec(memory_space=pl.ANY),
            const(w_pool.shape),
            const((1, MIX_A)),
            const((CONV_W, MIX_B)),
            pl.BlockSpec(memory_space=pl.ANY),
            const((1, D_MODEL)),
            const(w_r.shape),
            const((1, LANES)),
        ],
        out_specs=[
            pl.BlockSpec((TM, D_MODEL), lambda i: (i, 0)),
            pl.BlockSpec((TM * ROW_SUB, LANES), lambda i: (i, 0)),
            pl.BlockSpec((TM, LANES), lambda i: (i, 0)),
        ],
        out_shape=[
            jax.ShapeDtypeStruct((t, D_MODEL), jnp.float32),
            jax.ShapeDtypeStruct((t * ROW_SUB, LANES), jnp.bfloat16),
            jax.ShapeDtypeStruct((t, LANES), jnp.float32),
        ],
        scratch_shapes=[
            pltpu.VMEM((TM + HIST, MIX_A), jnp.float32),
            pltpu.VMEM((TM + HIST, MIX_B), jnp.float32),
            pltpu.VMEM((TM, D_MODEL), jnp.bfloat16),
            pltpu.VMEM(w_in.shape, jnp.bfloat16),
            pltpu.VMEM(w_out.shape, jnp.bfloat16),
            pltpu.VMEM((WEIGHT_LAND_SLOTS, WEIGHT_SLAB, w_in.shape[1]), jnp.float32),
            pltpu.SemaphoreType.DMA((WEIGHT_LAND_SLOTS,)),
        ],
        compiler_params=pltpu.CompilerParams(
            dimension_semantics=("arbitrary",), vmem_limit_bytes=VMEM_LIMIT),
        name="mixer",
    )(x, x, g1, w_in, w_pool, pscale, convw, w_out, g2, w_r, b_r)


def _row_copy(src_hbm, row_start, dst_ref, r, sem):
    dst_start = r * ROW_SUB if isinstance(r, int) else pl.multiple_of(r * ROW_SUB, ROW_SUB)
    return pltpu.make_async_copy(src_hbm.at[pl.ds(pl.multiple_of(row_start, ROW_SUB), ROW_SUB)],
                                 dst_ref.at[pl.ds(dst_start, ROW_SUB)], sem)


def _gather_rows_issued(n_rows):
    if isinstance(n_rows, int):
        return n_rows
    return ((n_rows + GATHER_UNROLL - 1) >> GATHER_SHIFT) << GATHER_SHIFT


def _start_row_gather(src_hbm, idx_ref, base, n_rows, dst_ref, sem):
    if isinstance(n_rows, int):
        for r in range(n_rows):
            _row_copy(src_hbm, idx_ref[base + r], dst_ref, r, sem).start(priority=r % DMA_QUEUES)
        return

    def body(i, carry):
        for u in range(GATHER_UNROLL):
            r = i * GATHER_UNROLL + u
            src_row = idx_ref[base + jnp.minimum(r, n_rows - 1)]
            _row_copy(src_hbm, src_row, dst_ref, r, sem).start(priority=u % DMA_QUEUES)
        return carry

    lax.fori_loop(0, _gather_rows_issued(n_rows) >> GATHER_SHIFT, body, 0)


def _wait_row_gather(src_hbm, dst_ref, sem, n_rows):
    n = _gather_rows_issued(n_rows) * ROW_SUB
    pltpu.make_async_copy(src_hbm.at[pl.ds(0, n)], dst_ref.at[pl.ds(0, n)], sem).wait()


def _weight_copies(w_hbm, e, buf, slot, k, sem):
    rows = w_hbm.shape[1] // WEIGHT_DMA_CHUNKS
    return [pltpu.make_async_copy(w_hbm.at[e, pl.ds(c * rows, rows)],
                                  buf.at[slot, k, pl.ds(c * rows, rows)], sem.at[slot])
            for c in range(WEIGHT_DMA_CHUNKS)]


def _start_weight(*args):
    for cp in _weight_copies(*args):
        cp.start(priority=WEIGHT_DMA_PRIORITY)


def _wait_weight(*args):
    for cp in _weight_copies(*args):
        cp.wait()


def _build_row_sources(dest_ref, row_ref):
    n_tokens = dest_ref.shape[0] // TOP_K

    def scatter(i, carry):
        for u in range(SCALAR_UNROLL):
            tok = i * SCALAR_UNROLL + u
            for k in range(TOP_K):
                row_ref[dest_ref[k * n_tokens + tok]] = tok * ROW_SUB
        return carry

    lax.fori_loop(0, n_tokens // SCALAR_UNROLL, scatter, 0)


def _gate_up_kernel(blk_e_ref, first_ref, wslot_ref, nxt_e_ref, n_used_ref, n_valid_ref, dest_ref,
                    h2_hbm, wg_hbm, wu_hbm, hid_ref,
                    xg, wbuf, row_ref, gsem, wsem):
    b = pl.program_id(0)
    n_used = n_used_ref[0]
    e = blk_e_ref[b]
    slot = wslot_ref[b]

    @pl.when(b == 0)
    def _():
        _start_weight(wg_hbm, e, wbuf, slot, 0, wsem)
        _start_weight(wu_hbm, e, wbuf, slot, 1, wsem)
        xg[...] = jnp.zeros_like(xg)
        _build_row_sources(dest_ref, row_ref)
        _start_row_gather(h2_hbm, row_ref, 0, n_valid_ref[0], xg.at[0], gsem.at[0])

    @pl.when(b < n_used)
    def _():
        cur = b % 2

        @pl.when(b + 1 < n_used)
        def _():
            _start_row_gather(h2_hbm, row_ref, (b + 1) * ROW_BLK, n_valid_ref[b + 1],
                              xg.at[1 - cur], gsem.at[1 - cur])

        @pl.when(first_ref[b] == 1)
        def _():
            _wait_weight(wg_hbm, e, wbuf, slot, 0, wsem)
            _wait_weight(wu_hbm, e, wbuf, slot, 1, wsem)
            nxt = nxt_e_ref[b]

            @pl.when(nxt != e)
            def _():
                _start_weight(wg_hbm, nxt, wbuf, 1 - slot, 0, wsem)
                _start_weight(wu_hbm, nxt, wbuf, 1 - slot, 1, wsem)


        _wait_row_gather(h2_hbm, xg.at[cur], gsem.at[cur], n_valid_ref[b])
        xb = _load_rows(xg.at[cur], ROW_BLK)
        gate = _bdot(xb, wbuf[slot, 0].astype(jnp.bfloat16))
        up = _bdot(xb, wbuf[slot, 1].astype(jnp.bfloat16))
        hid_ref[...] = (gate * jax.nn.sigmoid(gate) * up).astype(jnp.bfloat16)

    @pl.when(b >= n_used)
    def _():
        hid_ref[...] = jnp.zeros_like(hid_ref)


def _gate_up(plan, h2, wg, wu):
    n_blocks = plan["blk_e"].shape[0]
    any_spec = pl.BlockSpec(memory_space=pl.ANY)
    return pl.pallas_call(
        _gate_up_kernel,
        grid_spec=pltpu.PrefetchScalarGridSpec(
            num_scalar_prefetch=7,
            grid=(n_blocks,),
            in_specs=[any_spec, any_spec, any_spec],
            out_specs=pl.BlockSpec((ROW_BLK, D_EXPERT), lambda b, *_: (b, 0)),
            scratch_shapes=[
                pltpu.VMEM((2, ROW_BLK * ROW_SUB, LANES), jnp.bfloat16),
                pltpu.VMEM((2, 2, D_MODEL, D_EXPERT), jnp.float32),
                pltpu.SMEM((n_blocks * ROW_BLK,), jnp.int32),
                pltpu.SemaphoreType.DMA((2,)),
                pltpu.SemaphoreType.DMA((2,)),
            ],
        ),
        out_shape=jax.ShapeDtypeStruct((n_blocks * ROW_BLK, D_EXPERT), jnp.bfloat16),
        compiler_params=pltpu.CompilerParams(
            dimension_semantics=("arbitrary",), vmem_limit_bytes=VMEM_LIMIT),
        name="gate_up",
    )(plan["blk_e"], plan["first"], plan["wslot"], plan["nxt_e"], plan["n_used"], plan["n_valid"],
      plan["dest"], h2, wg, wu)


def _down_kernel(blk_e_ref, first_ref, wslot_ref, nxt_e_ref, n_used_ref,
                 hid_ref, wd_hbm, y_ref, wbuf, wsem):
    b = pl.program_id(0)
    n_used = n_used_ref[0]
    e = blk_e_ref[b]
    slot = wslot_ref[b]

    @pl.when(b == 0)
    def _():
        _start_weight(wd_hbm, e, wbuf, slot, 0, wsem)

    @pl.when(b < n_used)
    def _():
        @pl.when(first_ref[b] == 1)
        def _():
            _wait_weight(wd_hbm, e, wbuf, slot, 0, wsem)
            nxt = nxt_e_ref[b]

            @pl.when(nxt != e)
            def _():
                _start_weight(wd_hbm, nxt, wbuf, 1 - slot, 0, wsem)


        _store_rows(y_ref, _bdot(hid_ref[...], wbuf[slot, 0].astype(jnp.bfloat16)))

    @pl.when(b >= n_used)
    def _():
        y_ref[...] = jnp.zeros_like(y_ref)


def _down(plan, hid, wd):
    n_blocks = plan["blk_e"].shape[0]
    return pl.pallas_call(
        _down_kernel,
        grid_spec=pltpu.PrefetchScalarGridSpec(
            num_scalar_prefetch=5,
            grid=(n_blocks,),
            in_specs=[pl.BlockSpec((ROW_BLK, D_EXPERT), lambda b, *_: (b, 0)),
                      pl.BlockSpec(memory_space=pl.ANY)],
            out_specs=pl.BlockSpec((ROW_BLK * ROW_SUB, LANES), lambda b, *_: (b, 0)),
            scratch_shapes=[
                pltpu.VMEM((2, 1, D_EXPERT, D_MODEL), jnp.float32),
                pltpu.SemaphoreType.DMA((2,)),
            ],
        ),
        out_shape=jax.ShapeDtypeStruct((n_blocks * ROW_BLK * ROW_SUB, LANES), jnp.bfloat16),
        compiler_params=pltpu.CompilerParams(
            dimension_semantics=("arbitrary",), vmem_limit_bytes=VMEM_LIMIT),
        name="down",
    )(plan["blk_e"], plan["first"], plan["wslot"], plan["nxt_e"], plan["n_used"], hid, wd)


def _combine_kernel(dest_ref, x1_ref, route_ref, g_ref, y_hbm, o_ref, y0_buf, y1_buf, sem):
    i = pl.program_id(0)
    n = pl.num_programs(0)
    cur = i % 2
    n_tokens = n * CTM

    def start(step, slot):
        _start_row_gather(y_hbm, dest_ref, step * CTM, CTM, y0_buf.at[slot], sem.at[0, slot])
        _start_row_gather(y_hbm, dest_ref, n_tokens + step * CTM, CTM, y1_buf.at[slot], sem.at[1, slot])

    @pl.when(i == 0)
    def _():
        start(0, 0)

    @pl.when(i + 1 < n)
    def _():
        start(i + 1, 1 - cur)

    r = route_ref[...]
    _wait_row_gather(y_hbm, y0_buf.at[cur], sem.at[0, cur], CTM)
    _wait_row_gather(y_hbm, y1_buf.at[cur], sem.at[1, cur], CTM)
    y0 = _load_rows(y0_buf.at[cur], CTM).astype(jnp.float32)
    y1 = _load_rows(y1_buf.at[cur], CTM).astype(jnp.float32)
    xo = x1_ref[...] + (y0 * r[:, 0:1] + y1 * r[:, 1:2])
    o_ref[...] = _rms(xo, g_ref[...])


def _combine(dest_rows, x1, route, g, ybuf):
    t = x1.shape[0]
    tile = lambda w: pl.BlockSpec((CTM, w), lambda i, *_: (i, 0))
    return pl.pallas_call(
        _combine_kernel,
        grid_spec=pltpu.PrefetchScalarGridSpec(
            num_scalar_prefetch=1,
            grid=(t // CTM,),
            in_specs=[tile(D_MODEL), tile(LANES),
                      pl.BlockSpec((1, D_MODEL), lambda i, *_: (0, 0)),
                      pl.BlockSpec(memory_space=pl.ANY)],
            out_specs=tile(D_MODEL),
            scratch_shapes=[pltpu.VMEM((2, CTM * ROW_SUB, LANES), jnp.bfloat16),
                            pltpu.VMEM((2, CTM * ROW_SUB, LANES), jnp.bfloat16),
                            pltpu.SemaphoreType.DMA((2, 2))],
        ),
        out_shape=jax.ShapeDtypeStruct((t, D_MODEL), jnp.float32),
        compiler_params=pltpu.CompilerParams(
            dimension_semantics=("arbitrary",), vmem_limit_bytes=VMEM_LIMIT),
        name="combine",
    )(dest_rows, x1, route, g, ybuf)


def _dispatch_plan(e_flat):
    a = e_flat.shape[0]
    i32 = jnp.int32
    experts = jnp.arange(N_EXPERTS, dtype=i32)
    onehot = (e_flat[:, None] == experts[None, :]).astype(i32)
    rank = jnp.cumsum(onehot, axis=0) - onehot
    counts = jnp.sum(onehot, axis=0)
    padded = ((counts + ROW_BLK - 1) // ROW_BLK) * ROW_BLK
    pends = jnp.cumsum(padded)
    pstarts = pends - padded
    dest = jnp.sum((rank + pstarts[None, :]) * onehot, axis=1).astype(i32)
    n_blocks = (a + N_EXPERTS * (ROW_BLK - 1) + ROW_BLK - 1) // ROW_BLK
    blk_start = jnp.arange(n_blocks, dtype=i32) * ROW_BLK
    blk_e = jnp.clip(jnp.sum(pends[None, :] <= blk_start[:, None], axis=1), 0, N_EXPERTS - 1).astype(i32)
    first = jnp.concatenate([jnp.ones((1,), i32), (blk_e[1:] != blk_e[:-1]).astype(i32)])
    wslot = (jnp.cumsum(first) - 1) % 2
    later_used = (experts[None, :] > experts[:, None]) & (counts[None, :] > 0)
    nxt = jnp.min(jnp.where(later_used, experts[None, :], N_EXPERTS), axis=1)
    nxt = jnp.where(nxt == N_EXPERTS, experts, nxt)
    return {
        "dest": dest,
        "blk_e": blk_e,
        "first": first,
        "wslot": wslot.astype(i32),
        "nxt_e": nxt[blk_e].astype(i32),
        "n_used": (pends[-1] // ROW_BLK).astype(i32).reshape(1),
        "n_valid": jnp.clip(counts[blk_e] - (blk_start - pstarts[blk_e]), 0, ROW_BLK).astype(i32),
    }


def kernel(x, norm_mix_g, w_in, w_pool, pool_scale, conv_w, w_out, norm_ffn_g, w_router_group, b_router_group, w_router_expert, b_router_expert, w_gate, w_up, w_down, norm_final_g):
    bt, s, d = x.shape
    t = bt * s
    bf = jnp.bfloat16
    xt = x.reshape(t, d)
    assert norm_mix_g.shape[0] == 1, "the final norm is fused into the last layer's combine"
    for l in range(norm_mix_g.shape[0]):
        w_r = jnp.concatenate([w_router_group[l], w_router_expert[l]], axis=1)
        w_r = jnp.pad(w_r, ((0, 0), (0, LANES - w_r.shape[1]))).astype(bf)
        b_r = jnp.concatenate([b_router_group[l], b_router_expert[l]])
        b_r = jnp.pad(b_r, (0, LANES - b_r.shape[0])).reshape(1, LANES)
        x1, h2, route = _mixer(
            xt, norm_mix_g[l].reshape(1, d), w_in[l], w_pool[l].astype(bf),
            pool_scale[l].reshape(1, MIX_A), conv_w[l].T, w_out[l],
            norm_ffn_g[l].reshape(1, d), w_r, b_r)
        e_flat = jnp.concatenate([route[:, 2 + k] for k in range(TOP_K)]).astype(jnp.int32)
        plan = _dispatch_plan(e_flat)
        hid = _gate_up(plan, h2, w_gate[l], w_up[l])
        ybuf = _down(plan, hid, w_down[l])
        xt = _combine(plan["dest"] * ROW_SUB, x1, route, norm_final_g.reshape(1, d), ybuf)
    return xt.reshape(bt, s, d)
```
